```python
import math
import jax, jax.numpy as jnp
from jax import lax
import numpy as np

D_MODEL = 2048
BATCH = 1
SEQ = 8192
DEPTH = 4

HEAD_DIM = 128
ROPE_THETA = 10000.0
NORM_EPS = 1e-6
D_FF = 4 * D_MODEL
N_MIX_HEADS = D_MODEL // HEAD_DIM
MOBA_HEADS = N_MIX_HEADS // 2
MOBA_BLOCK = 256
MOBA_TOPK = 3
SWA_HEADS = N_MIX_HEADS - MOBA_HEADS
SWA_KV_HEADS = 2
SWA_WINDOW = 128
FOX_HEADS = N_MIX_HEADS // 2
FOX_Q_BLOCK = 128
NSA_HEADS = N_MIX_HEADS - FOX_HEADS
NSA_KV_HEADS = 2
NSA_CMP_LEN = 32
NSA_CMP_STRIDE = 16
NSA_SLC_BLOCK = 64
NSA_SLC_TOPK = 16
NSA_WINDOW = 512
BAND_BLOCK = 128
Q_CHUNK = 64
NEG_INF = -1e30
N_EVEN = (DEPTH + 1) // 2
N_ODD = DEPTH // 2

EVEN_WIDTHS = (MOBA_HEADS * HEAD_DIM,) * 3 + (SWA_HEADS * HEAD_DIM, SWA_KV_HEADS * HEAD_DIM, SWA_KV_HEADS * HEAD_DIM)
ODD_WIDTHS = ((FOX_HEADS * HEAD_DIM,) * 3 + (FOX_HEADS, NSA_HEADS * HEAD_DIM)
              + (NSA_KV_HEADS * HEAD_DIM,) * 6 + (NSA_HEADS * 3,))
EVEN_IN = sum(EVEN_WIDTHS)
ODD_IN = sum(ODD_WIDTHS)
MIX_OUT = N_MIX_HEADS * HEAD_DIM

kernel_name = 'hybrid_moba_swa_fox_nsa_trunk'


def rms_norm(x, g):
    xf = x.astype(jnp.float32)
    y = xf * lax.rsqrt(jnp.mean(xf * xf, axis=-1, keepdims=True) + NORM_EPS)
    return (y * g.astype(jnp.float32)).astype(x.dtype)


def rope_tables(seq, dim):
    inv = 1.0 / (ROPE_THETA ** (jnp.arange(0, dim, 2, dtype=jnp.float32) / dim))
    ang = jnp.arange(seq, dtype=jnp.float32)[:, None] * inv[None, :]
    return jnp.cos(ang), jnp.sin(ang)


def apply_rope(x, cos, sin):
    x1, x2 = jnp.split(x.astype(jnp.float32), 2, axis=-1)
    c = cos[None, :, None, :]
    s = sin[None, :, None, :]
    return jnp.concatenate([x1 * c - x2 * s, x1 * s + x2 * c], axis=-1).astype(x.dtype)


def masked_softmax(s, mask):
    s = jnp.where(mask, s, NEG_INF)
    m = jnp.max(s, axis=-1, keepdims=True)
    p = jnp.where(mask, jnp.exp(s - m), 0.0)
    return p / jnp.maximum(jnp.sum(p, axis=-1, keepdims=True), 1e-30)


def split_cols(a, widths):
    cuts = [int(v) for v in np.cumsum(widths)[:-1]]
    return jnp.split(a, cuts, axis=-1)


def heads(a, n):
    return a.reshape(a.shape[0], a.shape[1], n, HEAD_DIM)


def banded_attention(q, k, v, window, sinks):
    B, S, Hq, hd = q.shape
    Hkv = k.shape[2]
    G = Hq // Hkv
    nb = S // BAND_BLOCK
    n_prev = -(-(window - 1) // BAND_BLOCK)
    pad = n_prev * BAND_BLOCK

    def band(a):
        ap = jnp.pad(a, ((0, 0), (pad, 0), (0, 0), (0, 0))).reshape(B, nb + n_prev, BAND_BLOCK, Hkv, hd)
        return jnp.concatenate([ap[:, i:i + nb] for i in range(n_prev + 1)], axis=2)

    kb, vb = band(k), band(v)
    qb = q.reshape(B, nb, BAND_BLOCK, Hkv, G, hd)
    s = jnp.einsum('bnqhgd,bnkhd->bnhgqk', qb, kb, preferred_element_type=jnp.float32) * (hd ** -0.5)
    t = jnp.arange(nb)[:, None] * BAND_BLOCK + jnp.arange(BAND_BLOCK)[None, :]
    spos = (jnp.arange(nb)[:, None] - n_prev) * BAND_BLOCK + jnp.arange((n_prev + 1) * BAND_BLOCK)[None, :]
    mask = (spos[:, None, :] <= t[:, :, None]) & (spos[:, None, :] > t[:, :, None] - window) & (spos[:, None, :] >= 0)
    mask = mask[None, :, None, None]
    s = jnp.where(mask, s, NEG_INF)
    m = jnp.max(s, axis=-1, keepdims=True)
    if sinks is not None:
        sk = sinks.astype(jnp.float32).reshape(1, 1, Hkv, G, 1, 1)
        m = jnp.maximum(m, sk)
    p = jnp.where(mask, jnp.exp(s - m), 0.0)
    den = jnp.sum(p, axis=-1, keepdims=True)
    if sinks is not None:
        den = den + jnp.exp(sk - m)
    p = p / den
    o = jnp.einsum('bnhgqk,bnkhd->bnqhgd', p.astype(v.dtype), vb)
    return o.reshape(B, S, Hq, hd)


def moba_attention(q, k, v):
    B, S, H, hd = q.shape
    nb = -(-S // MOBA_BLOCK)
    sp = nb * MOBA_BLOCK
    kpad = jnp.pad(k, ((0, 0), (0, sp - S), (0, 0), (0, 0)))
    vpad = jnp.pad(v, ((0, 0), (0, sp - S), (0, 0), (0, 0)))
    kbt = kpad.reshape(B, nb, MOBA_BLOCK, H, hd).transpose(0, 3, 1, 2, 4)
    vbt = vpad.reshape(B, nb, MOBA_BLOCK, H, hd).transpose(0, 3, 1, 2, 4)
    kmean = jnp.mean(kbt.astype(jnp.float32), axis=3)
    topk = min(MOBA_TOPK, nb)
    scale = hd ** -0.5
    n_chunks = S // Q_CHUNK
    qc = q.reshape(B, n_chunks, Q_CHUNK, H, hd).transpose(1, 0, 2, 3, 4)
    bi = jnp.arange(B)[:, None, None, None]
    hi = jnp.arange(H)[None, None, :, None]
    blk = jnp.arange(nb)

    def step(args):
        ci, qi = args
        t = ci * Q_CHUNK + jnp.arange(Q_CHUNK)
        own = (ci * Q_CHUNK) // MOBA_BLOCK
        gate = jnp.einsum('bqhd,bhnd->bqhn', qi.astype(jnp.float32), kmean)
        gate = jnp.where(blk < own, gate, NEG_INF)
        _, idx = lax.top_k(gate, topk)
        valid = idx < own
        ks = kbt[bi, hi, idx]
        vs = vbt[bi, hi, idx]
        s_sel = jnp.einsum('bqhd,bqhnkd->bqhnk', qi, ks, preferred_element_type=jnp.float32) * scale
        s_sel = jnp.where(valid[..., None], s_sel, NEG_INF).reshape(B, Q_CHUNK, H, topk * MOBA_BLOCK)
        ko = lax.dynamic_slice_in_dim(kpad, own * MOBA_BLOCK, MOBA_BLOCK, axis=1)
        vo = lax.dynamic_slice_in_dim(vpad, own * MOBA_BLOCK, MOBA_BLOCK, axis=1)
        s_own = jnp.einsum('bqhd,bkhd->bqhk', qi, ko, preferred_element_type=jnp.float32) * scale
        kpos = own * MOBA_BLOCK + jnp.arange(MOBA_BLOCK)
        s_own = jnp.where((kpos[None, :] <= t[:, None])[None, :, None, :], s_own, NEG_INF)
        p = jax.nn.softmax(jnp.concatenate([s_sel, s_own], axis=-1), axis=-1)
        p_sel = p[..., :topk * MOBA_BLOCK].reshape(B, Q_CHUNK, H, topk, MOBA_BLOCK).astype(v.dtype)
        p_own = p[..., topk * MOBA_BLOCK:].astype(v.dtype)
        return jnp.einsum('bqhnk,bqhnkd->bqhd', p_sel, vs) + jnp.einsum('bqhk,bkhd->bqhd', p_own, vo)

    o = lax.map(step, (jnp.arange(n_chunks), qc))
    return o.transpose(1, 0, 2, 3, 4).reshape(B, S, H, hd)


def forgetting_attention(q, k, v, log_f):
    B, S, H, hd = q.shape
    scale = hd ** -0.5
    cum = jnp.cumsum(log_f, axis=1).transpose(0, 2, 1)
    n_blk = S // FOX_Q_BLOCK
    qb = q.reshape(B, n_blk, FOX_Q_BLOCK, H, hd).transpose(1, 0, 2, 3, 4)
    kpos = jnp.arange(S)

    def step(args):
        bi, qi = args
        t = bi * FOX_Q_BLOCK + jnp.arange(FOX_Q_BLOCK)
        cq = lax.dynamic_slice_in_dim(cum, bi * FOX_Q_BLOCK, FOX_Q_BLOCK, axis=2)
        s = jnp.einsum('bqhd,bkhd->bhqk', qi, k, preferred_element_type=jnp.float32) * scale
        s = s + cq[..., :, None] - cum[..., None, :]
        p = masked_softmax(s, kpos[None, :] <= t[:, None])
        return jnp.einsum('bhqk,bkhd->bqhd', p.astype(v.dtype), v)

    o = lax.map(step, (jnp.arange(n_blk), qb))
    return o.transpose(1, 0, 2, 3, 4).reshape(B, S, H, hd)


def nsa_compress(a, pos, w1, w2):
    B, S, H, hd = a.shape
    n_cmp = (S - NSA_CMP_LEN) // NSA_CMP_STRIDE + 1
    idx = jnp.arange(n_cmp)[:, None] * NSA_CMP_STRIDE + jnp.arange(NSA_CMP_LEN)[None, :]
    blocks = a[:, idx] + pos[:, None, :]
    flat = blocks.transpose(0, 1, 3, 2, 4).reshape(B, n_cmp, H, NSA_CMP_LEN * hd)
    return jax.nn.gelu(flat @ w1) @ w2


def nsa_compressed_selected(q, k_cmp, v_cmp, k_slc, v_slc):
    B, S, Hq, hd = q.shape
    Hkv = k_slc.shape[2]
    G = Hq // Hkv
    n_cmp = k_cmp.shape[1]
    n_slc = S // NSA_SLC_BLOCK
    topk = min(NSA_SLC_TOPK, n_slc)
    scale = hd ** -0.5
    cmp_end = jnp.arange(n_cmp) * NSA_CMP_STRIDE + NSA_CMP_LEN - 1
    c_start = jnp.arange(n_cmp)[:, None] * NSA_CMP_STRIDE
    s_start = jnp.arange(n_slc)[None, :] * NSA_SLC_BLOCK
    overlap = ((c_start < s_start + NSA_SLC_BLOCK) & (c_start + NSA_CMP_LEN > s_start)).astype(jnp.float32)
    ksb = k_slc.reshape(B, n_slc, NSA_SLC_BLOCK, Hkv, hd).transpose(0, 3, 1, 2, 4)
    vsb = v_slc.reshape(B, n_slc, NSA_SLC_BLOCK, Hkv, hd).transpose(0, 3, 1, 2, 4)
    bi = jnp.arange(B)[:, None, None, None]
    hi = jnp.arange(Hkv)[None, :, None, None]
    blk = jnp.arange(n_slc)
    n_chunks = S // Q_CHUNK
    qc = q.reshape(B, n_chunks, Q_CHUNK, Hkv, G, hd).transpose(1, 0, 2, 3, 4, 5)

    def step(args):
        ci, qi = args
        t = ci * Q_CHUNK + jnp.arange(Q_CHUNK)
        s_c = jnp.einsum('bqhgd,bnhd->bhgqn', qi, k_cmp, preferred_element_type=jnp.float32) * scale
        p_c = masked_softmax(s_c, cmp_end[None, :] <= t[:, None])
        o_c = jnp.einsum('bhgqn,bnhd->bqhgd', p_c.astype(v_cmp.dtype), v_cmp)
        imp = jnp.einsum('bhgqn,nj->bhqj', p_c, overlap)
        cur = (t // NSA_SLC_BLOCK)[:, None]
        forced = (blk == 0) | (blk == cur) | (blk == cur - 1)
        imp = jnp.where(forced, jnp.inf, imp)
        imp = jnp.where(blk > cur, -jnp.inf, imp)
        _, sel = lax.top_k(imp, topk)
        ks = ksb[bi, hi, sel]
        vs = vsb[bi, hi, sel]
        s_s = jnp.einsum('bqhgd,bhqnkd->bhgqnk', qi, ks, preferred_element_type=jnp.float32) * scale
        kpos = sel[..., None] * NSA_SLC_BLOCK + jnp.arange(NSA_SLC_BLOCK)
        m_s = (kpos <= t[:, None, None])[:, :, None]
        p_s = masked_softmax(s_s.reshape(B, Hkv, G, Q_CHUNK, topk * NSA_SLC_BLOCK),
                             m_s.reshape(B, Hkv, 1, Q_CHUNK, topk * NSA_SLC_BLOCK)).reshape(s_s.shape)
        o_s = jnp.einsum('bhgqnk,bhqnkd->bqhgd', p_s.astype(vs.dtype), vs)
        return o_c, o_s

    o_c, o_s = lax.map(step, (jnp.arange(n_chunks), qc))
    back = lambda o: o.transpose(1, 0, 2, 3, 4, 5).reshape(B, S, Hq, hd)
    return back(o_c), back(o_s)


def even_mixer(h, w_in, sinks, w_out, cos, sin):
    B, S, _ = h.shape
    qa, ka, va, qb, kb, vb = split_cols(h @ w_in, EVEN_WIDTHS)
    qa = apply_rope(heads(qa, MOBA_HEADS), cos, sin)
    ka = apply_rope(heads(ka, MOBA_HEADS), cos, sin)
    qb = apply_rope(heads(qb, SWA_HEADS), cos, sin)
    kb = apply_rope(heads(kb, SWA_KV_HEADS), cos, sin)
    oa = moba_attention(qa, ka, heads(va, MOBA_HEADS))
    ob = banded_attention(qb, kb, heads(vb, SWA_KV_HEADS), SWA_WINDOW, sinks)
    o = jnp.concatenate([oa.reshape(B, S, -1), ob.reshape(B, S, -1)], axis=-1)
    return o @ w_out


def odd_mixer(h, w_in, forget_b, k_pos, k_w1, k_w2, v_pos, v_w1, v_w2, w_out, cos, sin):
    B, S, _ = h.shape
    qc, kc, vc, fc, qd, kcmp, vcmp, kslc, vslc, kwin, vwin, gd = split_cols(h @ w_in, ODD_WIDTHS)
    log_f = jax.nn.log_sigmoid(fc.astype(jnp.float32) + forget_b.astype(jnp.float32))
    oc = forgetting_attention(heads(qc, FOX_HEADS), heads(kc, FOX_HEADS), heads(vc, FOX_HEADS), log_f)
    qd = apply_rope(heads(qd, NSA_HEADS), cos, sin)
    k_c = nsa_compress(apply_rope(heads(kcmp, NSA_KV_HEADS), cos, sin), k_pos, k_w1, k_w2)
    v_c = nsa_compress(heads(vcmp, NSA_KV_HEADS), v_pos, v_w1, v_w2)
    o_cmp, o_slc = nsa_compressed_selected(qd, k_c, v_c,
                                           apply_rope(heads(kslc, NSA_KV_HEADS), cos, sin),
                                           heads(vslc, NSA_KV_HEADS))
    o_win = banded_attention(qd, apply_rope(heads(kwin, NSA_KV_HEADS), cos, sin),
                             heads(vwin, NSA_KV_HEADS), NSA_WINDOW, None)
    g = jax.nn.sigmoid(gd.reshape(B, S, NSA_HEADS, 3).astype(jnp.float32)).astype(qd.dtype)
    od = g[..., 0:1] * o_cmp + g[..., 1:2] * o_slc + g[..., 2:3] * o_win
    o = jnp.concatenate([oc.reshape(B, S, -1), od.reshape(B, S, -1)], axis=-1)
    return o @ w_out


def setup_inputs(seed: int = 0) -> dict:
    key = jax.random.key(seed)
    ks = jax.random.split(key, 21)
    nrm = lambda k, shape, scale: jax.random.normal(k, shape, jnp.float32) * scale
    L, hd = NSA_CMP_LEN, HEAD_DIM
    return {
        'x': nrm(ks[0], (BATCH, SEQ, D_MODEL), 1.0),
        'c': nrm(ks[1], (BATCH, D_MODEL), 1.0),
        'norm_mix_g': 1.0 + nrm(ks[2], (DEPTH, D_MODEL), 0.02),
        'norm_mlp_g': 1.0 + nrm(ks[3], (DEPTH, D_MODEL), 0.02),
        'ada_w': nrm(ks[4], (DEPTH, D_MODEL, 6 * D_MODEL), 0.5 * D_MODEL ** -0.5),
        'ada_b': nrm(ks[5], (DEPTH, 6 * D_MODEL), 0.02),
        'mlp_up': nrm(ks[6], (DEPTH, D_MODEL, D_FF), D_MODEL ** -0.5),
        'mlp_down': nrm(ks[7], (DEPTH, D_FF, D_MODEL), D_FF ** -0.5),
        'even_w_in': nrm(ks[8], (N_EVEN, D_MODEL, EVEN_IN), D_MODEL ** -0.5),
        'even_sinks': nrm(ks[9], (N_EVEN, SWA_HEADS), 0.5),
        'even_w_out': nrm(ks[10], (N_EVEN, MIX_OUT, D_MODEL), MIX_OUT ** -0.5),
        'odd_w_in': nrm(ks[11], (N_ODD, D_MODEL, ODD_IN), D_MODEL ** -0.5),
        'fox_forget_b': 3.0 + nrm(ks[12], (N_ODD, FOX_HEADS), 0.5),
        'nsa_k_pos': nrm(ks[13], (N_ODD, L, hd), 0.1),
        'nsa_k_w1': nrm(ks[14], (N_ODD, L * hd, hd), (L * hd) ** -0.5),
        'nsa_k_w2': nrm(ks[15], (N_ODD, hd, hd), hd ** -0.5),
        'nsa_v_pos': nrm(ks[16], (N_ODD, L, hd), 0.1),
        'nsa_v_w1': nrm(ks[17], (N_ODD, L * hd, hd), (L * hd) ** -0.5),
        'nsa_v_w2': nrm(ks[18], (N_ODD, hd, hd), hd ** -0.5),
        'odd_w_out': nrm(ks[19], (N_ODD, MIX_OUT, D_MODEL), MIX_OUT ** -0.5),
        'final_norm_g': 1.0 + nrm(ks[20], (D_MODEL,), 0.02),
    }


def reference(x, c, norm_mix_g, norm_mlp_g, ada_w, ada_b, mlp_up, mlp_down, even_w_in, even_sinks,
              even_w_out, odd_w_in, fox_forget_b, nsa_k_pos, nsa_k_w1, nsa_k_w2, nsa_v_pos, nsa_v_w1,
              nsa_v_w2, odd_w_out, final_norm_g):
    S = x.shape[1]
    cos, sin = rope_tables(S, HEAD_DIM)
    cond = jax.nn.silu(c)
    for i in range(DEPTH):
        mod = cond @ ada_w[i] + ada_b[i]
        sh1, sc1, g1, sh2, sc2, g2 = [m[:, None, :] for m in jnp.split(mod, 6, axis=-1)]
        h = rms_norm(x, norm_mix_g[i]) * (1.0 + sc1) + sh1
        if i % 2 == 0:
            j = i // 2
            y = even_mixer(h, even_w_in[j], even_sinks[j], even_w_out[j], cos, sin)
        else:
            j = i // 2
            y = odd_mixer(h, odd_w_in[j], fox_forget_b[j], nsa_k_pos[j], nsa_k_w1[j], nsa_k_w2[j],
                          nsa_v_pos[j], nsa_v_w1[j], nsa_v_w2[j], odd_w_out[j], cos, sin)
        x = x + g1 * y
        h = rms_norm(x, norm_mlp_g[i]) * (1.0 + sc2) + sh2
        x = x + g2 * (jnp.square(jax.nn.relu(h @ mlp_up[i])) @ mlp_down[i])
    return rms_norm(x, final_norm_g)
```

```python
import functools
import math

import numpy as np
import jax
import jax.numpy as jnp
from jax import lax
from jax.experimental import pallas as pl
from jax.experimental.pallas import tpu as pltpu

HEAD_DIM = 128
ROPE_THETA = 10000.0
NORM_EPS = 1e-6
MOBA_BLOCK = 256
MOBA_TOPK = 3
SWA_KV_HEADS = 2
SWA_WINDOW = 128
NSA_KV_HEADS = 2
NSA_CMP_LEN = 32
NSA_CMP_STRIDE = 16
NSA_SLC_BLOCK = 64
SLC_SHIFT = 6
NSA_SLC_TOPK = 16
NSA_WINDOW = 512
BAND_BLOCK = 128
NEG_INF = -1e30
SCALE = HEAD_DIM ** -0.5
LANES = 128
GROUP = 4
VMEM_LIMIT = 48 * 1024 * 1024

F32 = jnp.float32
BF16 = jnp.bfloat16

CT_PLAIN, CT_ROPE, CT_ROPE_SCALE, CT_SCALE = 0, 1, 2, 3


def _cparams(*sem):
    return pltpu.CompilerParams(dimension_semantics=sem, vmem_limit_bytes=VMEM_LIMIT)


def _dot(a, b):
    return jnp.dot(a, b, preferred_element_type=F32)


def _dot_t(a, b):
    return lax.dot_general(a, b, (((1,), (1,)), ((), ())), preferred_element_type=F32)


def _ada_kernel(c_ref, w_ref, b_ref, o_ref):
    c = c_ref[...]
    cond = c * jax.nn.sigmoid(c)
    cond8 = jnp.broadcast_to(cond, (8, cond.shape[1])).astype(BF16)
    y = _dot(cond8, w_ref[...].astype(BF16))
    o_ref[...] = y[0:1] + b_ref[...]


def _ada_mod(c, ada_w, ada_b):
    depth, d, n = ada_w.shape
    tn = 1024
    return pl.pallas_call(
        _ada_kernel,
        grid=(depth, n // tn),
        in_specs=[
            pl.BlockSpec((1, d), lambda l, j: (0, 0)),
            pl.BlockSpec((None, d, tn), lambda l, j: (l, 0, j)),
            pl.BlockSpec((None, 1, tn), lambda l, j: (l, 0, j)),
        ],
        out_specs=pl.BlockSpec((None, 1, tn), lambda l, j: (l, 0, j)),
        out_shape=jax.ShapeDtypeStruct((depth, 1, n), F32),
        compiler_params=_cparams("arbitrary", "arbitrary"),
        name="ada_mod",
    )(c, ada_w, ada_b.reshape(depth, 1, n))


def _norm_mod(x, g, sc, sh):
    y = x * lax.rsqrt(jnp.mean(x * x, axis=-1, keepdims=True) + NORM_EPS)
    return (y * g) * (1.0 + sc) + sh


def _proj_kernel(ct_ref, x_ref, g_ref, sc_ref, sh_ref, w_ref, cos_ref, sin_ref, o_ref, h_ref):
    j = pl.program_id(1)

    @pl.when(j == 0)
    def _():
        h_ref[...] = _norm_mod(x_ref[...], g_ref[...], sc_ref[...], sh_ref[...]).astype(BF16)

    y = _dot(h_ref[...], w_ref[...])
    groups = y.shape[1] // LANES
    for gi in range(groups):
        lo, hi = gi * LANES, (gi + 1) * LANES
        yg = y[:, lo:hi]
        ct = ct_ref[j * groups + gi]

        @pl.when(ct == CT_PLAIN)
        def _():
            o_ref[:, lo:hi] = yg.astype(o_ref.dtype)

        @pl.when(ct == CT_SCALE)
        def _():
            o_ref[:, lo:hi] = (yg * SCALE).astype(o_ref.dtype)

        @pl.when((ct == CT_ROPE) | (ct == CT_ROPE_SCALE))
        def _():
            mult = jnp.where(ct == CT_ROPE_SCALE, SCALE, 1.0).astype(F32)
            r = yg * cos_ref[...] + pltpu.roll(yg, HEAD_DIM // 2, 1) * sin_ref[...]
            o_ref[:, lo:hi] = (r * mult).astype(o_ref.dtype)


def _norm_proj(x, g, sc, sh, w, ctypes, cosf, sinf, out_dtype, tn):
    s, d = x.shape
    n = w.shape[1]
    tm = min(1024, s)
    grid_spec = pltpu.PrefetchScalarGridSpec(
        num_scalar_prefetch=1,
        grid=(s // tm, n // tn),
        in_specs=[
            pl.BlockSpec((tm, d), lambda i, j, ct: (i, 0)),
            pl.BlockSpec((1, d), lambda i, j, ct: (0, 0)),
            pl.BlockSpec((1, d), lambda i, j, ct: (0, 0)),
            pl.BlockSpec((1, d), lambda i, j, ct: (0, 0)),
            pl.BlockSpec((d, tn), lambda i, j, ct: (0, j)),
            pl.BlockSpec((tm, LANES), lambda i, j, ct: (i, 0)),
            pl.BlockSpec((tm, LANES), lambda i, j, ct: (i, 0)),
        ],
        out_specs=pl.BlockSpec((tm, tn), lambda i, j, ct: (i, j)),
        scratch_shapes=[pltpu.VMEM((tm, d), BF16)],
    )
    return pl.pallas_call(
        _proj_kernel,
        grid_spec=grid_spec,
        out_shape=jax.ShapeDtypeStruct((s, n), out_dtype),
        compiler_params=_cparams("arbitrary", "arbitrary"),
        name="norm_proj",
    )(ctypes, x, g, sc, sh, w, cosf, sinf)


def _outproj_kernel(a_ref, b_ref, w_ref, x_ref, g_ref, o_ref):
    ha = a_ref.shape[1]
    y = _dot(a_ref[...], w_ref[0:ha, :]) + _dot(b_ref[...], w_ref[ha:, :])
    o_ref[...] = x_ref[...] + g_ref[...] * y


def _out_proj(oa, ob, w, x, gate):
    s, d = x.shape
    ha, hb = oa.shape[1], ob.shape[1]
    tm, tn = min(1024, s), 512
    return pl.pallas_call(
        _outproj_kernel,
        grid=(s // tm, d // tn),
        in_specs=[
            pl.BlockSpec((tm, ha), lambda i, j: (i, 0)),
            pl.BlockSpec((tm, hb), lambda i, j: (i, 0)),
            pl.BlockSpec((ha + hb, tn), lambda i, j: (0, j)),
            pl.BlockSpec((tm, tn), lambda i, j: (i, j)),
            pl.BlockSpec((1, tn), lambda i, j: (0, j)),
        ],
        out_specs=pl.BlockSpec((tm, tn), lambda i, j: (i, j)),
        out_shape=jax.ShapeDtypeStruct((s, d), F32),
        compiler_params=_cparams("arbitrary", "arbitrary"),
        name="out_proj",
    )(oa, ob, w, x, gate)


def _mlp_kernel(x_ref, g_ref, sc_ref, sh_ref, up_ref, down_ref, gate_ref, o_ref, h_ref, acc_ref):
    f = pl.program_id(1)

    @pl.when(f == 0)
    def _():
        h_ref[...] = _norm_mod(x_ref[...], g_ref[...], sc_ref[...], sh_ref[...]).astype(BF16)
        acc_ref[...] = jnp.zeros_like(acc_ref)

    hid = jnp.maximum(_dot(h_ref[...], up_ref[...]), 0.0)
    acc_ref[...] += _dot((hid * hid).astype(BF16), down_ref[...])

    @pl.when(f == pl.num_programs(1) - 1)
    def _():
        o_ref[...] = x_ref[...] + gate_ref[...] * acc_ref[...]


def _mlp(x, g, sc, sh, up, down, gate):
    s, d = x.shape
    ff = up.shape[1]
    tm, tf = min(512, s), 512
    return pl.pallas_call(
        _mlp_kernel,
        grid=(s // tm, ff // tf),
        in_specs=[
            pl.BlockSpec((tm, d), lambda i, f: (i, 0)),
            pl.BlockSpec((1, d), lambda i, f: (0, 0)),
            pl.BlockSpec((1, d), lambda i, f: (0, 0)),
            pl.BlockSpec((1, d), lambda i, f: (0, 0)),
            pl.BlockSpec((d, tf), lambda i, f: (0, f)),
            pl.BlockSpec((tf, d), lambda i, f: (f, 0)),
            pl.BlockSpec((1, d), lambda i, f: (0, 0)),
        ],
        out_specs=pl.BlockSpec((tm, d), lambda i, f: (i, 0)),
        out_shape=jax.ShapeDtypeStruct((s, d), F32),
        scratch_shapes=[pltpu.VMEM((tm, d), BF16), pltpu.VMEM((tm, d), F32)],
        compiler_params=_cparams("arbitrary", "arbitrary"),
        name="mlp",
    )(x, g, sc, sh, up, down, gate)


def _final_norm_kernel(x_ref, g_ref, o_ref):
    x = x_ref[...]
    o_ref[...] = x * lax.rsqrt(jnp.mean(x * x, axis=-1, keepdims=True) + NORM_EPS) * g_ref[...]


def _final_norm(x, g):
    s, d = x.shape
    tm = min(512, s)
    return pl.pallas_call(
        _final_norm_kernel,
        grid=(s // tm,),
        in_specs=[pl.BlockSpec((tm, d), lambda i: (i, 0)), pl.BlockSpec((1, d), lambda i: (0, 0))],
        out_specs=pl.BlockSpec((tm, d), lambda i: (i, 0)),
        out_shape=jax.ShapeDtypeStruct((s, d), F32),
        compiler_params=_cparams("arbitrary"),
        name="final_norm",
    )(x, g)


def _softmax_init(s, v, m_ref, l_ref, acc_ref):
    m = jnp.max(s, axis=1, keepdims=True)
    p = jnp.exp(s - m)
    m_ref[...] = m
    l_ref[...] = jnp.sum(p, axis=1, keepdims=True)
    acc_ref[...] = _dot(p.astype(BF16), v)


def _softmax_update(s, v, m_ref, l_ref, acc_ref):
    m_prev = m_ref[...]
    m_new = jnp.maximum(m_prev, jnp.max(s, axis=1, keepdims=True))
    alpha = jnp.exp(m_prev - m_new)
    p = jnp.exp(s - m_new)
    l_ref[...] = alpha * l_ref[...] + jnp.sum(p, axis=1, keepdims=True)
    acc_ref[...] = alpha * acc_ref[...] + _dot(p.astype(BF16), v)
    m_ref[...] = m_new


def _rows(ref, start, size):
    return ref[pl.ds(pl.multiple_of(start, size), size), :]


def _kmean_kernel(k_ref, o_ref):
    k = k_ref[...].astype(F32)
    o_ref[...] = jnp.sum(k, axis=0, keepdims=True) * (1.0 / k.shape[0])


def _moba_kmean(proj, col_block, width):
    s = proj.shape[0]
    nb = s // MOBA_BLOCK
    out = pl.pallas_call(
        _kmean_kernel,
        grid=(nb,),
        in_specs=[pl.BlockSpec((MOBA_BLOCK, width), lambda b: (b, col_block))],
        out_specs=pl.BlockSpec((None, 1, width), lambda b: (b, 0, 0)),
        out_shape=jax.ShapeDtypeStruct((nb, 1, width), F32),
        compiler_params=_cparams("arbitrary"),
        name="moba_kmean",
    )(proj)
    return out.reshape(nb, width)


def _moba_kernel(q_ref, k_ref, v_ref, km_ref, o_ref, m_ref, l_ref, acc_ref):
    qi = pl.program_id(1)
    q = q_ref[...]
    tq = q.shape[0]

    gate = _dot_t(q, km_ref[...])
    blk = lax.broadcasted_iota(jnp.int32, gate.shape, 1)
    blkf = blk.astype(F32)
    g = jnp.where(blk < qi, gate, NEG_INF)
    sel = jnp.zeros(gate.shape, F32)
    for _ in range(MOBA_TOPK):
        mx = jnp.max(g, axis=1, keepdims=True)
        idx = jnp.min(jnp.where(g == mx, blkf, float(LANES)), axis=1, keepdims=True)
        pick = blkf == idx
        sel = jnp.where(pick & (idx < qi.astype(F32)), 1.0, sel)
        g = jnp.where(pick, -jnp.inf, g)

    row = lax.broadcasted_iota(jnp.int32, (tq, tq), 0)
    col = lax.broadcasted_iota(jnp.int32, (tq, tq), 1)
    s = jnp.where(col <= row, _dot_t(q, _rows(k_ref, qi * tq, tq)), NEG_INF)
    _softmax_init(s, _rows(v_ref, qi * tq, tq), m_ref, l_ref, acc_ref)

    def body(j, carry):
        selj = jnp.sum(jnp.where(blk == j, sel, 0.0), axis=1, keepdims=True)
        s = jnp.where(selj > 0.5, _dot_t(q, _rows(k_ref, j * tq, tq)), NEG_INF)
        _softmax_update(s, _rows(v_ref, j * tq, tq), m_ref, l_ref, acc_ref)
        return carry

    lax.fori_loop(0, qi, body, 0)
    o_ref[...] = (acc_ref[...] / l_ref[...]).astype(o_ref.dtype)


def _moba_attention(proj, kmean, n_heads, q_cb, k_cb, v_cb):
    s = proj.shape[0]
    tq = MOBA_BLOCK
    return pl.pallas_call(
        _moba_kernel,
        grid=(n_heads, s // tq),
        in_specs=[
            pl.BlockSpec((tq, HEAD_DIM), lambda h, i: (i, q_cb + h)),
            pl.BlockSpec((s, HEAD_DIM), lambda h, i: (0, k_cb + h)),
            pl.BlockSpec((s, HEAD_DIM), lambda h, i: (0, v_cb + h)),
            pl.BlockSpec((LANES, HEAD_DIM), lambda h, i: (0, h)),
        ],
        out_specs=pl.BlockSpec((tq, HEAD_DIM), lambda h, i: (i, h)),
        out_shape=jax.ShapeDtypeStruct((s, n_heads * HEAD_DIM), BF16),
        scratch_shapes=[pltpu.VMEM((tq, 1), F32), pltpu.VMEM((tq, 1), F32), pltpu.VMEM((tq, HEAD_DIM), F32)],
        compiler_params=_cparams("arbitrary", "arbitrary"),
        name="moba_attn",
    )(proj, proj, proj, kmean)


def _swa_kernel(sink_ref, q_ref, kp_ref, ko_ref, vp_ref, vo_ref, o_ref):
    kvh = pl.program_id(0)
    qi = pl.program_id(1)
    tq = q_ref.shape[0]
    row = lax.broadcasted_iota(jnp.int32, (tq, tq), 0)
    col = lax.broadcasted_iota(jnp.int32, (tq, tq), 1)
    ok_own = col <= row
    ok_prev = col - row > jnp.where(qi > 0, 0, tq)
    kp, ko, vp, vo = kp_ref[...], ko_ref[...], vp_ref[...], vo_ref[...]
    for g in range(GROUP):
        lo, hi = g * HEAD_DIM, (g + 1) * HEAD_DIM
        q = q_ref[:, lo:hi]
        sp = jnp.where(ok_prev, _dot_t(q, kp), NEG_INF)
        so = jnp.where(ok_own, _dot_t(q, ko), NEG_INF)
        sink = sink_ref[kvh * GROUP + g]
        m = jnp.maximum(jnp.maximum(jnp.max(sp, axis=1, keepdims=True), jnp.max(so, axis=1, keepdims=True)), sink)
        pp = jnp.where(ok_prev, jnp.exp(sp - m), 0.0)
        po = jnp.where(ok_own, jnp.exp(so - m), 0.0)
        den = jnp.sum(pp, axis=1, keepdims=True) + jnp.sum(po, axis=1, keepdims=True) + jnp.exp(sink - m)
        o = (_dot(pp.astype(BF16), vp) + _dot(po.astype(BF16), vo)) / den
        o_ref[:, lo:hi] = o.astype(o_ref.dtype)


def _swa_attention(proj, sinks, q_cb, k_cb, v_cb):
    s = proj.shape[0]
    tq = BAND_BLOCK
    gw = GROUP * HEAD_DIM
    qb = q_cb // GROUP
    prev = lambda i: jnp.maximum(i - 1, 0)
    return pl.pallas_call(
        _swa_kernel,
        grid=(SWA_KV_HEADS, s // tq),
        in_specs=[
            pl.BlockSpec(memory_space=pltpu.SMEM),
            pl.BlockSpec((tq, gw), lambda h, i: (i, qb + h)),
            pl.BlockSpec((tq, HEAD_DIM), lambda h, i: (prev(i), k_cb + h)),
            pl.BlockSpec((tq, HEAD_DIM), lambda h, i: (i, k_cb + h)),
            pl.BlockSpec((tq, HEAD_DIM), lambda h, i: (prev(i), v_cb + h)),
            pl.BlockSpec((tq, HEAD_DIM), lambda h, i: (i, v_cb + h)),
        ],
        out_specs=pl.BlockSpec((tq, gw), lambda h, i: (i, h)),
        out_shape=jax.ShapeDtypeStruct((s, SWA_KV_HEADS * gw), BF16),
        compiler_params=_cparams("arbitrary", "arbitrary"),
        name="swa_attn",
    )(sinks, proj, proj, proj, proj, proj)


def _fox_cum_kernel(aux_ref, b_ref, o_ref, carry_ref):
    i = pl.program_id(0)

    @pl.when(i == 0)
    def _():
        carry_ref[...] = jnp.zeros_like(carry_ref)

    x = aux_ref[...] + b_ref[...]
    log_f = -(jnp.maximum(-x, 0.0) + jnp.log(1.0 + jnp.exp(-jnp.abs(x))))
    tb = x.shape[0]
    r = lax.broadcasted_iota(jnp.int32, (tb, tb), 0)
    c = lax.broadcasted_iota(jnp.int32, (tb, tb), 1)
    tri = jnp.where(c <= r, 1.0, 0.0).astype(BF16)
    x1 = log_f.astype(BF16)
    r1 = log_f - x1.astype(F32)
    x2 = r1.astype(BF16)
    x3 = (r1 - x2.astype(F32)).astype(BF16)
    out = _dot(tri, x1) + _dot(tri, x2) + _dot(tri, x3) + carry_ref[...]
    o_ref[...] = out
    carry_ref[...] = out[tb - 1:tb, :]


def _fox_cum(aux, bias_row):
    s, w = aux.shape
    tb = min(512, s)
    return pl.pallas_call(
        _fox_cum_kernel,
        grid=(s // tb,),
        in_specs=[pl.BlockSpec((tb, w), lambda i: (i, 0)), pl.BlockSpec((1, w), lambda i: (0, 0))],
        out_specs=pl.BlockSpec((tb, w), lambda i: (i, 0)),
        out_shape=jax.ShapeDtypeStruct((s, w), F32),
        scratch_shapes=[pltpu.VMEM((1, w), F32)],
        compiler_params=_cparams("arbitrary"),
        name="fox_cum",
    )(aux, bias_row)


def _fox_kernel(q_ref, k_ref, v_ref, cc_ref, cr_ref, o_ref, m_ref, l_ref, acc_ref):
    h = pl.program_id(0)
    qi = pl.program_id(1)
    q = q_ref[...]
    tq = q.shape[0]
    cc = cc_ref[...]
    lane = lax.broadcasted_iota(jnp.int32, cc.shape, 1)
    cq = jnp.sum(jnp.where(lane == h, cc, 0.0), axis=1, keepdims=True)

    def scores(j):
        ck = cr_ref[:, pl.ds(pl.multiple_of(j * tq, tq), tq)]
        return _dot_t(q, _rows(k_ref, j * tq, tq)) + cq - ck

    row = lax.broadcasted_iota(jnp.int32, (tq, tq), 0)
    col = lax.broadcasted_iota(jnp.int32, (tq, tq), 1)
    _softmax_init(jnp.where(col <= row, scores(qi), NEG_INF), _rows(v_ref, qi * tq, tq), m_ref, l_ref, acc_ref)

    def body(j, carry):
        _softmax_update(scores(j), _rows(v_ref, j * tq, tq), m_ref, l_ref, acc_ref)
        return carry

    lax.fori_loop(0, qi, body, 0)
    o_ref[...] = (acc_ref[...] / l_ref[...]).astype(o_ref.dtype)


def _fox_attention(proj, cum_col, cum_row, n_heads, q_cb, k_cb, v_cb):
    s = proj.shape[0]
    tq = 256
    return pl.pallas_call(
        _fox_kernel,
        grid=(n_heads, s // tq),
        in_specs=[
            pl.BlockSpec((tq, HEAD_DIM), lambda h, i: (i, q_cb + h)),
            pl.BlockSpec((s, HEAD_DIM), lambda h, i: (0, k_cb + h)),
            pl.BlockSpec((s, HEAD_DIM), lambda h, i: (0, v_cb + h)),
            pl.BlockSpec((tq, LANES), lambda h, i: (i, 0)),
            pl.BlockSpec((None, 1, s), lambda h, i: (h, 0, 0)),
        ],
        out_specs=pl.BlockSpec((tq, HEAD_DIM), lambda h, i: (i, h)),
        out_shape=jax.ShapeDtypeStruct((s, n_heads * HEAD_DIM), BF16),
        scratch_shapes=[pltpu.VMEM((tq, 1), F32), pltpu.VMEM((tq, 1), F32), pltpu.VMEM((tq, HEAD_DIM), F32)],
        compiler_params=_cparams("arbitrary", "arbitrary"),
        name="fox_attn",
    )(proj, proj, proj, cum_col, cum_row)


def _gelu_tanh(x):
    return 0.5 * x * (1.0 + jnp.tanh(math.sqrt(2.0 / math.pi) * (x + 0.044715 * (x * x * x))))


def _nsa_compress_kernel(a_ref, w1_ref, pos_ref, w2_ref, o_ref):
    a = a_ref[...]
    half = a.shape[1]
    n = a.shape[0]
    lo = _dot(a, w1_ref[0:half, :])
    hi = _dot(a, w1_ref[half:, :])
    pos8 = jnp.broadcast_to(pos_ref[...], (8, pos_ref.shape[1]))
    pterm = _dot(pos8, w1_ref[...])[0:1]
    pre = lo + pltpu.roll(hi, n - 1, 0) + pterm
    o_ref[...] = _dot(_gelu_tanh(pre).astype(BF16), w2_ref[...]).astype(o_ref.dtype)


def _nsa_compress(a, w1, pos, w2):
    n, rows, width = a.shape
    return pl.pallas_call(
        _nsa_compress_kernel,
        grid=(n,),
        in_specs=[
            pl.BlockSpec((None, rows, width), lambda i: (i, 0, 0)),
            pl.BlockSpec((None, 2 * width, HEAD_DIM), lambda i: (i // NSA_KV_HEADS, 0, 0)),
            pl.BlockSpec((None, 1, 2 * width), lambda i: (i // NSA_KV_HEADS, 0, 0)),
            pl.BlockSpec((None, HEAD_DIM, HEAD_DIM), lambda i: (i // NSA_KV_HEADS, 0, 0)),
        ],
        out_specs=pl.BlockSpec((None, rows, HEAD_DIM), lambda i: (i, 0, 0)),
        out_shape=jax.ShapeDtypeStruct((n, rows, HEAD_DIM), BF16),
        compiler_params=_cparams("arbitrary"),
        name="nsa_compress",
    )(a, w1, pos, w2)


def _nsa_cmp_kernel(q_ref, kc_ref, vc_ref, ov_ref, ocmp_ref, sel_ref):
    qi = pl.program_id(1)
    tq = q_ref.shape[0]
    ncp = kc_ref.shape[0]
    t = qi * tq + lax.broadcasted_iota(jnp.int32, (tq, ncp), 0)
    n = lax.broadcasted_iota(jnp.int32, (tq, ncp), 1)
    vis = (n * NSA_CMP_STRIDE + (NSA_CMP_LEN - 1)) <= t
    kc, vc = kc_ref[...], vc_ref[...]
    psum = jnp.zeros((tq, ncp), F32)
    for g in range(GROUP):
        lo, hi = g * HEAD_DIM, (g + 1) * HEAD_DIM
        s = jnp.where(vis, _dot_t(q_ref[:, lo:hi], kc), NEG_INF)
        m = jnp.max(s, axis=1, keepdims=True)
        p = jnp.where(vis, jnp.exp(s - m), 0.0)
        p = p / jnp.maximum(jnp.sum(p, axis=1, keepdims=True), 1e-30)
        ocmp_ref[:, lo:hi] = _dot(p.astype(BF16), vc)
        psum = psum + p

    p_hi = psum.astype(BF16)
    p_lo = (psum - p_hi.astype(F32)).astype(BF16)
    ov = ov_ref[...]
    imp = _dot(p_hi, ov) + _dot(p_lo, ov)

    blk = lax.broadcasted_iota(jnp.int32, imp.shape, 1)
    blkf = blk.astype(F32)
    cur = (qi * tq + lax.broadcasted_iota(jnp.int32, imp.shape, 0)) >> SLC_SHIFT
    forced = (blk == 0) | (blk == cur) | (blk == cur - 1)
    imp = jnp.where(forced, jnp.inf, imp)
    imp = jnp.where(blk > cur, -jnp.inf, imp)
    sel = jnp.zeros(imp.shape, F32)
    for _ in range(NSA_SLC_TOPK):
        mx = jnp.max(imp, axis=1, keepdims=True)
        idx = jnp.min(jnp.where(imp == mx, blkf, float(LANES)), axis=1, keepdims=True)
        pick = blkf == idx
        sel = jnp.where(pick, 1.0, sel)
        imp = jnp.where(pick, -jnp.inf, imp)
    sel_ref[...] = sel.astype(sel_ref.dtype)


def _nsa_cmp_select(proj, kv_c, overlap, q_cb):
    s = proj.shape[0]
    tq = BAND_BLOCK
    gw = GROUP * HEAD_DIM
    qb = q_cb // GROUP
    ncp = kv_c.shape[1]
    return pl.pallas_call(
        _nsa_cmp_kernel,
        grid=(NSA_KV_HEADS, s // tq),
        in_specs=[
            pl.BlockSpec((tq, gw), lambda h, i: (i, qb + h)),
            pl.BlockSpec((None, ncp, HEAD_DIM), lambda h, i: (h, 0, 0)),
            pl.BlockSpec((None, ncp, HEAD_DIM), lambda h, i: (NSA_KV_HEADS + h, 0, 0)),
            pl.BlockSpec((ncp, LANES), lambda h, i: (0, 0)),
        ],
        out_specs=[
            pl.BlockSpec((tq, gw), lambda h, i: (i, h)),
            pl.BlockSpec((None, tq, LANES), lambda h, i: (h, i, 0)),
        ],
        out_shape=[
            jax.ShapeDtypeStruct((s, NSA_KV_HEADS * gw), F32),
            jax.ShapeDtypeStruct((NSA_KV_HEADS, s, LANES), BF16),
        ],
        compiler_params=_cparams("arbitrary", "arbitrary"),
        name="nsa_cmp_select",
    )(proj, kv_c, kv_c, overlap)


def _stack_heads(q_ref, qs_ref):
    tq = q_ref.shape[0]
    for g in range(GROUP):
        qs_ref[g * tq:(g + 1) * tq, :] = q_ref[:, g * HEAD_DIM:(g + 1) * HEAD_DIM]
    return qs_ref[...]


def _tile_group(bias):
    return jnp.concatenate([bias] * GROUP, axis=0)


def _nsa_slc_kernel(q_ref, k_ref, v_ref, sel_ref, o_ref, qs_ref, m_ref, l_ref, acc_ref):
    qi = pl.program_id(1)
    tq = q_ref.shape[0]
    tk = 2 * tq
    per = tk // NSA_SLC_BLOCK
    q = _stack_heads(q_ref, qs_ref)
    sel = sel_ref[...]
    t = qi * tq + lax.broadcasted_iota(jnp.int32, (tq, tk), 0)
    kcol = lax.broadcasted_iota(jnp.int32, (tq, tk), 1)
    brow = lax.broadcasted_iota(jnp.int32, (LANES, tk), 0)
    bcol = lax.broadcasted_iota(jnp.int32, (LANES, tk), 1) >> SLC_SHIFT

    def scores(jt):
        expand = jnp.where(brow == jt * per + bcol, 1.0, 0.0).astype(BF16)
        chosen = _dot(sel, expand)
        ok = (chosen > 0.5) & (jt * tk + kcol <= t)
        bias = jnp.where(ok, 0.0, NEG_INF)
        return _dot_t(q, _rows(k_ref, jt * tk, tk)) + _tile_group(bias)

    last = (qi * tq) // tk
    _softmax_init(scores(last), _rows(v_ref, last * tk, tk), m_ref, l_ref, acc_ref)

    def body(jt, carry):
        _softmax_update(scores(jt), _rows(v_ref, jt * tk, tk), m_ref, l_ref, acc_ref)
        return carry

    lax.fori_loop(0, last, body, 0)
    o = acc_ref[...] / l_ref[...]
    for g in range(GROUP):
        o_ref[:, g * HEAD_DIM:(g + 1) * HEAD_DIM] = o[g * tq:(g + 1) * tq, :]


def _nsa_selected(proj, sel, q_cb, k_cb, v_cb):
    s = proj.shape[0]
    tq = BAND_BLOCK
    gw = GROUP * HEAD_DIM
    qb = q_cb // GROUP
    return pl.pallas_call(
        _nsa_slc_kernel,
        grid=(NSA_KV_HEADS, s // tq),
        in_specs=[
            pl.BlockSpec((tq, gw), lambda h, i: (i, qb + h)),
            pl.BlockSpec((s, HEAD_DIM), lambda h, i: (0, k_cb + h)),
            pl.BlockSpec((s, HEAD_DIM), lambda h, i: (0, v_cb + h)),
            pl.BlockSpec((None, tq, LANES), lambda h, i: (h, i, 0)),
        ],
        out_specs=pl.BlockSpec((tq, gw), lambda h, i: (i, h)),
        out_shape=jax.ShapeDtypeStruct((s, NSA_KV_HEADS * gw), F32),
        scratch_shapes=[
            pltpu.VMEM((GROUP * tq, HEAD_DIM), BF16),
            pltpu.VMEM((GROUP * tq, 1), F32),
            pltpu.VMEM((GROUP * tq, 1), F32),
            pltpu.VMEM((GROUP * tq, HEAD_DIM), F32),
        ],
        compiler_params=_cparams("arbitrary", "arbitrary"),
        name="nsa_selected",
    )(proj, proj, proj, sel)


def _nsa_win_kernel(q_ref, k_ref, v_ref, gate_ref, ocmp_ref, oslc_ref, o_ref, qs_ref, m_ref, l_ref, acc_ref):
    qi = pl.program_id(1)
    tq = q_ref.shape[0]
    q = _stack_heads(q_ref, qs_ref)
    row = lax.broadcasted_iota(jnp.int32, (tq, tq), 0)
    col = lax.broadcasted_iota(jnp.int32, (tq, tq), 1)
    n_prev = -(-(NSA_WINDOW - 1) // tq)
    for b in range(n_prev + 1):
        kb = qi - b
        start = jnp.maximum(kb, 0) * tq
        dist = row - col + b * tq
        ok = (dist >= 0) & (dist < jnp.where(kb >= 0, NSA_WINDOW, 0))
        s = _dot_t(q, _rows(k_ref, start, tq)) + _tile_group(jnp.where(ok, 0.0, NEG_INF))
        if b == 0:
            _softmax_init(s, _rows(v_ref, start, tq), m_ref, l_ref, acc_ref)
        else:
            _softmax_update(s, _rows(v_ref, start, tq), m_ref, l_ref, acc_ref)
    o_win = acc_ref[...] / l_ref[...]
    sig = jax.nn.sigmoid(gate_ref[...])
    for g in range(GROUP):
        lo, hi = g * HEAD_DIM, (g + 1) * HEAD_DIM
        o = (sig[:, 3 * g:3 * g + 1] * ocmp_ref[:, lo:hi]
             + sig[:, 3 * g + 1:3 * g + 2] * oslc_ref[:, lo:hi]
             + sig[:, 3 * g + 2:3 * g + 3] * o_win[g * tq:(g + 1) * tq, :])
        o_ref[:, lo:hi] = o.astype(o_ref.dtype)


def _nsa_window_merge(proj, gates, o_cmp, o_slc, q_cb, k_cb, v_cb):
    s = proj.shape[0]
    tq = BAND_BLOCK
    gw = GROUP * HEAD_DIM
    qb = q_cb // GROUP
    return pl.pallas_call(
        _nsa_win_kernel,
        grid=(NSA_KV_HEADS, s // tq),
        in_specs=[
            pl.BlockSpec((tq, gw), lambda h, i: (i, qb + h)),
            pl.BlockSpec((s, HEAD_DIM), lambda h, i: (0, k_cb + h)),
            pl.BlockSpec((s, HEAD_DIM), lambda h, i: (0, v_cb + h)),
            pl.BlockSpec((None, tq, LANES), lambda h, i: (h, i, 0)),
            pl.BlockSpec((tq, gw), lambda h, i: (i, h)),
            pl.BlockSpec((tq, gw), lambda h, i: (i, h)),
        ],
        out_specs=pl.BlockSpec((tq, gw), lambda h, i: (i, h)),
        out_shape=jax.ShapeDtypeStruct((s, NSA_KV_HEADS * gw), BF16),
        scratch_shapes=[
            pltpu.VMEM((GROUP * tq, HEAD_DIM), BF16),
            pltpu.VMEM((GROUP * tq, 1), F32),
            pltpu.VMEM((GROUP * tq, 1), F32),
            pltpu.VMEM((GROUP * tq, HEAD_DIM), F32),
        ],
        compiler_params=_cparams("arbitrary", "arbitrary"),
        name="nsa_window_merge",
    )(proj, proj, proj, gates, o_cmp, o_slc)


def _rope_tables(seq):
    inv = 1.0 / (ROPE_THETA ** (jnp.arange(0, HEAD_DIM, 2, dtype=F32) / HEAD_DIM))
    ang = jnp.arange(seq, dtype=F32)[:, None] * inv[None, :]
    cos, sin = jnp.cos(ang), jnp.sin(ang)
    return jnp.concatenate([cos, cos], axis=-1), jnp.concatenate([-sin, sin], axis=-1)


def _overlap_matrix(seq):
    n_pad = seq // NSA_CMP_STRIDE
    n_cmp = (seq - NSA_CMP_LEN) // NSA_CMP_STRIDE + 1
    c_start = np.arange(n_pad)[:, None] * NSA_CMP_STRIDE
    s_start = np.arange(LANES)[None, :] * NSA_SLC_BLOCK
    ov = (c_start < s_start + NSA_SLC_BLOCK) & (c_start + NSA_CMP_LEN > s_start)
    ov &= (np.arange(n_pad)[:, None] < n_cmp) & (np.arange(LANES)[None, :] < seq // NSA_SLC_BLOCK)
    return jnp.asarray(ov.astype(np.float32), dtype=BF16)


def _ctypes(spec):
    return jnp.asarray(np.concatenate([np.full(n, kind, np.int32) for n, kind in spec]))


def _even_layer(x, mods, norm_g, w_in, sinks, w_out, cosf, sinf):
    sh1, sc1, g1 = mods
    heads = w_in.shape[1] // HEAD_DIM
    n_moba = 8
    ctypes = _ctypes([(n_moba, CT_ROPE_SCALE), (n_moba, CT_ROPE), (n_moba, CT_PLAIN),
                      (8, CT_ROPE_SCALE), (SWA_KV_HEADS, CT_ROPE), (SWA_KV_HEADS, CT_PLAIN)])
    assert ctypes.shape[0] == heads
    proj = _norm_proj(x, norm_g, sc1, sh1, w_in.astype(BF16), ctypes, cosf, sinf, BF16, 512)
    nb = x.shape[0] // MOBA_BLOCK
    kmean = _moba_kmean(proj, 1, n_moba * HEAD_DIM)
    kmean = jnp.pad(kmean, ((0, LANES - nb), (0, 0))).astype(BF16)
    oa = _moba_attention(proj, kmean, n_moba, 0, n_moba, 2 * n_moba)
    ob = _swa_attention(proj, sinks, 3 * n_moba, 4 * n_moba, 4 * n_moba + SWA_KV_HEADS)
    return _out_proj(oa, ob, w_out.astype(BF16), x, g1)


def _odd_layer(x, mods, norm_g, w_in, forget_b, k_pos, k_w1, k_w2, v_pos, v_w1, v_w2, w_out, cosf, sinf, overlap):
    sh1, sc1, g1 = mods
    s = x.shape[0]
    n_fox = 8
    hw = n_fox * HEAD_DIM
    kvw = NSA_KV_HEADS * HEAD_DIM
    o_fc = 3 * hw
    o_qd = o_fc + n_fox
    o_gd = o_qd + hw + 6 * kvw
    w_main = jnp.concatenate([w_in[:, :o_fc], w_in[:, o_qd:o_gd]], axis=1).astype(BF16)
    n_aux = n_fox + 3 * 8
    w_aux = jnp.concatenate([w_in[:, o_fc:o_qd], w_in[:, o_gd:], jnp.zeros((w_in.shape[0], LANES - n_aux), F32)],
                            axis=1).astype(BF16)
    ctypes = _ctypes([(8, CT_SCALE), (8, CT_PLAIN), (8, CT_PLAIN), (8, CT_ROPE_SCALE),
                      (2, CT_ROPE), (2, CT_PLAIN), (2, CT_ROPE), (2, CT_PLAIN), (2, CT_ROPE), (2, CT_PLAIN)])
    proj = _norm_proj(x, norm_g, sc1, sh1, w_main, ctypes, cosf, sinf, BF16, 512)
    aux = _norm_proj(x, norm_g, sc1, sh1, w_aux, _ctypes([(1, CT_PLAIN)]), cosf, sinf, F32, LANES)

    bias_row = jnp.pad(forget_b.astype(F32), (0, LANES - n_fox)).reshape(1, LANES)
    cum = _fox_cum(aux, bias_row)
    cum_row = cum[:, :n_fox].T.reshape(n_fox, 1, s)
    oc = _fox_attention(proj, cum, cum_row, n_fox, 0, 8, 16)

    c0 = 32
    cmp_in = proj[:, c0 * HEAD_DIM:(c0 + 4) * HEAD_DIM].reshape(s, 4, HEAD_DIM).transpose(1, 0, 2)
    cmp_in = cmp_in.reshape(4, s // NSA_CMP_STRIDE, NSA_CMP_STRIDE * HEAD_DIM)
    kv_c = _nsa_compress(
        cmp_in,
        jnp.stack([k_w1, v_w1]).astype(BF16),
        jnp.stack([k_pos.reshape(1, -1), v_pos.reshape(1, -1)]).astype(BF16),
        jnp.stack([k_w2, v_w2]).astype(BF16))
    o_cmp, sel = _nsa_cmp_select(proj, kv_c, overlap, 24)
    o_slc = _nsa_selected(proj, sel, 24, c0 + 4, c0 + 6)
    gates = aux[:, n_fox:n_aux].reshape(s, NSA_KV_HEADS, 3 * GROUP).transpose(1, 0, 2)
    gates = jnp.pad(gates, ((0, 0), (0, 0), (0, LANES - 3 * GROUP)))
    od = _nsa_window_merge(proj, gates, o_cmp, o_slc, 24, c0 + 8, c0 + 10)
    return _out_proj(oc, od, w_out.astype(BF16), x, g1)


def kernel(x, c, norm_mix_g, norm_mlp_g, ada_w, ada_b, mlp_up, mlp_down, even_w_in, even_sinks, even_w_out, odd_w_in, fox_forget_b, nsa_k_pos, nsa_k_w1, nsa_k_w2, nsa_v_pos, nsa_v_w1, nsa_v_w2, odd_w_out, final_norm_g):
    batch, seq, d = x.shape
    assert batch == 1
    depth = ada_w.shape[0]
    cosf, sinf = _rope_tables(seq)
    overlap = _overlap_matrix(seq)
    mod = _ada_mod(c, ada_w, ada_b).reshape(depth, 6, 1, d)
    xs = x[0]
    for i in range(depth):
        sh1, sc1, g1, sh2, sc2, g2 = [mod[i, t] for t in range(6)]
        ng = norm_mix_g[i].reshape(1, d)
        j = i // 2
        if i % 2 == 0:
            xs = _even_layer(xs, (sh1, sc1, g1), ng, even_w_in[j], even_sinks[j], even_w_out[j], cosf, sinf)
        else:
            xs = _odd_layer(xs, (sh1, sc1, g1), ng, odd_w_in[j], fox_forget_b[j], nsa_k_pos[j], nsa_k_w1[j],
                            nsa_k_w2[j], nsa_v_pos[j], nsa_v_w1[j], nsa_v_w2[j], odd_w_out[j], cosf, sinf, overlap)
        xs = _mlp(xs, norm_mlp_g[i].reshape(1, d), sc2, sh2, mlp_up[i].astype(BF16), mlp_down[i].astype(BF16), g2)
    return _final_norm(xs, final_norm_g.reshape(1, d))[None]
```

```python
import functools
import math

import numpy as np
import jax
import jax.numpy as jnp
from jax import lax
from jax.experimental import pallas as pl
from jax.experimental.pallas import tpu as pltpu

HEAD_DIM = 128
ROPE_THETA = 10000.0
NORM_EPS = 1e-6
MOBA_BLOCK = 256
MOBA_TOPK = 3
SWA_KV_HEADS = 2
SWA_WINDOW = 128
NSA_KV_HEADS = 2
NSA_CMP_LEN = 32
NSA_CMP_STRIDE = 16
NSA_SLC_BLOCK = 64
SLC_SHIFT = 6
NSA_SLC_TOPK = 16
NSA_WINDOW = 512
BAND_BLOCK = 128
NEG_INF = -1e30
SCALE = HEAD_DIM ** -0.5
LANES = 128
GROUP = 4
VMEM_LIMIT = 48 * 1024 * 1024

F32 = jnp.float32
BF16 = jnp.bfloat16

CT_PLAIN, CT_ROPE, CT_ROPE_SCALE, CT_SCALE = 0, 1, 2, 3


def _cparams(*sem):
    return pltpu.CompilerParams(dimension_semantics=sem, vmem_limit_bytes=VMEM_LIMIT)


def _dot(a, b):
    return jnp.dot(a, b, preferred_element_type=F32)


def _dot_t(a, b):
    return lax.dot_general(a, b, (((1,), (1,)), ((), ())), preferred_element_type=F32)


def _ada_kernel(c_ref, w_ref, b_ref, o_ref):
    c = c_ref[...]
    cond = c * jax.nn.sigmoid(c)
    cond8 = jnp.broadcast_to(cond, (8, cond.shape[1])).astype(BF16)
    y = _dot(cond8, w_ref[...].astype(BF16))
    o_ref[...] = y[0:1] + b_ref[...]


def _ada_mod(c, ada_w, ada_b):
    depth, d, n = ada_w.shape
    tn = 1024
    return pl.pallas_call(
        _ada_kernel,
        grid=(depth, n // tn),
        in_specs=[
            pl.BlockSpec((1, d), lambda l, j: (0, 0)),
            pl.BlockSpec((None, d, tn), lambda l, j: (l, 0, j)),
            pl.BlockSpec((None, 1, tn), lambda l, j: (l, 0, j)),
        ],
        out_specs=pl.BlockSpec((None, 1, tn), lambda l, j: (l, 0, j)),
        out_shape=jax.ShapeDtypeStruct((depth, 1, n), F32),
        compiler_params=_cparams("arbitrary", "arbitrary"),
        name="ada_mod",
    )(c, ada_w, ada_b.reshape(depth, 1, n))


def _norm_mod(x, g, sc, sh):
    y = x * lax.rsqrt(jnp.mean(x * x, axis=-1, keepdims=True) + NORM_EPS)
    return (y * g) * (1.0 + sc) + sh


def _proj_kernel(ct_ref, x_ref, g_ref, sc_ref, sh_ref, w_ref, cos_ref, sin_ref, o_ref, h_ref):
    j = pl.program_id(1)

    @pl.when(j == 0)
    def _():
        h_ref[...] = _norm_mod(x_ref[...], g_ref[...], sc_ref[...], sh_ref[...]).astype(BF16)

    y = _dot(h_ref[...], w_ref[...])
    groups = y.shape[1] // LANES
    for gi in range(groups):
        lo, hi = gi * LANES, (gi + 1) * LANES
        yg = y[:, lo:hi]
        ct = ct_ref[j * groups + gi]

        @pl.when(ct == CT_PLAIN)
        def _():
            o_ref[:, lo:hi] = yg.astype(o_ref.dtype)

        @pl.when(ct == CT_SCALE)
        def _():
            o_ref[:, lo:hi] = (yg * SCALE).astype(o_ref.dtype)

        @pl.when((ct == CT_ROPE) | (ct == CT_ROPE_SCALE))
        def _():
            mult = jnp.where(ct == CT_ROPE_SCALE, SCALE, 1.0).astype(F32)
            r = yg * cos_ref[...] + pltpu.roll(yg, HEAD_DIM // 2, 1) * sin_ref[...]
            o_ref[:, lo:hi] = (r * mult).astype(o_ref.dtype)


def _norm_proj(x, g, sc, sh, w, ctypes, cosf, sinf, out_dtype, tn):
    s, d = x.shape
    n = w.shape[1]
    tm = min(1024, s)
    grid_spec = pltpu.PrefetchScalarGridSpec(
        num_scalar_prefetch=1,
        grid=(s // tm, n // tn),
        in_specs=[
            pl.BlockSpec((tm, d), lambda i, j, ct: (i, 0)),
            pl.BlockSpec((1, d), lambda i, j, ct: (0, 0)),
            pl.BlockSpec((1, d), lambda i, j, ct: (0, 0)),
            pl.BlockSpec((1, d), lambda i, j, ct: (0, 0)),
            pl.BlockSpec((d, tn), lambda i, j, ct: (0, j)),
            pl.BlockSpec((tm, LANES), lambda i, j, ct: (i, 0)),
            pl.BlockSpec((tm, LANES), lambda i, j, ct: (i, 0)),
        ],
        out_specs=pl.BlockSpec((tm, tn), lambda i, j, ct: (i, j)),
        scratch_shapes=[pltpu.VMEM((tm, d), BF16)],
    )
    return pl.pallas_call(
        _proj_kernel,
        grid_spec=grid_spec,
        out_shape=jax.ShapeDtypeStruct((s, n), out_dtype),
        compiler_params=_cparams("arbitrary", "arbitrary"),
        name="norm_proj",
    )(ctypes, x, g, sc, sh, w, cosf, sinf)


def _outproj_kernel(a_ref, b_ref, w_ref, x_ref, g_ref, o_ref):
    ha = a_ref.shape[1]
    y = _dot(a_ref[...], w_ref[0:ha, :]) + _dot(b_ref[...], w_ref[ha:, :])
    o_ref[...] = x_ref[...] + g_ref[...] * y


def _out_proj(oa, ob, w, x, gate):
    s, d = x.shape
    ha, hb = oa.shape[1], ob.shape[1]
    tm, tn = min(1024, s), 512
    return pl.pallas_call(
        _outproj_kernel,
        grid=(s // tm, d // tn),
        in_specs=[
            pl.BlockSpec((tm, ha), lambda i, j: (i, 0)),
            pl.BlockSpec((tm, hb), lambda i, j: (i, 0)),
            pl.BlockSpec((ha + hb, tn), lambda i, j: (0, j)),
            pl.BlockSpec((tm, tn), lambda i, j: (i, j)),
            pl.BlockSpec((1, tn), lambda i, j: (0, j)),
        ],
        out_specs=pl.BlockSpec((tm, tn), lambda i, j: (i, j)),
        out_shape=jax.ShapeDtypeStruct((s, d), F32),
        compiler_params=_cparams("arbitrary", "arbitrary"),
        name="out_proj",
    )(oa, ob, w, x, gate)


def _mlp_kernel(x_ref, g_ref, sc_ref, sh_ref, up_ref, down_ref, gate_ref, o_ref, h_ref, acc_ref):
    f = pl.program_id(1)

    @pl.when(f == 0)
    def _():
        h_ref[...] = _norm_mod(x_ref[...], g_ref[...], sc_ref[...], sh_ref[...]).astype(BF16)
        acc_ref[...] = jnp.zeros_like(acc_ref)

    hid = jnp.maximum(_dot(h_ref[...], up_ref[...]), 0.0)
    acc_ref[...] += _dot((hid * hid).astype(BF16), down_ref[...])

    @pl.when(f == pl.num_programs(1) - 1)
    def _():
        o_ref[...] = x_ref[...] + gate_ref[...] * acc_ref[...]


def _mlp(x, g, sc, sh, up, down, gate):
    s, d = x.shape
    ff = up.shape[1]
    tm, tf = min(512, s), 512
    return pl.pallas_call(
        _mlp_kernel,
        grid=(s // tm, ff // tf),
        in_specs=[
            pl.BlockSpec((tm, d), lambda i, f: (i, 0)),
            pl.BlockSpec((1, d), lambda i, f: (0, 0)),
            pl.BlockSpec((1, d), lambda i, f: (0, 0)),
            pl.BlockSpec((1, d), lambda i, f: (0, 0)),
            pl.BlockSpec((d, tf), lambda i, f: (0, f)),
            pl.BlockSpec((tf, d), lambda i, f: (f, 0)),
            pl.BlockSpec((1, d), lambda i, f: (0, 0)),
        ],
        out_specs=pl.BlockSpec((tm, d), lambda i, f: (i, 0)),
        out_shape=jax.ShapeDtypeStruct((s, d), F32),
        scratch_shapes=[pltpu.VMEM((tm, d), BF16), pltpu.VMEM((tm, d), F32)],
        compiler_params=_cparams("arbitrary", "arbitrary"),
        name="mlp",
    )(x, g, sc, sh, up, down, gate)


def _final_norm_kernel(x_ref, g_ref, o_ref):
    x = x_ref[...]
    o_ref[...] = x * lax.rsqrt(jnp.mean(x * x, axis=-1, keepdims=True) + NORM_EPS) * g_ref[...]


def _final_norm(x, g):
    s, d = x.shape
    tm = min(512, s)
    return pl.pallas_call(
        _final_norm_kernel,
        grid=(s // tm,),
        in_specs=[pl.BlockSpec((tm, d), lambda i: (i, 0)), pl.BlockSpec((1, d), lambda i: (0, 0))],
        out_specs=pl.BlockSpec((tm, d), lambda i: (i, 0)),
        out_shape=jax.ShapeDtypeStruct((s, d), F32),
        compiler_params=_cparams("arbitrary"),
        name="final_norm",
    )(x, g)


def _softmax_init(s, v, m_ref, l_ref, acc_ref):
    m = jnp.max(s, axis=1, keepdims=True)
    p = jnp.exp(s - m)
    m_ref[...] = m
    l_ref[...] = jnp.sum(p, axis=1, keepdims=True)
    acc_ref[...] = _dot(p.astype(BF16), v)


def _softmax_update(s, v, m_ref, l_ref, acc_ref):
    m_prev = m_ref[...]
    m_new = jnp.maximum(m_prev, jnp.max(s, axis=1, keepdims=True))
    alpha = jnp.exp(m_prev - m_new)
    p = jnp.exp(s - m_new)
    l_ref[...] = alpha * l_ref[...] + jnp.sum(p, axis=1, keepdims=True)
    acc_ref[...] = alpha * acc_ref[...] + _dot(p.astype(BF16), v)
    m_ref[...] = m_new


def _rows(ref, start, size):
    return ref[pl.ds(pl.multiple_of(start, size), size), :]


def _log2(n):
    assert n > 0 and n & (n - 1) == 0
    return n.bit_length() - 1


FLASH_TILE = 512
HEADS_PER_STEP = 4


def _resident(shape, index_map):
    return pl.BlockSpec(shape, index_map, pipeline_mode=pl.Buffered(1))


def _flash_tile(qa, k_ref, kx_ref, v_ref, start, tk, m_ref, acc_ref, mask=None, first=False, head=0):
    cols = slice(head * HEAD_DIM, (head + 1) * HEAD_DIM)
    rows = pl.ds(pl.multiple_of(start, tk), tk)
    ka = jnp.concatenate([k_ref[rows, cols], _rows(kx_ref, start, tk)], axis=1)
    s = _dot_t(qa, ka)
    if mask is not None:
        s = jnp.where(mask, s, NEG_INF)
    chunks = [s[:, c * LANES:(c + 1) * LANES] for c in range(tk // LANES)]
    m_cur = jnp.max(functools.reduce(jnp.maximum, chunks), axis=1, keepdims=True)
    if first:
        m_new = jnp.broadcast_to(m_cur, (s.shape[0], LANES))
    else:
        m_prev = m_ref[...]
        m_new = jnp.maximum(m_prev, m_cur)
    p = jnp.concatenate([jnp.exp(c - m_new) for c in chunks], axis=1).astype(BF16)
    va = jnp.concatenate([v_ref[rows, cols], jnp.ones((tk, LANES), BF16)], axis=1)
    pv = _dot(p, va)
    if first:
        acc_ref[...] = pv
    else:
        alpha = jnp.exp(m_prev - m_new)
        acc_ref[...] = jnp.concatenate([alpha, alpha], axis=1) * acc_ref[...] + pv
    m_ref[...] = m_new


def _flash_finish(acc_ref):
    acc = acc_ref[...]
    return acc[:, :HEAD_DIM] / acc[:, HEAD_DIM:]


def _block_onehot(seq, block):
    return jnp.asarray((np.arange(seq)[:, None] // block == np.arange(LANES)[None, :]).astype(np.float32), dtype=BF16)


def _kmean_kernel(k_ref, o_ref):
    k = k_ref[...].astype(F32)
    o_ref[...] = jnp.sum(k, axis=0, keepdims=True) * (1.0 / k.shape[0])


def _moba_kmean(proj, col_block, width):
    s = proj.shape[0]
    nb = s // MOBA_BLOCK
    out = pl.pallas_call(
        _kmean_kernel,
        grid=(nb,),
        in_specs=[pl.BlockSpec((MOBA_BLOCK, width), lambda b: (b, col_block))],
        out_specs=pl.BlockSpec((None, 1, width), lambda b: (b, 0, 0)),
        out_shape=jax.ShapeDtypeStruct((nb, 1, width), F32),
        compiler_params=_cparams("arbitrary"),
        name="moba_kmean",
    )(proj)
    return out.reshape(nb, width)


def _moba_kernel(q_ref, k_ref, kx_ref, v_ref, km_ref, o_ref, m_ref, acc_ref, *, tk, hp):
    qi = pl.program_id(1)
    tq = q_ref.shape[0]
    blk = lax.broadcasted_iota(jnp.int32, (tq, LANES), 1)
    blkf = blk.astype(F32)
    own = jnp.right_shift(qi * tq + lax.broadcasted_iota(jnp.int32, (tq, LANES), 0), _log2(MOBA_BLOCK))
    ownf = own.astype(F32)

    qas = []
    for hh in range(hp):
        cols = slice(hh * HEAD_DIM, (hh + 1) * HEAD_DIM)
        q = q_ref[:, cols]
        g = jnp.where(blk < own, _dot_t(q, km_ref[:, cols]), NEG_INF)
        sel = jnp.zeros((tq, LANES), F32)
        for _ in range(MOBA_TOPK):
            mx = jnp.max(g, axis=1, keepdims=True)
            idx = jnp.min(jnp.where(g == mx, blkf, float(LANES)), axis=1, keepdims=True)
            pick = blkf == idx
            sel = jnp.where(pick & (idx < ownf), 1.0, sel)
            g = jnp.where(pick, -jnp.inf, g)
        qx = jnp.where((sel > 0.5) | (blk == own), 0.0, NEG_INF).astype(BF16)
        qas.append(jnp.concatenate([q, qx], axis=1))

    def tiles(start, **kw):
        for hh in range(hp):
            _flash_tile(qas[hh], k_ref, kx_ref, v_ref, start, tk, m_ref.at[hh], acc_ref.at[hh], head=hh, **kw)

    last = jnp.right_shift(qi, _log2(tk // tq))
    t = qi * tq + lax.broadcasted_iota(jnp.int32, (tq, tk), 0)
    kpos = last * tk + lax.broadcasted_iota(jnp.int32, (tq, tk), 1)
    tiles(last * tk, mask=kpos <= t, first=True)

    def body(p, carry):
        tiles(p * tk)
        return carry

    lax.fori_loop(0, last, body, 0)
    for hh in range(hp):
        o_ref[:, hh * HEAD_DIM:(hh + 1) * HEAD_DIM] = _flash_finish(acc_ref.at[hh]).astype(o_ref.dtype)


def _moba_attention(proj, kmean, n_heads, q_cb, k_cb, v_cb):
    s = proj.shape[0]
    tq = tk = min(FLASH_TILE, s)
    hp = HEADS_PER_STEP
    hw = hp * HEAD_DIM
    assert s % tk == 0 and tk % MOBA_BLOCK == 0 and n_heads % hp == 0 and q_cb % hp == k_cb % hp == v_cb % hp == 0
    return pl.pallas_call(
        functools.partial(_moba_kernel, tk=tk, hp=hp),
        grid=(n_heads // hp, s // tq),
        in_specs=[
            pl.BlockSpec((tq, hw), lambda h, i: (i, q_cb // hp + h)),
            _resident((s, hw), lambda h, i: (0, k_cb // hp + h)),
            _resident((s, LANES), lambda h, i: (0, 0)),
            _resident((s, hw), lambda h, i: (0, v_cb // hp + h)),
            pl.BlockSpec((LANES, hw), lambda h, i: (0, h)),
        ],
        out_specs=pl.BlockSpec((tq, hw), lambda h, i: (i, h)),
        out_shape=jax.ShapeDtypeStruct((s, n_heads * HEAD_DIM), BF16),
        scratch_shapes=[pltpu.VMEM((hp, tq, LANES), F32), pltpu.VMEM((hp, tq, 2 * HEAD_DIM), F32)],
        compiler_params=_cparams("arbitrary", "arbitrary"),
        name="moba_attn",
    )(proj, proj, _block_onehot(s, MOBA_BLOCK), proj, kmean)


def _swa_kernel(sink_ref, q_ref, kp_ref, ko_ref, vp_ref, vo_ref, o_ref):
    kvh = pl.program_id(0)
    qi = pl.program_id(1)
    tq = q_ref.shape[0]
    row = lax.broadcasted_iota(jnp.int32, (tq, tq), 0)
    col = lax.broadcasted_iota(jnp.int32, (tq, tq), 1)
    ok_own = col <= row
    ok_prev = col - row > jnp.where(qi > 0, 0, tq)
    kp, ko, vp, vo = kp_ref[...], ko_ref[...], vp_ref[...], vo_ref[...]
    for g in range(GROUP):
        lo, hi = g * HEAD_DIM, (g + 1) * HEAD_DIM
        q = q_ref[:, lo:hi]
        sp = jnp.where(ok_prev, _dot_t(q, kp), NEG_INF)
        so = jnp.where(ok_own, _dot_t(q, ko), NEG_INF)
        sink = sink_ref[kvh * GROUP + g]
        m = jnp.maximum(jnp.maximum(jnp.max(sp, axis=1, keepdims=True), jnp.max(so, axis=1, keepdims=True)), sink)
        pp = jnp.where(ok_prev, jnp.exp(sp - m), 0.0)
        po = jnp.where(ok_own, jnp.exp(so - m), 0.0)
        den = jnp.sum(pp, axis=1, keepdims=True) + jnp.sum(po, axis=1, keepdims=True) + jnp.exp(sink - m)
        o = (_dot(pp.astype(BF16), vp) + _dot(po.astype(BF16), vo)) / den
        o_ref[:, lo:hi] = o.astype(o_ref.dtype)


def _swa_attention(proj, sinks, q_cb, k_cb, v_cb):
    s = proj.shape[0]
    tq = BAND_BLOCK
    gw = GROUP * HEAD_DIM
    qb = q_cb // GROUP
    prev = lambda i: jnp.maximum(i - 1, 0)
    return pl.pallas_call(
        _swa_kernel,
        grid=(SWA_KV_HEADS, s // tq),
        in_specs=[
            pl.BlockSpec(memory_space=pltpu.SMEM),
            pl.BlockSpec((tq, gw), lambda h, i: (i, qb + h)),
            pl.BlockSpec((tq, HEAD_DIM), lambda h, i: (prev(i), k_cb + h)),
            pl.BlockSpec((tq, HEAD_DIM), lambda h, i: (i, k_cb + h)),
            pl.BlockSpec((tq, HEAD_DIM), lambda h, i: (prev(i), v_cb + h)),
            pl.BlockSpec((tq, HEAD_DIM), lambda h, i: (i, v_cb + h)),
        ],
        out_specs=pl.BlockSpec((tq, gw), lambda h, i: (i, h)),
        out_shape=jax.ShapeDtypeStruct((s, SWA_KV_HEADS * gw), BF16),
        compiler_params=_cparams("arbitrary", "arbitrary"),
        name="swa_attn",
    )(sinks, proj, proj, proj, proj, proj)


FOX_LANES = 16


def _split3(x):
    x1 = x.astype(BF16)
    r1 = x - x1.astype(F32)
    x2 = r1.astype(BF16)
    return x1, x2, (r1 - x2.astype(F32)).astype(BF16)


def _fox_tables(n_heads):
    place = np.zeros((6, LANES, LANES), np.float32)
    ones_q = np.zeros((1, LANES), np.float32)
    ones_k = np.zeros((1, LANES), np.float32)
    for h in range(n_heads):
        for i in range(6):
            place[i, h, FOX_LANES * h + i] = 1.0
        ones_q[0, FOX_LANES * h + 3:FOX_LANES * h + 6] = 1.0
        ones_k[0, FOX_LANES * h:FOX_LANES * h + 3] = 1.0
    return jnp.asarray(place, dtype=BF16), jnp.asarray(ones_q), jnp.asarray(ones_k)


def _fox_cum_kernel(aux_ref, b_ref, place_ref, oq_ref, ok_ref, qx_ref, kx_ref, carry_ref):
    i = pl.program_id(0)

    @pl.when(i == 0)
    def _():
        carry_ref[...] = jnp.zeros_like(carry_ref)

    x = aux_ref[...] + b_ref[...]
    log_f = -(jnp.maximum(-x, 0.0) + jnp.log(1.0 + jnp.exp(-jnp.abs(x))))
    tb = x.shape[0]
    r = lax.broadcasted_iota(jnp.int32, (tb, tb), 0)
    c = lax.broadcasted_iota(jnp.int32, (tb, tb), 1)
    tri = jnp.where(c <= r, 1.0, 0.0).astype(BF16)
    x1, x2, x3 = _split3(log_f)
    cum = _dot(tri, x1) + _dot(tri, x2) + _dot(tri, x3) + carry_ref[...]
    carry_ref[...] = cum[tb - 1:tb, :]
    c1, c2, c3 = _split3(cum)
    qx = _dot(c1, place_ref[0]) + _dot(c2, place_ref[1]) + _dot(c3, place_ref[2]) + oq_ref[...]
    kx = ok_ref[...] - (_dot(c1, place_ref[3]) + _dot(c2, place_ref[4]) + _dot(c3, place_ref[5]))
    qx_ref[...] = qx.astype(BF16)
    kx_ref[...] = kx.astype(BF16)


def _fox_cum(aux, bias_row, n_heads):
    s, w = aux.shape
    tb = min(512, s)
    place, ones_q, ones_k = _fox_tables(n_heads)
    full = lambda shape: pl.BlockSpec(shape, lambda i: (0,) * len(shape))
    return pl.pallas_call(
        _fox_cum_kernel,
        grid=(s // tb,),
        in_specs=[pl.BlockSpec((tb, w), lambda i: (i, 0)), full((1, w)), full(place.shape), full((1, w)), full((1, w))],
        out_specs=[pl.BlockSpec((tb, w), lambda i: (i, 0)), pl.BlockSpec((tb, w), lambda i: (i, 0))],
        out_shape=[jax.ShapeDtypeStruct((s, w), BF16), jax.ShapeDtypeStruct((s, w), BF16)],
        scratch_shapes=[pltpu.VMEM((1, w), F32)],
        compiler_params=_cparams("arbitrary"),
        name="fox_cum",
    )(aux, bias_row, place, ones_q, ones_k)


def _fox_kernel(q_ref, qx_ref, k_ref, kx_ref, v_ref, o_ref, m_ref, acc_ref, *, tk, hp):
    hb = pl.program_id(0)
    qi = pl.program_id(1)
    tq = q_ref.shape[0]
    lane = lax.broadcasted_iota(jnp.int32, (tq, LANES), 1)
    qxf = qx_ref[...].astype(F32)
    qas = []
    for hh in range(hp):
        mine = jnp.right_shift(lane, 4) == hb * hp + hh
        qx = jnp.where(mine, qxf, 0.0).astype(BF16)
        qas.append(jnp.concatenate([q_ref[:, hh * HEAD_DIM:(hh + 1) * HEAD_DIM], qx], axis=1))

    def tiles(start, **kw):
        for hh in range(hp):
            _flash_tile(qas[hh], k_ref, kx_ref, v_ref, start, tk, m_ref.at[hh], acc_ref.at[hh], head=hh, **kw)

    last = jnp.right_shift(qi, _log2(tk // tq))
    t = qi * tq + lax.broadcasted_iota(jnp.int32, (tq, tk), 0)
    kpos = last * tk + lax.broadcasted_iota(jnp.int32, (tq, tk), 1)
    tiles(last * tk, mask=kpos <= t, first=True)

    def body(p, carry):
        tiles(p * tk)
        return carry

    lax.fori_loop(0, last, body, 0)
    for hh in range(hp):
        o_ref[:, hh * HEAD_DIM:(hh + 1) * HEAD_DIM] = _flash_finish(acc_ref.at[hh]).astype(o_ref.dtype)


def _fox_attention(proj, qx, kx, n_heads, q_cb, k_cb, v_cb):
    s = proj.shape[0]
    tq = tk = min(FLASH_TILE, s)
    hp = HEADS_PER_STEP
    hw = hp * HEAD_DIM
    assert s % tk == 0 and FOX_LANES == 16 and n_heads % hp == 0 and q_cb % hp == k_cb % hp == v_cb % hp == 0
    return pl.pallas_call(
        functools.partial(_fox_kernel, tk=tk, hp=hp),
        grid=(n_heads // hp, s // tq),
        in_specs=[
            pl.BlockSpec((tq, hw), lambda h, i: (i, q_cb // hp + h)),
            pl.BlockSpec((tq, LANES), lambda h, i: (i, 0)),
            _resident((s, hw), lambda h, i: (0, k_cb // hp + h)),
            _resident((s, LANES), lambda h, i: (0, 0)),
            _resident((s, hw), lambda h, i: (0, v_cb // hp + h)),
        ],
        out_specs=pl.BlockSpec((tq, hw), lambda h, i: (i, h)),
        out_shape=jax.ShapeDtypeStruct((s, n_heads * HEAD_DIM), BF16),
        scratch_shapes=[pltpu.VMEM((hp, tq, LANES), F32), pltpu.VMEM((hp, tq, 2 * HEAD_DIM), F32)],
        compiler_params=_cparams("arbitrary", "arbitrary"),
        name="fox_attn",
    )(proj, qx, proj, kx, proj)


def _gelu_tanh(x):
    return 0.5 * x * (1.0 + jnp.tanh(math.sqrt(2.0 / math.pi) * (x + 0.044715 * (x * x * x))))


def _nsa_compress_kernel(a_ref, w1_ref, pos_ref, w2_ref, o_ref):
    a = a_ref[...]
    half = a.shape[1]
    n = a.shape[0]
    lo = _dot(a, w1_ref[0:half, :])
    hi = _dot(a, w1_ref[half:, :])
    pos8 = jnp.broadcast_to(pos_ref[...], (8, pos_ref.shape[1]))
    pterm = _dot(pos8, w1_ref[...])[0:1]
    pre = lo + pltpu.roll(hi, n - 1, 0) + pterm
    o_ref[...] = _dot(_gelu_tanh(pre).astype(BF16), w2_ref[...]).astype(o_ref.dtype)


def _nsa_compress(a, w1, pos, w2):
    n, rows, width = a.shape
    return pl.pallas_call(
        _nsa_compress_kernel,
        grid=(n,),
        in_specs=[
            pl.BlockSpec((None, rows, width), lambda i: (i, 0, 0)),
            pl.BlockSpec((None, 2 * width, HEAD_DIM), lambda i: (i // NSA_KV_HEADS, 0, 0)),
            pl.BlockSpec((None, 1, 2 * width), lambda i: (i // NSA_KV_HEADS, 0, 0)),
            pl.BlockSpec((None, HEAD_DIM, HEAD_DIM), lambda i: (i // NSA_KV_HEADS, 0, 0)),
        ],
        out_specs=pl.BlockSpec((None, rows, HEAD_DIM), lambda i: (i, 0, 0)),
        out_shape=jax.ShapeDtypeStruct((n, rows, HEAD_DIM), BF16),
        compiler_params=_cparams("arbitrary"),
        name="nsa_compress",
    )(a, w1, pos, w2)


def _nsa_cmp_kernel(q_ref, kc_ref, vc_ref, ov_ref, ocmp_ref, sel_ref):
    qi = pl.program_id(1)
    tq = q_ref.shape[0]
    ncp = kc_ref.shape[0]
    t = qi * tq + lax.broadcasted_iota(jnp.int32, (tq, ncp), 0)
    n = lax.broadcasted_iota(jnp.int32, (tq, ncp), 1)
    vis = (n * NSA_CMP_STRIDE + (NSA_CMP_LEN - 1)) <= t
    kc, vc = kc_ref[...], vc_ref[...]
    psum = jnp.zeros((tq, ncp), F32)
    for g in range(GROUP):
        lo, hi = g * HEAD_DIM, (g + 1) * HEAD_DIM
        s = jnp.where(vis, _dot_t(q_ref[:, lo:hi], kc), NEG_INF)
        m = jnp.max(s, axis=1, keepdims=True)
        p = jnp.where(vis, jnp.exp(s - m), 0.0)
        p = p / jnp.maximum(jnp.sum(p, axis=1, keepdims=True), 1e-30)
        ocmp_ref[:, lo:hi] = _dot(p.astype(BF16), vc)
        psum = psum + p

    p_hi = psum.astype(BF16)
    p_lo = (psum - p_hi.astype(F32)).astype(BF16)
    ov = ov_ref[...]
    imp = _dot(p_hi, ov) + _dot(p_lo, ov)

    blk = lax.broadcasted_iota(jnp.int32, imp.shape, 1)
    blkf = blk.astype(F32)
    cur = (qi * tq + lax.broadcasted_iota(jnp.int32, imp.shape, 0)) >> SLC_SHIFT
    forced = (blk == 0) | (blk == cur) | (blk == cur - 1)
    imp = jnp.where(forced, jnp.inf, imp)
    imp = jnp.where(blk > cur, -jnp.inf, imp)
    sel = jnp.zeros(imp.shape, F32)
    for _ in range(NSA_SLC_TOPK):
        mx = jnp.max(imp, axis=1, keepdims=True)
        idx = jnp.min(jnp.where(imp == mx, blkf, float(LANES)), axis=1, keepdims=True)
        pick = blkf == idx
        sel = jnp.where(pick, 1.0, sel)
        imp = jnp.where(pick, -jnp.inf, imp)
    sel_ref[...] = jnp.where(sel > 0.5, 0.0, NEG_INF).astype(sel_ref.dtype)


def _nsa_cmp_select(proj, kv_c, overlap, q_cb):
    s = proj.shape[0]
    tq = BAND_BLOCK
    gw = GROUP * HEAD_DIM
    qb = q_cb // GROUP
    ncp = kv_c.shape[1]
    return pl.pallas_call(
        _nsa_cmp_kernel,
        grid=(NSA_KV_HEADS, s // tq),
        in_specs=[
            pl.BlockSpec((tq, gw), lambda h, i: (i, qb + h)),
            pl.BlockSpec((None, ncp, HEAD_DIM), lambda h, i: (h, 0, 0)),
            pl.BlockSpec((None, ncp, HEAD_DIM), lambda h, i: (NSA_KV_HEADS + h, 0, 0)),
            pl.BlockSpec((ncp, LANES), lambda h, i: (0, 0)),
        ],
        out_specs=[
            pl.BlockSpec((tq, gw), lambda h, i: (i, h)),
            pl.BlockSpec((None, tq, LANES), lambda h, i: (h, i, 0)),
        ],
        out_shape=[
            jax.ShapeDtypeStruct((s, NSA_KV_HEADS * gw), F32),
            jax.ShapeDtypeStruct((NSA_KV_HEADS, s, LANES), BF16),
        ],
        compiler_params=_cparams("arbitrary", "arbitrary"),
        name="nsa_cmp_select",
    )(proj, kv_c, kv_c, overlap)


def _stack_heads(q_ref, qs_ref):
    tq = q_ref.shape[0]
    for g in range(GROUP):
        qs_ref[g * tq:(g + 1) * tq, :] = q_ref[:, g * HEAD_DIM:(g + 1) * HEAD_DIM]
    return qs_ref[...]


def _tile_group(bias):
    return jnp.concatenate([bias] * GROUP, axis=0)


def _nsa_slc_kernel(q_ref, k_ref, kx_ref, v_ref, sel_ref, o_ref, qa_ref, m_ref, acc_ref, *, tk):
    qi = pl.program_id(0)
    tq = q_ref.shape[0]
    rows = GROUP * tq
    gw = GROUP * HEAD_DIM
    for h in range(NSA_KV_HEADS):
        selb = sel_ref[h]
        for g in range(GROUP):
            qa_ref[h, g * tq:(g + 1) * tq, 0:HEAD_DIM] = q_ref[:, h * gw + g * HEAD_DIM:h * gw + (g + 1) * HEAD_DIM]
            qa_ref[h, g * tq:(g + 1) * tq, HEAD_DIM:] = selb
    qas = [qa_ref[h] for h in range(NSA_KV_HEADS)]

    def tiles(start, **kw):
        for h in range(NSA_KV_HEADS):
            _flash_tile(qas[h], k_ref, kx_ref, v_ref, start, tk, m_ref.at[h], acc_ref.at[h], head=h, **kw)

    last = jnp.right_shift(qi, _log2(tk // tq))
    t = qi * tq + (lax.broadcasted_iota(jnp.int32, (rows, tk), 0) & (tq - 1))
    kpos = last * tk + lax.broadcasted_iota(jnp.int32, (rows, tk), 1)
    tiles(last * tk, mask=kpos <= t, first=True)

    def body(jt, carry):
        tiles(jt * tk)
        return carry

    lax.fori_loop(0, last, body, 0)
    for h in range(NSA_KV_HEADS):
        o = _flash_finish(acc_ref.at[h])
        for g in range(GROUP):
            o_ref[:, h * gw + g * HEAD_DIM:h * gw + (g + 1) * HEAD_DIM] = o[g * tq:(g + 1) * tq, :]


def _nsa_selected(proj, selb, q_cb, k_cb, v_cb):
    s = proj.shape[0]
    tq = 2 * BAND_BLOCK
    tk = min(FLASH_TILE, s)
    nq = NSA_KV_HEADS * GROUP
    kvw = NSA_KV_HEADS * HEAD_DIM
    assert tq & (tq - 1) == 0 and s % tk == 0 and q_cb % nq == 0 and k_cb % NSA_KV_HEADS == v_cb % NSA_KV_HEADS == 0
    return pl.pallas_call(
        functools.partial(_nsa_slc_kernel, tk=tk),
        grid=(s // tq,),
        in_specs=[
            pl.BlockSpec((tq, nq * HEAD_DIM), lambda i: (i, q_cb // nq)),
            _resident((s, kvw), lambda i: (0, k_cb // NSA_KV_HEADS)),
            _resident((s, LANES), lambda i: (0, 0)),
            _resident((s, kvw), lambda i: (0, v_cb // NSA_KV_HEADS)),
            pl.BlockSpec((NSA_KV_HEADS, tq, LANES), lambda i: (0, i, 0)),
        ],
        out_specs=pl.BlockSpec((tq, nq * HEAD_DIM), lambda i: (i, 0)),
        out_shape=jax.ShapeDtypeStruct((s, nq * HEAD_DIM), F32),
        scratch_shapes=[
            pltpu.VMEM((NSA_KV_HEADS, GROUP * tq, 2 * HEAD_DIM), BF16),
            pltpu.VMEM((NSA_KV_HEADS, GROUP * tq, LANES), F32),
            pltpu.VMEM((NSA_KV_HEADS, GROUP * tq, 2 * HEAD_DIM), F32),
        ],
        compiler_params=_cparams("arbitrary"),
        name="nsa_selected",
    )(proj, proj, _block_onehot(s, NSA_SLC_BLOCK), proj, selb)


def _nsa_win_kernel(q_ref, k_ref, v_ref, gate_ref, ocmp_ref, oslc_ref, o_ref, qs_ref, m_ref, l_ref, acc_ref):
    qi = pl.program_id(1)
    tq = q_ref.shape[0]
    q = _stack_heads(q_ref, qs_ref)
    row = lax.broadcasted_iota(jnp.int32, (tq, tq), 0)
    col = lax.broadcasted_iota(jnp.int32, (tq, tq), 1)
    n_prev = -(-(NSA_WINDOW - 1) // tq)
    for b in range(n_prev + 1):
        kb = qi - b
        start = jnp.maximum(kb, 0) * tq
        dist = row - col + b * tq
        ok = (dist >= 0) & (dist < jnp.where(kb >= 0, NSA_WINDOW, 0))
        s = _dot_t(q, _rows(k_ref, start, tq)) + _tile_group(jnp.where(ok, 0.0, NEG_INF))
        if b == 0:
            _softmax_init(s, _rows(v_ref, start, tq), m_ref, l_ref, acc_ref)
        else:
            _softmax_update(s, _rows(v_ref, start, tq), m_ref, l_ref, acc_ref)
    o_win = acc_ref[...] / l_ref[...]
    sig = jax.nn.sigmoid(gate_ref[...])
    for g in range(GROUP):
        lo, hi = g * HEAD_DIM, (g + 1) * HEAD_DIM
        o = (sig[:, 3 * g:3 * g + 1] * ocmp_ref[:, lo:hi]
             + sig[:, 3 * g + 1:3 * g + 2] * oslc_ref[:, lo:hi]
             + sig[:, 3 * g + 2:3 * g + 3] * o_win[g * tq:(g + 1) * tq, :])
        o_ref[:, lo:hi] = o.astype(o_ref.dtype)


def _nsa_window_merge(proj, gates, o_cmp, o_slc, q_cb, k_cb, v_cb):
    s = proj.shape[0]
    tq = BAND_BLOCK
    gw = GROUP * HEAD_DIM
    qb = q_cb // GROUP
    return pl.pallas_call(
        _nsa_win_kernel,
        grid=(NSA_KV_HEADS, s // tq),
        in_specs=[
            pl.BlockSpec((tq, gw), lambda h, i: (i, qb + h)),
            pl.BlockSpec((s, HEAD_DIM), lambda h, i: (0, k_cb + h)),
            pl.BlockSpec((s, HEAD_DIM), lambda h, i: (0, v_cb + h)),
            pl.BlockSpec((None, tq, LANES), lambda h, i: (h, i, 0)),
            pl.BlockSpec((tq, gw), lambda h, i: (i, h)),
            pl.BlockSpec((tq, gw), lambda h, i: (i, h)),
        ],
        out_specs=pl.BlockSpec((tq, gw), lambda h, i: (i, h)),
        out_shape=jax.ShapeDtypeStruct((s, NSA_KV_HEADS * gw), BF16),
        scratch_shapes=[
            pltpu.VMEM((GROUP * tq, HEAD_DIM), BF16),
            pltpu.VMEM((GROUP * tq, 1), F32),
            pltpu.VMEM((GROUP * tq, 1), F32),
            pltpu.VMEM((GROUP * tq, HEAD_DIM), F32),
        ],
        compiler_params=_cparams("arbitrary", "arbitrary"),
        name="nsa_window_merge",
    )(proj, proj, proj, gates, o_cmp, o_slc)


def _rope_tables(seq):
    inv = 1.0 / (ROPE_THETA ** (jnp.arange(0, HEAD_DIM, 2, dtype=F32) / HEAD_DIM))
    ang = jnp.arange(seq, dtype=F32)[:, None] * inv[None, :]
    cos, sin = jnp.cos(ang), jnp.sin(ang)
    return jnp.concatenate([cos, cos], axis=-1), jnp.concatenate([-sin, sin], axis=-1)


def _overlap_matrix(seq):
    n_pad = seq // NSA_CMP_STRIDE
    n_cmp = (seq - NSA_CMP_LEN) // NSA_CMP_STRIDE + 1
    c_start = np.arange(n_pad)[:, None] * NSA_CMP_STRIDE
    s_start = np.arange(LANES)[None, :] * NSA_SLC_BLOCK
    ov = (c_start < s_start + NSA_SLC_BLOCK) & (c_start + NSA_CMP_LEN > s_start)
    ov &= (np.arange(n_pad)[:, None] < n_cmp) & (np.arange(LANES)[None, :] < seq // NSA_SLC_BLOCK)
    return jnp.asarray(ov.astype(np.float32), dtype=BF16)


def _ctypes(spec):
    return jnp.asarray(np.concatenate([np.full(n, kind, np.int32) for n, kind in spec]))


def _even_layer(x, mods, norm_g, w_in, sinks, w_out, cosf, sinf):
    sh1, sc1, g1 = mods
    heads = w_in.shape[1] // HEAD_DIM
    n_moba = 8
    ctypes = _ctypes([(n_moba, CT_ROPE_SCALE), (n_moba, CT_ROPE), (n_moba, CT_PLAIN),
                      (8, CT_ROPE_SCALE), (SWA_KV_HEADS, CT_ROPE), (SWA_KV_HEADS, CT_PLAIN)])
    assert ctypes.shape[0] == heads
    proj = _norm_proj(x, norm_g, sc1, sh1, w_in.astype(BF16), ctypes, cosf, sinf, BF16, 512)
    nb = x.shape[0] // MOBA_BLOCK
    kmean = _moba_kmean(proj, 1, n_moba * HEAD_DIM)
    kmean = jnp.pad(kmean, ((0, LANES - nb), (0, 0))).astype(BF16)
    oa = _moba_attention(proj, kmean, n_moba, 0, n_moba, 2 * n_moba)
    ob = _swa_attention(proj, sinks, 3 * n_moba, 4 * n_moba, 4 * n_moba + SWA_KV_HEADS)
    return _out_proj(oa, ob, w_out.astype(BF16), x, g1)


def _odd_layer(x, mods, norm_g, w_in, forget_b, k_pos, k_w1, k_w2, v_pos, v_w1, v_w2, w_out, cosf, sinf, overlap):
    sh1, sc1, g1 = mods
    s = x.shape[0]
    n_fox = 8
    hw = n_fox * HEAD_DIM
    kvw = NSA_KV_HEADS * HEAD_DIM
    o_fc = 3 * hw
    o_qd = o_fc + n_fox
    o_gd = o_qd + hw + 6 * kvw
    w_main = jnp.concatenate([w_in[:, :o_fc], w_in[:, o_qd:o_gd]], axis=1).astype(BF16)
    n_aux = n_fox + 3 * 8
    w_aux = jnp.concatenate([w_in[:, o_fc:o_qd], w_in[:, o_gd:], jnp.zeros((w_in.shape[0], LANES - n_aux), F32)],
                            axis=1).astype(BF16)
    ctypes = _ctypes([(8, CT_SCALE), (8, CT_PLAIN), (8, CT_PLAIN), (8, CT_ROPE_SCALE),
                      (2, CT_ROPE), (2, CT_PLAIN), (2, CT_ROPE), (2, CT_PLAIN), (2, CT_ROPE), (2, CT_PLAIN)])
    proj = _norm_proj(x, norm_g, sc1, sh1, w_main, ctypes, cosf, sinf, BF16, 512)
    aux = _norm_proj(x, norm_g, sc1, sh1, w_aux, _ctypes([(1, CT_PLAIN)]), cosf, sinf, F32, LANES)

    bias_row = jnp.pad(forget_b.astype(F32), (0, LANES - n_fox)).reshape(1, LANES)
    fox_qx, fox_kx = _fox_cum(aux, bias_row, n_fox)
    oc = _fox_attention(proj, fox_qx, fox_kx, n_fox, 0, 8, 16)

    c0 = 32
    cmp_in = proj[:, c0 * HEAD_DIM:(c0 + 4) * HEAD_DIM].reshape(s, 4, HEAD_DIM).transpose(1, 0, 2)
    cmp_in = cmp_in.reshape(4, s // NSA_CMP_STRIDE, NSA_CMP_STRIDE * HEAD_DIM)
    kv_c = _nsa_compress(
        cmp_in,
        jnp.stack([k_w1, v_w1]).astype(BF16),
        jnp.stack([k_pos.reshape(1, -1), v_pos.reshape(1, -1)]).astype(BF16),
        jnp.stack([k_w2, v_w2]).astype(BF16))
    o_cmp, sel = _nsa_cmp_select(proj, kv_c, overlap, 24)
    o_slc = _nsa_selected(proj, sel, 24, c0 + 4, c0 + 6)
    gates = aux[:, n_fox:n_aux].reshape(s, NSA_KV_HEADS, 3 * GROUP).transpose(1, 0, 2)
    gates = jnp.pad(gates, ((0, 0), (0, 0), (0, LANES - 3 * GROUP)))
    od = _nsa_window_merge(proj, gates, o_cmp, o_slc, 24, c0 + 8, c0 + 10)
    return _out_proj(oc, od, w_out.astype(BF16), x, g1)


def kernel(x, c, norm_mix_g, norm_mlp_g, ada_w, ada_b, mlp_up, mlp_down, even_w_in, even_sinks, even_w_out, odd_w_in, fox_forget_b, nsa_k_pos, nsa_k_w1, nsa_k_w2, nsa_v_pos, nsa_v_w1, nsa_v_w2, odd_w_out, final_norm_g):
    batch, seq, d = x.shape
    assert batch == 1
    depth = ada_w.shape[0]
    cosf, sinf = _rope_tables(seq)
    overlap = _overlap_matrix(seq)
    mod = _ada_mod(c, ada_w, ada_b).reshape(depth, 6, 1, d)
    xs = x[0]
    for i in range(depth):
        sh1, sc1, g1, sh2, sc2, g2 = [mod[i, t] for t in range(6)]
        ng = norm_mix_g[i].reshape(1, d)
        j = i // 2
        if i % 2 == 0:
            xs = _even_layer(xs, (sh1, sc1, g1), ng, even_w_in[j], even_sinks[j], even_w_out[j], cosf, sinf)
        else:
            xs = _odd_layer(xs, (sh1, sc1, g1), ng, odd_w_in[j], fox_forget_b[j], nsa_k_pos[j], nsa_k_w1[j],
                            nsa_k_w2[j], nsa_v_pos[j], nsa_v_w1[j], nsa_v_w2[j], odd_w_out[j], cosf, sinf, overlap)
        xs = _mlp(xs, norm_mlp_g[i].reshape(1, d), sc2, sh2, mlp_up[i].astype(BF16), mlp_down[i].astype(BF16), g2)
    return _final_norm(xs, final_norm_g.reshape(1, d))[None]
```

```python
import functools
import math

import numpy as np
import jax
import jax.numpy as jnp
from jax import lax
from jax.experimental import pallas as pl
from jax.experimental.pallas import tpu as pltpu

HEAD_DIM = 128
ROPE_THETA = 10000.0
NORM_EPS = 1e-6
MOBA_BLOCK = 256
MOBA_TOPK = 3
SWA_KV_HEADS = 2
SWA_WINDOW = 128
NSA_KV_HEADS = 2
NSA_CMP_LEN = 32
NSA_CMP_STRIDE = 16
NSA_SLC_BLOCK = 64
SLC_SHIFT = 6
NSA_SLC_TOPK = 16
NSA_WINDOW = 512
BAND_BLOCK = 128
NEG_INF = -1e30
SCALE = HEAD_DIM ** -0.5
LANES = 128
GROUP = 4
VMEM_LIMIT = 48 * 1024 * 1024

F32 = jnp.float32
BF16 = jnp.bfloat16

CT_PLAIN, CT_ROPE, CT_ROPE_SCALE, CT_SCALE = 0, 1, 2, 3


def _cparams(*sem):
    return pltpu.CompilerParams(dimension_semantics=sem, vmem_limit_bytes=VMEM_LIMIT)


def _dot(a, b):
    return jnp.dot(a, b, preferred_element_type=F32)


def _dot_t(a, b):
    return lax.dot_general(a, b, (((1,), (1,)), ((), ())), preferred_element_type=F32)


def _ada_kernel(c_ref, w_ref, b_ref, o_ref):
    c = c_ref[...]
    cond = c * jax.nn.sigmoid(c)
    cond8 = jnp.broadcast_to(cond, (8, cond.shape[1])).astype(BF16)
    y = _dot(cond8, w_ref[...].astype(BF16))
    o_ref[...] = y[0:1] + b_ref[...]


def _ada_mod(c, ada_w, ada_b):
    depth, d, n = ada_w.shape
    tn = 1024
    return pl.pallas_call(
        _ada_kernel,
        grid=(depth, n // tn),
        in_specs=[
            pl.BlockSpec((1, d), lambda l, j: (0, 0)),
            pl.BlockSpec((None, d, tn), lambda l, j: (l, 0, j)),
            pl.BlockSpec((None, 1, tn), lambda l, j: (l, 0, j)),
        ],
        out_specs=pl.BlockSpec((None, 1, tn), lambda l, j: (l, 0, j)),
        out_shape=jax.ShapeDtypeStruct((depth, 1, n), F32),
        compiler_params=_cparams("arbitrary", "arbitrary"),
        name="ada_mod",
    )(c, ada_w, ada_b.reshape(depth, 1, n))


def _norm_mod(x, g, sc, sh):
    y = x * lax.rsqrt(jnp.mean(x * x, axis=-1, keepdims=True) + NORM_EPS)
    return (y * g) * (1.0 + sc) + sh


def _proj_kernel(ct_ref, x_ref, g_ref, sc_ref, sh_ref, w_ref, ta_ref, tb_ref, o_ref, h_ref):
    j = pl.program_id(1)

    @pl.when(j == 0)
    def _():
        h_ref[...] = _norm_mod(x_ref[...], g_ref[...], sc_ref[...], sh_ref[...]).astype(BF16)

    y = _dot(h_ref[...], w_ref[...])
    groups = y.shape[1] // LANES
    for gi in range(groups):
        lo, hi = gi * LANES, (gi + 1) * LANES
        yg = y[:, lo:hi]
        ct = ct_ref[j * groups + gi]
        r = yg * ta_ref[ct] + pltpu.roll(yg, HEAD_DIM // 2, 1) * tb_ref[ct]
        o_ref[:, lo:hi] = r.astype(o_ref.dtype)


def _epilogue_tables(seq):
    inv = 1.0 / (ROPE_THETA ** (jnp.arange(0, HEAD_DIM, 2, dtype=F32) / HEAD_DIM))
    ang = jnp.arange(seq, dtype=F32)[:, None] * inv[None, :]
    cos, sin = jnp.cos(ang), jnp.sin(ang)
    cosf = jnp.concatenate([cos, cos], axis=-1)
    sinf = jnp.concatenate([-sin, sin], axis=-1)
    one, zero = jnp.ones_like(cosf), jnp.zeros_like(cosf)
    by_kind = {CT_PLAIN: (one, zero), CT_ROPE: (cosf, sinf), CT_ROPE_SCALE: (cosf * SCALE, sinf * SCALE),
               CT_SCALE: (one * SCALE, zero)}
    return (jnp.stack([by_kind[k][0] for k in range(4)]), jnp.stack([by_kind[k][1] for k in range(4)]))


def _norm_proj(x, g, sc, sh, w, ctypes, tables, out_dtype, tn):
    s, d = x.shape
    n = w.shape[1]
    tm = min(1024, s)
    kinds = tables[0].shape[0]
    grid_spec = pltpu.PrefetchScalarGridSpec(
        num_scalar_prefetch=1,
        grid=(s // tm, n // tn),
        in_specs=[
            pl.BlockSpec((tm, d), lambda i, j, ct: (i, 0)),
            pl.BlockSpec((1, d), lambda i, j, ct: (0, 0)),
            pl.BlockSpec((1, d), lambda i, j, ct: (0, 0)),
            pl.BlockSpec((1, d), lambda i, j, ct: (0, 0)),
            pl.BlockSpec((d, tn), lambda i, j, ct: (0, j)),
            pl.BlockSpec((kinds, tm, LANES), lambda i, j, ct: (0, i, 0)),
            pl.BlockSpec((kinds, tm, LANES), lambda i, j, ct: (0, i, 0)),
        ],
        out_specs=pl.BlockSpec((tm, tn), lambda i, j, ct: (i, j)),
        scratch_shapes=[pltpu.VMEM((tm, d), BF16)],
    )
    return pl.pallas_call(
        _proj_kernel,
        grid_spec=grid_spec,
        out_shape=jax.ShapeDtypeStruct((s, n), out_dtype),
        compiler_params=_cparams("arbitrary", "arbitrary"),
        name="norm_proj",
    )(ctypes, x, g, sc, sh, w, *tables)


def _outproj_kernel(a_ref, b_ref, w_ref, x_ref, g_ref, o_ref):
    ha = a_ref.shape[1]
    y = _dot(a_ref[...], w_ref[0:ha, :]) + _dot(b_ref[...], w_ref[ha:, :])
    o_ref[...] = x_ref[...] + g_ref[...] * y


def _out_proj(oa, ob, w, x, gate):
    s, d = x.shape
    ha, hb = oa.shape[1], ob.shape[1]
    tm, tn = min(1024, s), 512
    return pl.pallas_call(
        _outproj_kernel,
        grid=(s // tm, d // tn),
        in_specs=[
            pl.BlockSpec((tm, ha), lambda i, j: (i, 0)),
            pl.BlockSpec((tm, hb), lambda i, j: (i, 0)),
            pl.BlockSpec((ha + hb, tn), lambda i, j: (0, j)),
            pl.BlockSpec((tm, tn), lambda i, j: (i, j)),
            pl.BlockSpec((1, tn), lambda i, j: (0, j)),
        ],
        out_specs=pl.BlockSpec((tm, tn), lambda i, j: (i, j)),
        out_shape=jax.ShapeDtypeStruct((s, d), F32),
        compiler_params=_cparams("arbitrary", "arbitrary"),
        name="out_proj",
    )(oa, ob, w, x, gate)


def _mlp_kernel(x_ref, g_ref, sc_ref, sh_ref, up_ref, down_ref, gate_ref, o_ref, h_ref, acc_ref):
    f = pl.program_id(1)

    @pl.when(f == 0)
    def _():
        h_ref[...] = _norm_mod(x_ref[...], g_ref[...], sc_ref[...], sh_ref[...]).astype(BF16)
        acc_ref[...] = jnp.zeros_like(acc_ref)

    hid = jnp.maximum(_dot(h_ref[...], up_ref[...]), 0.0)
    acc_ref[...] += _dot((hid * hid).astype(BF16), down_ref[...])

    @pl.when(f == pl.num_programs(1) - 1)
    def _():
        o_ref[...] = x_ref[...] + gate_ref[...] * acc_ref[...]


def _mlp(x, g, sc, sh, up, down, gate):
    s, d = x.shape
    ff = up.shape[1]
    tm, tf = min(512, s), 512
    return pl.pallas_call(
        _mlp_kernel,
        grid=(s // tm, ff // tf),
        in_specs=[
            pl.BlockSpec((tm, d), lambda i, f: (i, 0)),
            pl.BlockSpec((1, d), lambda i, f: (0, 0)),
            pl.BlockSpec((1, d), lambda i, f: (0, 0)),
            pl.BlockSpec((1, d), lambda i, f: (0, 0)),
            pl.BlockSpec((d, tf), lambda i, f: (0, f)),
            pl.BlockSpec((tf, d), lambda i, f: (f, 0)),
            pl.BlockSpec((1, d), lambda i, f: (0, 0)),
        ],
        out_specs=pl.BlockSpec((tm, d), lambda i, f: (i, 0)),
        out_shape=jax.ShapeDtypeStruct((s, d), F32),
        scratch_shapes=[pltpu.VMEM((tm, d), BF16), pltpu.VMEM((tm, d), F32)],
        compiler_params=_cparams("arbitrary", "arbitrary"),
        name="mlp",
    )(x, g, sc, sh, up, down, gate)


def _final_norm_kernel(x_ref, g_ref, o_ref):
    x = x_ref[...]
    o_ref[...] = x * lax.rsqrt(jnp.mean(x * x, axis=-1, keepdims=True) + NORM_EPS) * g_ref[...]


def _final_norm(x, g):
    s, d = x.shape
    tm = min(512, s)
    return pl.pallas_call(
        _final_norm_kernel,
        grid=(s // tm,),
        in_specs=[pl.BlockSpec((tm, d), lambda i: (i, 0)), pl.BlockSpec((1, d), lambda i: (0, 0))],
        out_specs=pl.BlockSpec((tm, d), lambda i: (i, 0)),
        out_shape=jax.ShapeDtypeStruct((s, d), F32),
        compiler_params=_cparams("arbitrary"),
        name="final_norm",
    )(x, g)


def _rows(ref, start, size):
    return ref[pl.ds(pl.multiple_of(start, size), size), :]


def _log2(n):
    assert n > 0 and n & (n - 1) == 0
    return n.bit_length() - 1


FLASH_TILE = 512
HEADS_PER_STEP = 4


def _resident(shape, index_map):
    return pl.BlockSpec(shape, index_map, pipeline_mode=pl.Buffered(1))


def _flash_tile(qa, k_ref, kx_ref, v_ref, start, tk, m_ref, acc_ref, mask=None, first=False, head=0):
    cols = slice(head * HEAD_DIM, (head + 1) * HEAD_DIM)
    rows = pl.ds(pl.multiple_of(start, tk), tk)
    ka = jnp.concatenate([k_ref[rows, cols], _rows(kx_ref, start, tk)], axis=1)
    s = _dot_t(qa, ka)
    if mask is not None:
        s = jnp.where(mask, s, NEG_INF)
    chunks = [s[:, c * LANES:(c + 1) * LANES] for c in range(tk // LANES)]
    m_cur = jnp.max(functools.reduce(jnp.maximum, chunks), axis=1, keepdims=True)
    if first:
        m_new = jnp.broadcast_to(m_cur, (s.shape[0], LANES))
    else:
        m_prev = m_ref[...]
        m_new = jnp.maximum(m_prev, m_cur)
    p = jnp.concatenate([jnp.exp(c - m_new) for c in chunks], axis=1).astype(BF16)
    va = jnp.concatenate([v_ref[rows, cols], jnp.ones((tk, LANES), BF16)], axis=1)
    pv = _dot(p, va)
    if first:
        acc_ref[...] = pv
    else:
        alpha = jnp.exp(m_prev - m_new)
        acc_ref[...] = jnp.concatenate([alpha, alpha], axis=1) * acc_ref[...] + pv
    m_ref[...] = m_new


def _flash_finish(acc_ref):
    acc = acc_ref[...]
    return acc[:, :HEAD_DIM] / acc[:, HEAD_DIM:]


def _block_onehot(seq, block):
    return jnp.asarray((np.arange(seq)[:, None] // block == np.arange(LANES)[None, :]).astype(np.float32), dtype=BF16)


def _kmean_kernel(k_ref, o_ref):
    k = k_ref[...].astype(F32)
    o_ref[...] = jnp.sum(k, axis=0, keepdims=True) * (1.0 / k.shape[0])


def _moba_kmean(proj, col_block, width):
    s = proj.shape[0]
    nb = s // MOBA_BLOCK
    out = pl.pallas_call(
        _kmean_kernel,
        grid=(nb,),
        in_specs=[pl.BlockSpec((MOBA_BLOCK, width), lambda b: (b, col_block))],
        out_specs=pl.BlockSpec((None, 1, width), lambda b: (b, 0, 0)),
        out_shape=jax.ShapeDtypeStruct((nb, 1, width), F32),
        compiler_params=_cparams("arbitrary"),
        name="moba_kmean",
    )(proj)
    return out.reshape(nb, width)


def _moba_kernel(q_ref, k_ref, kx_ref, v_ref, km_ref, o_ref, m_ref, acc_ref, *, tk, hp):
    qi = pl.program_id(1)
    tq = q_ref.shape[0]
    blk = lax.broadcasted_iota(jnp.int32, (tq, LANES), 1)
    blkf = blk.astype(F32)
    own = jnp.right_shift(qi * tq + lax.broadcasted_iota(jnp.int32, (tq, LANES), 0), _log2(MOBA_BLOCK))
    ownf = own.astype(F32)

    qas = []
    for hh in range(hp):
        cols = slice(hh * HEAD_DIM, (hh + 1) * HEAD_DIM)
        q = q_ref[:, cols]
        g = jnp.where(blk < own, _dot_t(q, km_ref[:, cols]), NEG_INF)
        sel = jnp.zeros((tq, LANES), F32)
        for _ in range(MOBA_TOPK):
            mx = jnp.max(g, axis=1, keepdims=True)
            idx = jnp.min(jnp.where(g == mx, blkf, float(LANES)), axis=1, keepdims=True)
            pick = blkf == idx
            sel = jnp.where(pick & (idx < ownf), 1.0, sel)
            g = jnp.where(pick, -jnp.inf, g)
        qx = jnp.where((sel > 0.5) | (blk == own), 0.0, NEG_INF).astype(BF16)
        qas.append(jnp.concatenate([q, qx], axis=1))

    def tiles(start, **kw):
        for hh in range(hp):
            _flash_tile(qas[hh], k_ref, kx_ref, v_ref, start, tk, m_ref.at[hh], acc_ref.at[hh], head=hh, **kw)

    last = jnp.right_shift(qi, _log2(tk // tq))
    t = qi * tq + lax.broadcasted_iota(jnp.int32, (tq, tk), 0)
    kpos = last * tk + lax.broadcasted_iota(jnp.int32, (tq, tk), 1)
    tiles(last * tk, mask=kpos <= t, first=True)

    def body(p, carry):
        tiles(p * tk)
        return carry

    lax.fori_loop(0, last, body, 0)
    for hh in range(hp):
        o_ref[:, hh * HEAD_DIM:(hh + 1) * HEAD_DIM] = _flash_finish(acc_ref.at[hh]).astype(o_ref.dtype)


def _moba_attention(proj, kmean, n_heads, q_cb, k_cb, v_cb):
    s = proj.shape[0]
    tq = tk = min(FLASH_TILE, s)
    hp = HEADS_PER_STEP
    hw = hp * HEAD_DIM
    assert s % tk == 0 and tk % MOBA_BLOCK == 0 and n_heads % hp == 0 and q_cb % hp == k_cb % hp == v_cb % hp == 0
    return pl.pallas_call(
        functools.partial(_moba_kernel, tk=tk, hp=hp),
        grid=(n_heads // hp, s // tq),
        in_specs=[
            pl.BlockSpec((tq, hw), lambda h, i: (i, q_cb // hp + h)),
            _resident((s, hw), lambda h, i: (0, k_cb // hp + h)),
            _resident((s, LANES), lambda h, i: (0, 0)),
            _resident((s, hw), lambda h, i: (0, v_cb // hp + h)),
            pl.BlockSpec((LANES, hw), lambda h, i: (0, h)),
        ],
        out_specs=pl.BlockSpec((tq, hw), lambda h, i: (i, h)),
        out_shape=jax.ShapeDtypeStruct((s, n_heads * HEAD_DIM), BF16),
        scratch_shapes=[pltpu.VMEM((hp, tq, LANES), F32), pltpu.VMEM((hp, tq, 2 * HEAD_DIM), F32)],
        compiler_params=_cparams("arbitrary", "arbitrary"),
        name="moba_attn",
    )(proj, proj, _block_onehot(s, MOBA_BLOCK), proj, kmean)


def _swa_kernel(sink_ref, q_ref, kp_ref, ko_ref, vp_ref, vo_ref, o_ref):
    kvh = pl.program_id(0)
    qi = pl.program_id(1)
    tq = q_ref.shape[0]
    row = lax.broadcasted_iota(jnp.int32, (tq, tq), 0)
    col = lax.broadcasted_iota(jnp.int32, (tq, tq), 1)
    ok_own = col <= row
    ok_prev = col - row > jnp.where(qi > 0, 0, tq)
    kp, ko, vp, vo = kp_ref[...], ko_ref[...], vp_ref[...], vo_ref[...]
    for g in range(GROUP):
        lo, hi = g * HEAD_DIM, (g + 1) * HEAD_DIM
        q = q_ref[:, lo:hi]
        sp = jnp.where(ok_prev, _dot_t(q, kp), NEG_INF)
        so = jnp.where(ok_own, _dot_t(q, ko), NEG_INF)
        sink = sink_ref[kvh * GROUP + g]
        m = jnp.maximum(jnp.maximum(jnp.max(sp, axis=1, keepdims=True), jnp.max(so, axis=1, keepdims=True)), sink)
        pp = jnp.where(ok_prev, jnp.exp(sp - m), 0.0)
        po = jnp.where(ok_own, jnp.exp(so - m), 0.0)
        den = jnp.sum(pp, axis=1, keepdims=True) + jnp.sum(po, axis=1, keepdims=True) + jnp.exp(sink - m)
        o = (_dot(pp.astype(BF16), vp) + _dot(po.astype(BF16), vo)) / den
        o_ref[:, lo:hi] = o.astype(o_ref.dtype)


def _swa_attention(proj, sinks, q_cb, k_cb, v_cb):
    s = proj.shape[0]
    tq = BAND_BLOCK
    gw = GROUP * HEAD_DIM
    qb = q_cb // GROUP
    prev = lambda i: jnp.maximum(i - 1, 0)
    return pl.pallas_call(
        _swa_kernel,
        grid=(SWA_KV_HEADS, s // tq),
        in_specs=[
            pl.BlockSpec(memory_space=pltpu.SMEM),
            pl.BlockSpec((tq, gw), lambda h, i: (i, qb + h)),
            pl.BlockSpec((tq, HEAD_DIM), lambda h, i: (prev(i), k_cb + h)),
            pl.BlockSpec((tq, HEAD_DIM), lambda h, i: (i, k_cb + h)),
            pl.BlockSpec((tq, HEAD_DIM), lambda h, i: (prev(i), v_cb + h)),
            pl.BlockSpec((tq, HEAD_DIM), lambda h, i: (i, v_cb + h)),
        ],
        out_specs=pl.BlockSpec((tq, gw), lambda h, i: (i, h)),
        out_shape=jax.ShapeDtypeStruct((s, SWA_KV_HEADS * gw), BF16),
        compiler_params=_cparams("arbitrary", "arbitrary"),
        name="swa_attn",
    )(sinks, proj, proj, proj, proj, proj)


FOX_LANES = 16


def _split3(x):
    x1 = x.astype(BF16)
    r1 = x - x1.astype(F32)
    x2 = r1.astype(BF16)
    return x1, x2, (r1 - x2.astype(F32)).astype(BF16)


def _fox_tables(n_heads):
    place = np.zeros((6, LANES, LANES), np.float32)
    ones_q = np.zeros((1, LANES), np.float32)
    ones_k = np.zeros((1, LANES), np.float32)
    for h in range(n_heads):
        for i in range(6):
            place[i, h, FOX_LANES * h + i] = 1.0
        ones_q[0, FOX_LANES * h + 3:FOX_LANES * h + 6] = 1.0
        ones_k[0, FOX_LANES * h:FOX_LANES * h + 3] = 1.0
    return jnp.asarray(place, dtype=BF16), jnp.asarray(ones_q), jnp.asarray(ones_k)


def _fox_cum_kernel(aux_ref, b_ref, place_ref, oq_ref, ok_ref, qx_ref, kx_ref, carry_ref):
    i = pl.program_id(0)

    @pl.when(i == 0)
    def _():
        carry_ref[...] = jnp.zeros_like(carry_ref)

    x = aux_ref[...] + b_ref[...]
    log_f = -(jnp.maximum(-x, 0.0) + jnp.log(1.0 + jnp.exp(-jnp.abs(x))))
    tb = x.shape[0]
    r = lax.broadcasted_iota(jnp.int32, (tb, tb), 0)
    c = lax.broadcasted_iota(jnp.int32, (tb, tb), 1)
    tri = jnp.where(c <= r, 1.0, 0.0).astype(BF16)
    x1, x2, x3 = _split3(log_f)
    cum = _dot(tri, x1) + _dot(tri, x2) + _dot(tri, x3) + carry_ref[...]
    carry_ref[...] = cum[tb - 1:tb, :]
    c1, c2, c3 = _split3(cum)
    qx = _dot(c1, place_ref[0]) + _dot(c2, place_ref[1]) + _dot(c3, place_ref[2]) + oq_ref[...]
    kx = ok_ref[...] - (_dot(c1, place_ref[3]) + _dot(c2, place_ref[4]) + _dot(c3, place_ref[5]))
    qx_ref[...] = qx.astype(BF16)
    kx_ref[...] = kx.astype(BF16)


def _fox_cum(aux, bias_row, n_heads):
    s, w = aux.shape
    tb = min(512, s)
    place, ones_q, ones_k = _fox_tables(n_heads)
    full = lambda shape: pl.BlockSpec(shape, lambda i: (0,) * len(shape))
    return pl.pallas_call(
        _fox_cum_kernel,
        grid=(s // tb,),
        in_specs=[pl.BlockSpec((tb, w), lambda i: (i, 0)), full((1, w)), full(place.shape), full((1, w)), full((1, w))],
        out_specs=[pl.BlockSpec((tb, w), lambda i: (i, 0)), pl.BlockSpec((tb, w), lambda i: (i, 0))],
        out_shape=[jax.ShapeDtypeStruct((s, w), BF16), jax.ShapeDtypeStruct((s, w), BF16)],
        scratch_shapes=[pltpu.VMEM((1, w), F32)],
        compiler_params=_cparams("arbitrary"),
        name="fox_cum",
    )(aux, bias_row, place, ones_q, ones_k)


def _fox_kernel(q_ref, qx_ref, k_ref, kx_ref, v_ref, o_ref, m_ref, acc_ref, *, tk, hp):
    hb = pl.program_id(0)
    qi = pl.program_id(1)
    tq = q_ref.shape[0]
    lane = lax.broadcasted_iota(jnp.int32, (tq, LANES), 1)
    qxf = qx_ref[...].astype(F32)
    qas = []
    for hh in range(hp):
        mine = jnp.right_shift(lane, 4) == hb * hp + hh
        qx = jnp.where(mine, qxf, 0.0).astype(BF16)
        qas.append(jnp.concatenate([q_ref[:, hh * HEAD_DIM:(hh + 1) * HEAD_DIM], qx], axis=1))

    def tiles(start, **kw):
        for hh in range(hp):
            _flash_tile(qas[hh], k_ref, kx_ref, v_ref, start, tk, m_ref.at[hh], acc_ref.at[hh], head=hh, **kw)

    last = jnp.right_shift(qi, _log2(tk // tq))
    t = qi * tq + lax.broadcasted_iota(jnp.int32, (tq, tk), 0)
    kpos = last * tk + lax.broadcasted_iota(jnp.int32, (tq, tk), 1)
    tiles(last * tk, mask=kpos <= t, first=True)

    def body(p, carry):
        tiles(p * tk)
        return carry

    lax.fori_loop(0, last, body, 0)
    for hh in range(hp):
        o_ref[:, hh * HEAD_DIM:(hh + 1) * HEAD_DIM] = _flash_finish(acc_ref.at[hh]).astype(o_ref.dtype)


def _fox_attention(proj, qx, kx, n_heads, q_cb, k_cb, v_cb):
    s = proj.shape[0]
    tq = tk = min(FLASH_TILE, s)
    hp = HEADS_PER_STEP
    hw = hp * HEAD_DIM
    assert s % tk == 0 and FOX_LANES == 16 and n_heads % hp == 0 and q_cb % hp == k_cb % hp == v_cb % hp == 0
    return pl.pallas_call(
        functools.partial(_fox_kernel, tk=tk, hp=hp),
        grid=(n_heads // hp, s // tq),
        in_specs=[
            pl.BlockSpec((tq, hw), lambda h, i: (i, q_cb // hp + h)),
            pl.BlockSpec((tq, LANES), lambda h, i: (i, 0)),
            _resident((s, hw), lambda h, i: (0, k_cb // hp + h)),
            _resident((s, LANES), lambda h, i: (0, 0)),
            _resident((s, hw), lambda h, i: (0, v_cb // hp + h)),
        ],
        out_specs=pl.BlockSpec((tq, hw), lambda h, i: (i, h)),
        out_shape=jax.ShapeDtypeStruct((s, n_heads * HEAD_DIM), BF16),
        scratch_shapes=[pltpu.VMEM((hp, tq, LANES), F32), pltpu.VMEM((hp, tq, 2 * HEAD_DIM), F32)],
        compiler_params=_cparams("arbitrary", "arbitrary"),
        name="fox_attn",
    )(proj, qx, proj, kx, proj)


def _gelu_tanh(x):
    return 0.5 * x * (1.0 + jnp.tanh(math.sqrt(2.0 / math.pi) * (x + 0.044715 * (x * x * x))))


def _nsa_compress_kernel(a_ref, w1_ref, pos_ref, w2_ref, o_ref):
    a = a_ref[...]
    half = a.shape[1]
    n = a.shape[0]
    lo = _dot(a, w1_ref[0:half, :])
    hi = _dot(a, w1_ref[half:, :])
    pos8 = jnp.broadcast_to(pos_ref[...], (8, pos_ref.shape[1]))
    pterm = _dot(pos8, w1_ref[...])[0:1]
    pre = lo + pltpu.roll(hi, n - 1, 0) + pterm
    o_ref[...] = _dot(_gelu_tanh(pre).astype(BF16), w2_ref[...]).astype(o_ref.dtype)


def _nsa_compress(a, w1, pos, w2):
    n, rows, width = a.shape
    return pl.pallas_call(
        _nsa_compress_kernel,
        grid=(n,),
        in_specs=[
            pl.BlockSpec((None, rows, width), lambda i: (i, 0, 0)),
            pl.BlockSpec((None, 2 * width, HEAD_DIM), lambda i: (i // NSA_KV_HEADS, 0, 0)),
            pl.BlockSpec((None, 1, 2 * width), lambda i: (i // NSA_KV_HEADS, 0, 0)),
            pl.BlockSpec((None, HEAD_DIM, HEAD_DIM), lambda i: (i // NSA_KV_HEADS, 0, 0)),
        ],
        out_specs=pl.BlockSpec((None, rows, HEAD_DIM), lambda i: (i, 0, 0)),
        out_shape=jax.ShapeDtypeStruct((n, rows, HEAD_DIM), BF16),
        compiler_params=_cparams("arbitrary"),
        name="nsa_compress",
    )(a, w1, pos, w2)


def _nsa_cmp_kernel(q_ref, kc_ref, vc_ref, ovt_ref, ocmp_ref, sel_ref):
    qi = pl.program_id(0)
    tq = q_ref.shape[0]
    ncp = kc_ref.shape[1]
    gw = GROUP * HEAD_DIM
    t = qi * tq + lax.broadcasted_iota(jnp.int32, (tq, ncp), 0)
    n = lax.broadcasted_iota(jnp.int32, (tq, ncp), 1)
    vis = (n * NSA_CMP_STRIDE + (NSA_CMP_LEN - 1)) <= t
    ones = jnp.ones((ncp, LANES), BF16)
    blk = lax.broadcasted_iota(jnp.int32, (LANES, tq), 0)
    blkf = blk.astype(F32)
    cur = (qi * tq + lax.broadcasted_iota(jnp.int32, (LANES, tq), 1)) >> SLC_SHIFT
    forced = (blk == 0) | (blk == cur) | (blk == cur - 1)
    future = blk > cur
    ovt = ovt_ref[...]

    for h in range(NSA_KV_HEADS):
        kc = kc_ref[h]
        va = jnp.concatenate([vc_ref[h], ones], axis=1)
        psum = jnp.zeros((tq, ncp), F32)
        for g in range(GROUP):
            cols = slice(h * gw + g * HEAD_DIM, h * gw + (g + 1) * HEAD_DIM)
            s = jnp.where(vis, _dot_t(q_ref[:, cols], kc), NEG_INF)
            chunks = [s[:, c * LANES:(c + 1) * LANES] for c in range(ncp // LANES)]
            m = jnp.max(functools.reduce(jnp.maximum, chunks), axis=1, keepdims=True)
            p = jnp.where(vis, jnp.exp(s - m), 0.0)
            pv = _dot(p.astype(BF16), va)
            inv = 1.0 / jnp.maximum(pv[:, HEAD_DIM:], 1e-30)
            ocmp_ref[:, cols] = pv[:, :HEAD_DIM] * inv
            psum = psum + p * jnp.concatenate([inv] * (ncp // LANES), axis=1)

        p_hi = psum.astype(BF16)
        p_lo = (psum - p_hi.astype(F32)).astype(BF16)
        imp = _dot_t(ovt, p_hi) + _dot_t(ovt, p_lo)
        imp = jnp.where(forced, jnp.inf, imp)
        imp = jnp.where(future, -jnp.inf, imp)
        sel = jnp.zeros(imp.shape, F32)
        for _ in range(NSA_SLC_TOPK):
            mx = jnp.max(imp, axis=0, keepdims=True)
            idx = jnp.min(jnp.where(imp == mx, blkf, float(LANES)), axis=0, keepdims=True)
            pick = blkf == idx
            sel = jnp.where(pick, 1.0, sel)
            imp = jnp.where(pick, -jnp.inf, imp)
        sel_ref[h] = jnp.where(sel > 0.5, 0.0, NEG_INF).T.astype(sel_ref.dtype)


def _nsa_cmp_select(proj, kv_c, overlap_t, q_cb):
    s = proj.shape[0]
    tq = BAND_BLOCK
    nq = NSA_KV_HEADS * GROUP
    ncp = kv_c.shape[1]
    assert q_cb % nq == 0 and tq == LANES
    return pl.pallas_call(
        _nsa_cmp_kernel,
        grid=(s // tq,),
        in_specs=[
            pl.BlockSpec((tq, nq * HEAD_DIM), lambda i: (i, q_cb // nq)),
            pl.BlockSpec((NSA_KV_HEADS, ncp, HEAD_DIM), lambda i: (0, 0, 0)),
            pl.BlockSpec((NSA_KV_HEADS, ncp, HEAD_DIM), lambda i: (1, 0, 0)),
            pl.BlockSpec((LANES, ncp), lambda i: (0, 0)),
        ],
        out_specs=[
            pl.BlockSpec((tq, nq * HEAD_DIM), lambda i: (i, 0)),
            pl.BlockSpec((NSA_KV_HEADS, tq, LANES), lambda i: (0, i, 0)),
        ],
        out_shape=[
            jax.ShapeDtypeStruct((s, nq * HEAD_DIM), F32),
            jax.ShapeDtypeStruct((NSA_KV_HEADS, s, LANES), BF16),
        ],
        compiler_params=_cparams("arbitrary"),
        name="nsa_cmp_select",
    )(proj, kv_c, kv_c, overlap_t)


def _nsa_slc_kernel(q_ref, k_ref, kx_ref, v_ref, sel_ref, o_ref, qa_ref, m_ref, acc_ref, *, tk):
    qi = pl.program_id(0)
    tq = q_ref.shape[0]
    rows = GROUP * tq
    gw = GROUP * HEAD_DIM
    for h in range(NSA_KV_HEADS):
        selb = sel_ref[h]
        for g in range(GROUP):
            qa_ref[h, g * tq:(g + 1) * tq, 0:HEAD_DIM] = q_ref[:, h * gw + g * HEAD_DIM:h * gw + (g + 1) * HEAD_DIM]
            qa_ref[h, g * tq:(g + 1) * tq, HEAD_DIM:] = selb
    qas = [qa_ref[h] for h in range(NSA_KV_HEADS)]

    def tiles(start, **kw):
        for h in range(NSA_KV_HEADS):
            _flash_tile(qas[h], k_ref, kx_ref, v_ref, start, tk, m_ref.at[h], acc_ref.at[h], head=h, **kw)

    last = jnp.right_shift(qi, _log2(tk // tq))
    t = qi * tq + (lax.broadcasted_iota(jnp.int32, (rows, tk), 0) & (tq - 1))
    kpos = last * tk + lax.broadcasted_iota(jnp.int32, (rows, tk), 1)
    tiles(last * tk, mask=kpos <= t, first=True)

    def body(jt, carry):
        tiles(jt * tk)
        return carry

    lax.fori_loop(0, last, body, 0)
    for h in range(NSA_KV_HEADS):
        o = _flash_finish(acc_ref.at[h])
        for g in range(GROUP):
            o_ref[:, h * gw + g * HEAD_DIM:h * gw + (g + 1) * HEAD_DIM] = o[g * tq:(g + 1) * tq, :]


def _nsa_selected(proj, selb, q_cb, k_cb, v_cb):
    s = proj.shape[0]
    tq = 2 * BAND_BLOCK
    tk = min(FLASH_TILE, s)
    nq = NSA_KV_HEADS * GROUP
    kvw = NSA_KV_HEADS * HEAD_DIM
    assert tq & (tq - 1) == 0 and s % tk == 0 and q_cb % nq == 0 and k_cb % NSA_KV_HEADS == v_cb % NSA_KV_HEADS == 0
    return pl.pallas_call(
        functools.partial(_nsa_slc_kernel, tk=tk),
        grid=(s // tq,),
        in_specs=[
            pl.BlockSpec((tq, nq * HEAD_DIM), lambda i: (i, q_cb // nq)),
            _resident((s, kvw), lambda i: (0, k_cb // NSA_KV_HEADS)),
            _resident((s, LANES), lambda i: (0, 0)),
            _resident((s, kvw), lambda i: (0, v_cb // NSA_KV_HEADS)),
            pl.BlockSpec((NSA_KV_HEADS, tq, LANES), lambda i: (0, i, 0)),
        ],
        out_specs=pl.BlockSpec((tq, nq * HEAD_DIM), lambda i: (i, 0)),
        out_shape=jax.ShapeDtypeStruct((s, nq * HEAD_DIM), F32),
        scratch_shapes=[
            pltpu.VMEM((NSA_KV_HEADS, GROUP * tq, 2 * HEAD_DIM), BF16),
            pltpu.VMEM((NSA_KV_HEADS, GROUP * tq, LANES), F32),
            pltpu.VMEM((NSA_KV_HEADS, GROUP * tq, 2 * HEAD_DIM), F32),
        ],
        compiler_params=_cparams("arbitrary"),
        name="nsa_selected",
    )(proj, proj, _block_onehot(s, NSA_SLC_BLOCK), proj, selb)


def _nsa_win_kernel(q_ref, k_ref, v_ref, gate_ref, ocmp_ref, oslc_ref, o_ref, qs_ref):
    qi = pl.program_id(0)
    tq = q_ref.shape[0]
    gw = GROUP * HEAD_DIM
    rows = GROUP * tq
    n_prev = -(-(NSA_WINDOW - 1) // tq)
    row = lax.broadcasted_iota(jnp.int32, (rows, tq), 0) & (tq - 1)
    col = lax.broadcasted_iota(jnp.int32, (rows, tq), 1)
    ones = jnp.ones((tq, LANES), BF16)
    for h in range(NSA_KV_HEADS):
        for g in range(GROUP):
            qs_ref[h, g * tq:(g + 1) * tq, :] = q_ref[:, h * gw + g * HEAD_DIM:h * gw + (g + 1) * HEAD_DIM]
        q = qs_ref[h]
        cols = slice(h * HEAD_DIM, (h + 1) * HEAD_DIM)
        scores, starts = [], []
        for b in range(n_prev + 1):
            kb = qi - b
            start = pl.multiple_of(jnp.maximum(kb, 0) * tq, tq)
            s = _dot_t(q, k_ref[pl.ds(start, tq), cols])
            dist = row - col + b * tq
            if b * tq - (tq - 1) < 0:
                s = jnp.where(dist >= 0, s, NEG_INF)
            if b * tq + (tq - 1) >= NSA_WINDOW:
                s = jnp.where(dist < NSA_WINDOW, s, NEG_INF)
            if b > 0:
                s = s + jnp.where(kb >= 0, 0.0, NEG_INF)
            scores.append(s)
            starts.append(start)
        m = jnp.max(functools.reduce(jnp.maximum, scores), axis=1, keepdims=True)
        pv = None
        for s, start in zip(scores, starts):
            va = jnp.concatenate([v_ref[pl.ds(start, tq), cols], ones], axis=1)
            term = _dot(jnp.exp(s - m).astype(BF16), va)
            pv = term if pv is None else pv + term
        o_win = pv[:, :HEAD_DIM] / pv[:, HEAD_DIM:]
        sig = jax.nn.sigmoid(gate_ref[h])
        for g in range(GROUP):
            oc = slice(h * gw + g * HEAD_DIM, h * gw + (g + 1) * HEAD_DIM)
            o = (sig[:, 3 * g:3 * g + 1] * ocmp_ref[:, oc]
                 + sig[:, 3 * g + 1:3 * g + 2] * oslc_ref[:, oc]
                 + sig[:, 3 * g + 2:3 * g + 3] * o_win[g * tq:(g + 1) * tq, :])
            o_ref[:, oc] = o.astype(o_ref.dtype)


def _nsa_window_merge(proj, gates, o_cmp, o_slc, q_cb, k_cb, v_cb):
    s = proj.shape[0]
    tq = BAND_BLOCK
    nq = NSA_KV_HEADS * GROUP
    kvw = NSA_KV_HEADS * HEAD_DIM
    assert tq & (tq - 1) == 0 and q_cb % nq == 0 and k_cb % NSA_KV_HEADS == v_cb % NSA_KV_HEADS == 0
    wide = pl.BlockSpec((tq, nq * HEAD_DIM), lambda i: (i, 0))
    return pl.pallas_call(
        _nsa_win_kernel,
        grid=(s // tq,),
        in_specs=[
            pl.BlockSpec((tq, nq * HEAD_DIM), lambda i: (i, q_cb // nq)),
            _resident((s, kvw), lambda i: (0, k_cb // NSA_KV_HEADS)),
            _resident((s, kvw), lambda i: (0, v_cb // NSA_KV_HEADS)),
            pl.BlockSpec((NSA_KV_HEADS, tq, LANES), lambda i: (0, i, 0)),
            wide,
            wide,
        ],
        out_specs=wide,
        out_shape=jax.ShapeDtypeStruct((s, nq * HEAD_DIM), BF16),
        scratch_shapes=[pltpu.VMEM((NSA_KV_HEADS, GROUP * tq, HEAD_DIM), BF16)],
        compiler_params=_cparams("arbitrary"),
        name="nsa_window_merge",
    )(proj, proj, proj, gates, o_cmp, o_slc)


def _overlap_matrix(seq):
    n_pad = seq // NSA_CMP_STRIDE
    n_cmp = (seq - NSA_CMP_LEN) // NSA_CMP_STRIDE + 1
    c_start = np.arange(n_pad)[:, None] * NSA_CMP_STRIDE
    s_start = np.arange(LANES)[None, :] * NSA_SLC_BLOCK
    ov = (c_start < s_start + NSA_SLC_BLOCK) & (c_start + NSA_CMP_LEN > s_start)
    ov &= (np.arange(n_pad)[:, None] < n_cmp) & (np.arange(LANES)[None, :] < seq // NSA_SLC_BLOCK)
    return jnp.asarray(ov.T.astype(np.float32), dtype=BF16)


def _ctypes(spec):
    return jnp.asarray(np.concatenate([np.full(n, kind, np.int32) for n, kind in spec]))


def _even_layer(x, mods, norm_g, w_in, sinks, w_out, tables):
    sh1, sc1, g1 = mods
    heads = w_in.shape[1] // HEAD_DIM
    n_moba = 8
    ctypes = _ctypes([(n_moba, CT_ROPE_SCALE), (n_moba, CT_ROPE), (n_moba, CT_PLAIN),
                      (8, CT_ROPE_SCALE), (SWA_KV_HEADS, CT_ROPE), (SWA_KV_HEADS, CT_PLAIN)])
    assert ctypes.shape[0] == heads
    proj = _norm_proj(x, norm_g, sc1, sh1, w_in.astype(BF16), ctypes, tables, BF16, 512)
    nb = x.shape[0] // MOBA_BLOCK
    kmean = _moba_kmean(proj, 1, n_moba * HEAD_DIM)
    kmean = jnp.pad(kmean, ((0, LANES - nb), (0, 0))).astype(BF16)
    oa = _moba_attention(proj, kmean, n_moba, 0, n_moba, 2 * n_moba)
    ob = _swa_attention(proj, sinks, 3 * n_moba, 4 * n_moba, 4 * n_moba + SWA_KV_HEADS)
    return _out_proj(oa, ob, w_out.astype(BF16), x, g1)


def _odd_layer(x, mods, norm_g, w_in, forget_b, k_pos, k_w1, k_w2, v_pos, v_w1, v_w2, w_out, tables, overlap):
    sh1, sc1, g1 = mods
    s = x.shape[0]
    n_fox = 8
    hw = n_fox * HEAD_DIM
    kvw = NSA_KV_HEADS * HEAD_DIM
    o_fc = 3 * hw
    o_qd = o_fc + n_fox
    o_gd = o_qd + hw + 6 * kvw
    w_main = jnp.concatenate([w_in[:, :o_fc], w_in[:, o_qd:o_gd]], axis=1).astype(BF16)
    n_aux = n_fox + 3 * 8
    w_aux = jnp.concatenate([w_in[:, o_fc:o_qd], w_in[:, o_gd:], jnp.zeros((w_in.shape[0], LANES - n_aux), F32)],
                            axis=1).astype(BF16)
    ctypes = _ctypes([(8, CT_SCALE), (8, CT_PLAIN), (8, CT_PLAIN), (8, CT_ROPE_SCALE),
                      (2, CT_ROPE), (2, CT_PLAIN), (2, CT_ROPE), (2, CT_PLAIN), (2, CT_ROPE), (2, CT_PLAIN)])
    proj = _norm_proj(x, norm_g, sc1, sh1, w_main, ctypes, tables, BF16, 512)
    aux = _norm_proj(x, norm_g, sc1, sh1, w_aux, _ctypes([(1, CT_PLAIN)]), tables, F32, LANES)

    bias_row = jnp.pad(forget_b.astype(F32), (0, LANES - n_fox)).reshape(1, LANES)
    fox_qx, fox_kx = _fox_cum(aux, bias_row, n_fox)
    oc = _fox_attention(proj, fox_qx, fox_kx, n_fox, 0, 8, 16)

    c0 = 32
    cmp_in = proj[:, c0 * HEAD_DIM:(c0 + 4) * HEAD_DIM].reshape(s, 4, HEAD_DIM).transpose(1, 0, 2)
    cmp_in = cmp_in.reshape(4, s // NSA_CMP_STRIDE, NSA_CMP_STRIDE * HEAD_DIM)
    kv_c = _nsa_compress(
        cmp_in,
        jnp.stack([k_w1, v_w1]).astype(BF16),
        jnp.stack([k_pos.reshape(1, -1), v_pos.reshape(1, -1)]).astype(BF16),
        jnp.stack([k_w2, v_w2]).astype(BF16))
    o_cmp, sel = _nsa_cmp_select(proj, kv_c, overlap, 24)
    o_slc = _nsa_selected(proj, sel, 24, c0 + 4, c0 + 6)
    gates = aux[:, n_fox:n_aux].reshape(s, NSA_KV_HEADS, 3 * GROUP).transpose(1, 0, 2)
    gates = jnp.pad(gates, ((0, 0), (0, 0), (0, LANES - 3 * GROUP)))
    od = _nsa_window_merge(proj, gates, o_cmp, o_slc, 24, c0 + 8, c0 + 10)
    return _out_proj(oc, od, w_out.astype(BF16), x, g1)


def kernel(x, c, norm_mix_g, norm_mlp_g, ada_w, ada_b, mlp_up, mlp_down, even_w_in, even_sinks, even_w_out, odd_w_in, fox_forget_b, nsa_k_pos, nsa_k_w1, nsa_k_w2, nsa_v_pos, nsa_v_w1, nsa_v_w2, odd_w_out, final_norm_g):
    batch, seq, d = x.shape
    assert batch == 1
    depth = ada_w.shape[0]
    tables = _epilogue_tables(seq)
    overlap = _overlap_matrix(seq)
    mod = _ada_mod(c, ada_w, ada_b).reshape(depth, 6, 1, d)
    xs = x[0]
    for i in range(depth):
        sh1, sc1, g1, sh2, sc2, g2 = [mod[i, t] for t in range(6)]
        ng = norm_mix_g[i].reshape(1, d)
        j = i // 2
        if i % 2 == 0:
            xs = _even_layer(xs, (sh1, sc1, g1), ng, even_w_in[j], even_sinks[j], even_w_out[j], tables)
        else:
            xs = _odd_layer(xs, (sh1, sc1, g1), ng, odd_w_in[j], fox_forget_b[j], nsa_k_pos[j], nsa_k_w1[j],
                            nsa_k_w2[j], nsa_v_pos[j], nsa_v_w1[j], nsa_v_w2[j], odd_w_out[j], tables, overlap)
        xs = _mlp(xs, norm_mlp_g[i].reshape(1, d), sc2, sh2, mlp_up[i].astype(BF16), mlp_down[i].astype(BF16), g2)
    return _final_norm(xs, final_norm_g.reshape(1, d))[None]
```

```python
import functools
import math

import numpy as np
import jax
import jax.numpy as jnp
from jax import lax
from jax.experimental import pallas as pl
from jax.experimental.pallas import tpu as pltpu

HEAD_DIM = 128
ROPE_THETA = 10000.0
NORM_EPS = 1e-6
MOBA_BLOCK = 256
MOBA_TOPK = 3
SWA_KV_HEADS = 2
SWA_WINDOW = 128
NSA_KV_HEADS = 2
NSA_CMP_LEN = 32
NSA_CMP_STRIDE = 16
NSA_SLC_BLOCK = 64
SLC_SHIFT = 6
NSA_SLC_TOPK = 16
NSA_WINDOW = 512
BAND_BLOCK = 128
NEG_INF = -1e30
SCALE = HEAD_DIM ** -0.5
LANES = 128
GROUP = 4
VMEM_LIMIT = 48 * 1024 * 1024

F32 = jnp.float32
BF16 = jnp.bfloat16

CT_PLAIN, CT_ROPE, CT_ROPE_SCALE, CT_SCALE = 0, 1, 2, 3


def _cparams(*sem):
    return pltpu.CompilerParams(dimension_semantics=sem, vmem_limit_bytes=VMEM_LIMIT)


def _dot(a, b):
    return jnp.dot(a, b, preferred_element_type=F32)


def _dot_t(a, b):
    return lax.dot_general(a, b, (((1,), (1,)), ((), ())), preferred_element_type=F32)


def _ada_kernel(c_ref, w_ref, b_ref, o_ref):
    c = c_ref[...]
    cond = c * jax.nn.sigmoid(c)
    cond8 = jnp.broadcast_to(cond, (8, cond.shape[1])).astype(BF16)
    y = _dot(cond8, w_ref[...].astype(BF16))
    o_ref[...] = y[0:1] + b_ref[...]


def _ada_mod(c, ada_w, ada_b):
    depth, d, n = ada_w.shape
    tn = 1024
    return pl.pallas_call(
        _ada_kernel,
        grid=(depth, n // tn),
        in_specs=[
            pl.BlockSpec((1, d), lambda l, j: (0, 0)),
            pl.BlockSpec((None, d, tn), lambda l, j: (l, 0, j)),
            pl.BlockSpec((None, 1, tn), lambda l, j: (l, 0, j)),
        ],
        out_specs=pl.BlockSpec((None, 1, tn), lambda l, j: (l, 0, j)),
        out_shape=jax.ShapeDtypeStruct((depth, 1, n), F32),
        compiler_params=_cparams("arbitrary", "arbitrary"),
        name="ada_mod",
    )(c, ada_w, ada_b.reshape(depth, 1, n))


def _norm_mod(x, g, sc, sh):
    y = x * lax.rsqrt(jnp.mean(x * x, axis=-1, keepdims=True) + NORM_EPS)
    return (y * g) * (1.0 + sc) + sh


def _proj_kernel(ct_ref, x_ref, g_ref, sc_ref, sh_ref, w_ref, ta_ref, tb_ref, o_ref, h_ref):
    j = pl.program_id(1)

    @pl.when(j == 0)
    def _():
        h_ref[...] = _norm_mod(x_ref[...], g_ref[...], sc_ref[...], sh_ref[...]).astype(BF16)

    y = _dot(h_ref[...], w_ref[...])
    groups = y.shape[1] // LANES
    for gi in range(groups):
        lo, hi = gi * LANES, (gi + 1) * LANES
        yg = y[:, lo:hi]
        ct = ct_ref[j * groups + gi]
        r = yg * ta_ref[ct] + pltpu.roll(yg, HEAD_DIM // 2, 1) * tb_ref[ct]
        o_ref[:, lo:hi] = r.astype(o_ref.dtype)


def _epilogue_tables(seq):
    f32 = np.float32
    inv = (f32(1.0) / (f32(ROPE_THETA) ** (np.arange(0, HEAD_DIM, 2, dtype=f32) / f32(HEAD_DIM)))).astype(f32)
    ang = np.arange(seq, dtype=f32)[:, None] * inv[None, :]
    cos, sin = np.cos(ang), np.sin(ang)
    cosf = np.concatenate([cos, cos], axis=-1)
    sinf = np.concatenate([-sin, sin], axis=-1)
    one, zero = np.ones_like(cosf), np.zeros_like(cosf)
    by_kind = {CT_PLAIN: (one, zero), CT_ROPE: (cosf, sinf), CT_ROPE_SCALE: (cosf * f32(SCALE), sinf * f32(SCALE)),
               CT_SCALE: (one * f32(SCALE), zero)}
    return (jnp.asarray(np.stack([by_kind[k][0] for k in range(4)])),
            jnp.asarray(np.stack([by_kind[k][1] for k in range(4)])))


def _norm_proj(x, g, sc, sh, w, ctypes, tables, out_dtype, tn):
    s, d = x.shape
    n = w.shape[1]
    tm = min(1024, s)
    kinds = tables[0].shape[0]
    grid_spec = pltpu.PrefetchScalarGridSpec(
        num_scalar_prefetch=1,
        grid=(s // tm, n // tn),
        in_specs=[
            pl.BlockSpec((tm, d), lambda i, j, ct: (i, 0)),
            pl.BlockSpec((1, d), lambda i, j, ct: (0, 0)),
            pl.BlockSpec((1, d), lambda i, j, ct: (0, 0)),
            pl.BlockSpec((1, d), lambda i, j, ct: (0, 0)),
            pl.BlockSpec((d, tn), lambda i, j, ct: (0, j)),
            pl.BlockSpec((kinds, tm, LANES), lambda i, j, ct: (0, i, 0)),
            pl.BlockSpec((kinds, tm, LANES), lambda i, j, ct: (0, i, 0)),
        ],
        out_specs=pl.BlockSpec((tm, tn), lambda i, j, ct: (i, j)),
        scratch_shapes=[pltpu.VMEM((tm, d), BF16)],
    )
    return pl.pallas_call(
        _proj_kernel,
        grid_spec=grid_spec,
        out_shape=jax.ShapeDtypeStruct((s, n), out_dtype),
        compiler_params=_cparams("arbitrary", "arbitrary"),
        name="norm_proj",
    )(ctypes, x, g, sc, sh, w, *tables)


def _outproj_kernel(a_ref, b_ref, w_ref, x_ref, g_ref, o_ref):
    ha = a_ref.shape[1]
    y = _dot(a_ref[...], w_ref[0:ha, :]) + _dot(b_ref[...], w_ref[ha:, :])
    o_ref[...] = x_ref[...] + g_ref[...] * y


def _out_proj(oa, ob, w, x, gate):
    s, d = x.shape
    ha, hb = oa.shape[1], ob.shape[1]
    tm, tn = min(1024, s), 512
    return pl.pallas_call(
        _outproj_kernel,
        grid=(s // tm, d // tn),
        in_specs=[
            pl.BlockSpec((tm, ha), lambda i, j: (i, 0)),
            pl.BlockSpec((tm, hb), lambda i, j: (i, 0)),
            pl.BlockSpec((ha + hb, tn), lambda i, j: (0, j)),
            pl.BlockSpec((tm, tn), lambda i, j: (i, j)),
            pl.BlockSpec((1, tn), lambda i, j: (0, j)),
        ],
        out_specs=pl.BlockSpec((tm, tn), lambda i, j: (i, j)),
        out_shape=jax.ShapeDtypeStruct((s, d), F32),
        compiler_params=_cparams("arbitrary", "arbitrary"),
        name="out_proj",
    )(oa, ob, w, x, gate)


def _mlp_kernel(x_ref, g_ref, sc_ref, sh_ref, up_ref, down_ref, gate_ref, o_ref, h_ref, acc_ref):
    f = pl.program_id(1)

    @pl.when(f == 0)
    def _():
        h_ref[...] = _norm_mod(x_ref[...], g_ref[...], sc_ref[...], sh_ref[...]).astype(BF16)
        acc_ref[...] = jnp.zeros_like(acc_ref)

    hid = jnp.maximum(_dot(h_ref[...], up_ref[...]), 0.0)
    acc_ref[...] += _dot((hid * hid).astype(BF16), down_ref[...])

    @pl.when(f == pl.num_programs(1) - 1)
    def _():
        o_ref[...] = x_ref[...] + gate_ref[...] * acc_ref[...]


def _mlp(x, g, sc, sh, up, down, gate):
    s, d = x.shape
    ff = up.shape[1]
    tm, tf = min(512, s), 512
    return pl.pallas_call(
        _mlp_kernel,
        grid=(s // tm, ff // tf),
        in_specs=[
            pl.BlockSpec((tm, d), lambda i, f: (i, 0)),
            pl.BlockSpec((1, d), lambda i, f: (0, 0)),
            pl.BlockSpec((1, d), lambda i, f: (0, 0)),
            pl.BlockSpec((1, d), lambda i, f: (0, 0)),
            pl.BlockSpec((d, tf), lambda i, f: (0, f)),
            pl.BlockSpec((tf, d), lambda i, f: (f, 0)),
            pl.BlockSpec((1, d), lambda i, f: (0, 0)),
        ],
        out_specs=pl.BlockSpec((tm, d), lambda i, f: (i, 0)),
        out_shape=jax.ShapeDtypeStruct((s, d), F32),
        scratch_shapes=[pltpu.VMEM((tm, d), BF16), pltpu.VMEM((tm, d), F32)],
        compiler_params=_cparams("arbitrary", "arbitrary"),
        name="mlp",
    )(x, g, sc, sh, up, down, gate)


def _final_norm_kernel(x_ref, g_ref, o_ref):
    x = x_ref[...]
    o_ref[...] = x * lax.rsqrt(jnp.mean(x * x, axis=-1, keepdims=True) + NORM_EPS) * g_ref[...]


def _final_norm(x, g):
    s, d = x.shape
    tm = min(512, s)
    return pl.pallas_call(
        _final_norm_kernel,
        grid=(s // tm,),
        in_specs=[pl.BlockSpec((tm, d), lambda i: (i, 0)), pl.BlockSpec((1, d), lambda i: (0, 0))],
        out_specs=pl.BlockSpec((tm, d), lambda i: (i, 0)),
        out_shape=jax.ShapeDtypeStruct((s, d), F32),
        compiler_params=_cparams("arbitrary"),
        name="final_norm",
    )(x, g)


def _rows(ref, start, size):
    return ref[pl.ds(pl.multiple_of(start, size), size), :]


def _log2(n):
    assert n > 0 and n & (n - 1) == 0
    return n.bit_length() - 1


FLASH_TILE = 512
HEADS_PER_STEP = 4


def _resident(shape, index_map):
    return pl.BlockSpec(shape, index_map, pipeline_mode=pl.Buffered(1))


def _flash_attention(qas, k_ref, kx_ref, v_ref, state, last, tk, diag_mask):
    m_ref, alpha_ref, p_ref, acc_ref = state
    chains = range(len(qas))
    ones = jnp.ones((tk, LANES), BF16)

    def score(start, mask, first):
        rows = pl.ds(pl.multiple_of(start, tk), tk)
        for c in chains:
            ka = jnp.concatenate([k_ref[rows, c * HEAD_DIM:(c + 1) * HEAD_DIM], kx_ref[rows, :]], axis=1)
            s = _dot_t(qas[c], ka)
            if mask is not None:
                s = jnp.where(mask, s, NEG_INF)
            chunks = [s[:, i * LANES:(i + 1) * LANES] for i in range(tk // LANES)]
            m_cur = jnp.max(functools.reduce(jnp.maximum, chunks), axis=1, keepdims=True)
            if first:
                m_new = jnp.broadcast_to(m_cur, (s.shape[0], LANES))
            else:
                m_prev = m_ref[c]
                m_new = jnp.maximum(m_prev, m_cur)
                alpha_ref[c] = jnp.exp(m_prev - m_new)
            p_ref[c] = jnp.concatenate([jnp.exp(ch - m_new) for ch in chunks], axis=1).astype(BF16)
            m_ref[c] = m_new

    def accumulate(start):
        rows = pl.ds(pl.multiple_of(start, tk), tk)
        for c in chains:
            va = jnp.concatenate([v_ref[rows, c * HEAD_DIM:(c + 1) * HEAD_DIM], ones], axis=1)
            alpha = alpha_ref[c]
            acc_ref[c] = jnp.concatenate([alpha, alpha], axis=1) * acc_ref[c] + _dot(p_ref[c], va)

    for c in chains:
        acc_ref[c] = jnp.zeros(acc_ref.shape[1:], F32)
        alpha_ref[c] = jnp.ones(alpha_ref.shape[1:], F32)
    score(last * tk, diag_mask, True)

    def body(p, carry):
        accumulate(jnp.where(p == 0, last, p - 1) * tk)
        score(p * tk, None, False)
        return carry

    lax.fori_loop(0, last, body, 0)
    accumulate(jnp.where(last == 0, last, last - 1) * tk)
    return [acc_ref[c][:, :HEAD_DIM] / acc_ref[c][:, HEAD_DIM:] for c in chains]


def _flash_state(chains, rows, tk):
    return [pltpu.VMEM((chains, rows, LANES), F32), pltpu.VMEM((chains, rows, LANES), F32),
            pltpu.VMEM((chains, rows, tk), BF16), pltpu.VMEM((chains, rows, 2 * HEAD_DIM), F32)]


def _block_onehot(seq, block):
    return jnp.asarray((np.arange(seq)[:, None] // block == np.arange(LANES)[None, :]).astype(np.float32), dtype=BF16)


def _kmean_kernel(k_ref, o_ref):
    k = k_ref[...].astype(F32)
    o_ref[...] = jnp.sum(k, axis=0, keepdims=True) * (1.0 / k.shape[0])


def _moba_kmean(proj, col_block, width):
    s = proj.shape[0]
    nb = s // MOBA_BLOCK
    out = pl.pallas_call(
        _kmean_kernel,
        grid=(nb,),
        in_specs=[pl.BlockSpec((MOBA_BLOCK, width), lambda b: (b, col_block))],
        out_specs=pl.BlockSpec((None, 1, width), lambda b: (b, 0, 0)),
        out_shape=jax.ShapeDtypeStruct((nb, 1, width), F32),
        compiler_params=_cparams("arbitrary"),
        name="moba_kmean",
    )(proj)
    return out.reshape(nb, width)


def _moba_kernel(q_ref, k_ref, kx_ref, v_ref, km_ref, o_ref, *state, tk, hp):
    qi = pl.program_id(1)
    tq = q_ref.shape[0]
    blk = lax.broadcasted_iota(jnp.int32, (tq, LANES), 1)
    blkf = blk.astype(F32)
    own = jnp.right_shift(qi * tq + lax.broadcasted_iota(jnp.int32, (tq, LANES), 0), _log2(MOBA_BLOCK))
    ownf = own.astype(F32)

    qas = []
    for hh in range(hp):
        cols = slice(hh * HEAD_DIM, (hh + 1) * HEAD_DIM)
        q = q_ref[:, cols]
        g = jnp.where(blk < own, _dot_t(q, km_ref[:, cols]), NEG_INF)
        sel = jnp.zeros((tq, LANES), F32)
        for _ in range(MOBA_TOPK):
            mx = jnp.max(g, axis=1, keepdims=True)
            idx = jnp.min(jnp.where(g == mx, blkf, float(LANES)), axis=1, keepdims=True)
            pick = blkf == idx
            sel = jnp.where(pick & (idx < ownf), 1.0, sel)
            g = jnp.where(pick, -jnp.inf, g)
        qx = jnp.where((sel > 0.5) | (blk == own), 0.0, NEG_INF).astype(BF16)
        qas.append(jnp.concatenate([q, qx], axis=1))

    last = jnp.right_shift(qi, _log2(tk // tq))
    t = qi * tq + lax.broadcasted_iota(jnp.int32, (tq, tk), 0)
    kpos = last * tk + lax.broadcasted_iota(jnp.int32, (tq, tk), 1)
    outs = _flash_attention(qas, k_ref, kx_ref, v_ref, state, last, tk, kpos <= t)
    for hh in range(hp):
        o_ref[:, hh * HEAD_DIM:(hh + 1) * HEAD_DIM] = outs[hh].astype(o_ref.dtype)


def _moba_attention(proj, kmean, n_heads, q_cb, k_cb, v_cb):
    s = proj.shape[0]
    tq = tk = min(FLASH_TILE, s)
    hp = HEADS_PER_STEP
    hw = hp * HEAD_DIM
    assert s % tk == 0 and tk % MOBA_BLOCK == 0 and n_heads % hp == 0 and q_cb % hp == k_cb % hp == v_cb % hp == 0
    return pl.pallas_call(
        functools.partial(_moba_kernel, tk=tk, hp=hp),
        grid=(n_heads // hp, s // tq),
        in_specs=[
            pl.BlockSpec((tq, hw), lambda h, i: (i, q_cb // hp + h)),
            _resident((s, hw), lambda h, i: (0, k_cb // hp + h)),
            _resident((s, LANES), lambda h, i: (0, 0)),
            _resident((s, hw), lambda h, i: (0, v_cb // hp + h)),
            pl.BlockSpec((LANES, hw), lambda h, i: (0, h)),
        ],
        out_specs=pl.BlockSpec((tq, hw), lambda h, i: (i, h)),
        out_shape=jax.ShapeDtypeStruct((s, n_heads * HEAD_DIM), BF16),
        scratch_shapes=_flash_state(hp, tq, tk),
        compiler_params=_cparams("arbitrary", "arbitrary"),
        name="moba_attn",
    )(proj, proj, _block_onehot(s, MOBA_BLOCK), proj, kmean)


def _swa_kernel(sink_ref, q_ref, k_ref, v_ref, o_ref, qs_ref):
    qi = pl.program_id(0)
    tq = q_ref.shape[0]
    tk = 2 * tq
    rows = GROUP * tq
    gw = GROUP * HEAD_DIM
    start = pl.multiple_of(jnp.maximum(qi - 1, 0) * tq, tq)
    t = qi * tq + (lax.broadcasted_iota(jnp.int32, (rows, tk), 0) & (tq - 1))
    kpos = start + lax.broadcasted_iota(jnp.int32, (rows, tk), 1)
    ok = (kpos <= t) & (kpos > t - SWA_WINDOW)
    ones = jnp.ones((tk, LANES), BF16)
    for h in range(SWA_KV_HEADS):
        for g in range(GROUP):
            qs_ref[h, g * tq:(g + 1) * tq, :] = q_ref[:, h * gw + g * HEAD_DIM:h * gw + (g + 1) * HEAD_DIM]
        cols = slice(h * HEAD_DIM, (h + 1) * HEAD_DIM)
        s = jnp.where(ok, _dot_t(qs_ref[h], k_ref[pl.ds(start, tk), cols]), NEG_INF)
        sink = jnp.concatenate([jnp.full((tq, 1), sink_ref[h * GROUP + g], F32) for g in range(GROUP)], axis=0)
        m = jnp.maximum(jnp.max(jnp.maximum(s[:, :tq], s[:, tq:]), axis=1, keepdims=True), sink)
        p = jnp.exp(s - m)
        pv = _dot(p.astype(BF16), jnp.concatenate([v_ref[pl.ds(start, tk), cols], ones], axis=1))
        o = pv[:, :HEAD_DIM] / (pv[:, HEAD_DIM:] + jnp.exp(sink - m))
        for g in range(GROUP):
            o_ref[:, h * gw + g * HEAD_DIM:h * gw + (g + 1) * HEAD_DIM] = o[g * tq:(g + 1) * tq, :].astype(o_ref.dtype)


def _swa_attention(proj, sinks, q_cb, k_cb, v_cb):
    s = proj.shape[0]
    tq = BAND_BLOCK
    nq = SWA_KV_HEADS * GROUP
    kvw = SWA_KV_HEADS * HEAD_DIM
    assert SWA_WINDOW <= tq and tq == LANES and q_cb % nq == 0 and k_cb % SWA_KV_HEADS == v_cb % SWA_KV_HEADS == 0
    return pl.pallas_call(
        _swa_kernel,
        grid=(s // tq,),
        in_specs=[
            pl.BlockSpec(memory_space=pltpu.SMEM),
            pl.BlockSpec((tq, nq * HEAD_DIM), lambda i: (i, q_cb // nq)),
            _resident((s, kvw), lambda i: (0, k_cb // SWA_KV_HEADS)),
            _resident((s, kvw), lambda i: (0, v_cb // SWA_KV_HEADS)),
        ],
        out_specs=pl.BlockSpec((tq, nq * HEAD_DIM), lambda i: (i, 0)),
        out_shape=jax.ShapeDtypeStruct((s, nq * HEAD_DIM), BF16),
        scratch_shapes=[pltpu.VMEM((SWA_KV_HEADS, GROUP * tq, HEAD_DIM), BF16)],
        compiler_params=_cparams("arbitrary"),
        name="swa_attn",
    )(sinks, proj, proj, proj)


FOX_LANES = 16


def _split3(x):
    x1 = x.astype(BF16)
    r1 = x - x1.astype(F32)
    x2 = r1.astype(BF16)
    return x1, x2, (r1 - x2.astype(F32)).astype(BF16)


def _fox_tables(n_heads):
    place = np.zeros((6, LANES, LANES), np.float32)
    ones_q = np.zeros((1, LANES), np.float32)
    ones_k = np.zeros((1, LANES), np.float32)
    for h in range(n_heads):
        for i in range(6):
            place[i, h, FOX_LANES * h + i] = 1.0
        ones_q[0, FOX_LANES * h + 3:FOX_LANES * h + 6] = 1.0
        ones_k[0, FOX_LANES * h:FOX_LANES * h + 3] = 1.0
    return jnp.asarray(place, dtype=BF16), jnp.asarray(ones_q), jnp.asarray(ones_k)


def _fox_cum_kernel(aux_ref, b_ref, place_ref, oq_ref, ok_ref, qx_ref, kx_ref, carry_ref):
    i = pl.program_id(0)

    @pl.when(i == 0)
    def _():
        carry_ref[...] = jnp.zeros_like(carry_ref)

    x = aux_ref[...] + b_ref[...]
    log_f = -(jnp.maximum(-x, 0.0) + jnp.log(1.0 + jnp.exp(-jnp.abs(x))))
    tb = x.shape[0]
    r = lax.broadcasted_iota(jnp.int32, (tb, tb), 0)
    c = lax.broadcasted_iota(jnp.int32, (tb, tb), 1)
    tri = jnp.where(c <= r, 1.0, 0.0).astype(BF16)
    x1, x2, x3 = _split3(log_f)
    cum = _dot(tri, x1) + _dot(tri, x2) + _dot(tri, x3) + carry_ref[...]
    carry_ref[...] = cum[tb - 1:tb, :]
    c1, c2, c3 = _split3(cum)
    qx = _dot(c1, place_ref[0]) + _dot(c2, place_ref[1]) + _dot(c3, place_ref[2]) + oq_ref[...]
    kx = ok_ref[...] - (_dot(c1, place_ref[3]) + _dot(c2, place_ref[4]) + _dot(c3, place_ref[5]))
    qx_ref[...] = qx.astype(BF16)
    kx_ref[...] = kx.astype(BF16)


def _fox_cum(aux, bias_row, n_heads):
    s, w = aux.shape
    tb = min(512, s)
    place, ones_q, ones_k = _fox_tables(n_heads)
    full = lambda shape: pl.BlockSpec(shape, lambda i: (0,) * len(shape))
    return pl.pallas_call(
        _fox_cum_kernel,
        grid=(s // tb,),
        in_specs=[pl.BlockSpec((tb, w), lambda i: (i, 0)), full((1, w)), full(place.shape), full((1, w)), full((1, w))],
        out_specs=[pl.BlockSpec((tb, w), lambda i: (i, 0)), pl.BlockSpec((tb, w), lambda i: (i, 0))],
        out_shape=[jax.ShapeDtypeStruct((s, w), BF16), jax.ShapeDtypeStruct((s, w), BF16)],
        scratch_shapes=[pltpu.VMEM((1, w), F32)],
        compiler_params=_cparams("arbitrary"),
        name="fox_cum",
    )(aux, bias_row, place, ones_q, ones_k)


def _fox_kernel(q_ref, qx_ref, k_ref, kx_ref, v_ref, o_ref, *state, tk, hp):
    hb = pl.program_id(0)
    qi = pl.program_id(1)
    tq = q_ref.shape[0]
    lane = lax.broadcasted_iota(jnp.int32, (tq, LANES), 1)
    qxf = qx_ref[...].astype(F32)
    qas = []
    for hh in range(hp):
        mine = jnp.right_shift(lane, 4) == hb * hp + hh
        qx = jnp.where(mine, qxf, 0.0).astype(BF16)
        qas.append(jnp.concatenate([q_ref[:, hh * HEAD_DIM:(hh + 1) * HEAD_DIM], qx], axis=1))

    last = jnp.right_shift(qi, _log2(tk // tq))
    t = qi * tq + lax.broadcasted_iota(jnp.int32, (tq, tk), 0)
    kpos = last * tk + lax.broadcasted_iota(jnp.int32, (tq, tk), 1)
    outs = _flash_attention(qas, k_ref, kx_ref, v_ref, state, last, tk, kpos <= t)
    for hh in range(hp):
        o_ref[:, hh * HEAD_DIM:(hh + 1) * HEAD_DIM] = outs[hh].astype(o_ref.dtype)


def _fox_attention(proj, qx, kx, n_heads, q_cb, k_cb, v_cb):
    s = proj.shape[0]
    tq = tk = min(FLASH_TILE, s)
    hp = HEADS_PER_STEP
    hw = hp * HEAD_DIM
    assert s % tk == 0 and FOX_LANES == 16 and n_heads % hp == 0 and q_cb % hp == k_cb % hp == v_cb % hp == 0
    return pl.pallas_call(
        functools.partial(_fox_kernel, tk=tk, hp=hp),
        grid=(n_heads // hp, s // tq),
        in_specs=[
            pl.BlockSpec((tq, hw), lambda h, i: (i, q_cb // hp + h)),
            pl.BlockSpec((tq, LANES), lambda h, i: (i, 0)),
            _resident((s, hw), lambda h, i: (0, k_cb // hp + h)),
            _resident((s, LANES), lambda h, i: (0, 0)),
            _resident((s, hw), lambda h, i: (0, v_cb // hp + h)),
        ],
        out_specs=pl.BlockSpec((tq, hw), lambda h, i: (i, h)),
        out_shape=jax.ShapeDtypeStruct((s, n_heads * HEAD_DIM), BF16),
        scratch_shapes=_flash_state(hp, tq, tk),
        compiler_params=_cparams("arbitrary", "arbitrary"),
        name="fox_attn",
    )(proj, qx, proj, kx, proj)


def _gelu_tanh(x):
    return 0.5 * x * (1.0 + jnp.tanh(math.sqrt(2.0 / math.pi) * (x + 0.044715 * (x * x * x))))


def _nsa_compress_kernel(a_ref, w1_ref, pos_ref, w2_ref, o_ref):
    a = a_ref[...]
    half = a.shape[1]
    n = a.shape[0]
    lo = _dot(a, w1_ref[0:half, :])
    hi = _dot(a, w1_ref[half:, :])
    pos8 = jnp.broadcast_to(pos_ref[...], (8, pos_ref.shape[1]))
    pterm = _dot(pos8, w1_ref[...])[0:1]
    pre = lo + pltpu.roll(hi, n - 1, 0) + pterm
    o_ref[...] = _dot(_gelu_tanh(pre).astype(BF16), w2_ref[...]).astype(o_ref.dtype)


def _nsa_compress(a, w1, pos, w2):
    n, rows, width = a.shape
    return pl.pallas_call(
        _nsa_compress_kernel,
        grid=(n,),
        in_specs=[
            pl.BlockSpec((None, rows, width), lambda i: (i, 0, 0)),
            pl.BlockSpec((None, 2 * width, HEAD_DIM), lambda i: (i // NSA_KV_HEADS, 0, 0)),
            pl.BlockSpec((None, 1, 2 * width), lambda i: (i // NSA_KV_HEADS, 0, 0)),
            pl.BlockSpec((None, HEAD_DIM, HEAD_DIM), lambda i: (i // NSA_KV_HEADS, 0, 0)),
        ],
        out_specs=pl.BlockSpec((None, rows, HEAD_DIM), lambda i: (i, 0, 0)),
        out_shape=jax.ShapeDtypeStruct((n, rows, HEAD_DIM), BF16),
        compiler_params=_cparams("arbitrary"),
        name="nsa_compress",
    )(a, w1, pos, w2)


def _nsa_cmp_kernel(q_ref, kc_ref, vc_ref, ovt_ref, ocmp_ref, sel_ref):
    qi = pl.program_id(0)
    tq = q_ref.shape[0]
    ncp = kc_ref.shape[1]
    gw = GROUP * HEAD_DIM
    t = qi * tq + lax.broadcasted_iota(jnp.int32, (tq, ncp), 0)
    n = lax.broadcasted_iota(jnp.int32, (tq, ncp), 1)
    vis = (n * NSA_CMP_STRIDE + (NSA_CMP_LEN - 1)) <= t
    ones = jnp.ones((ncp, LANES), BF16)
    blk = lax.broadcasted_iota(jnp.int32, (LANES, tq), 0)
    blkf = blk.astype(F32)
    cur = (qi * tq + lax.broadcasted_iota(jnp.int32, (LANES, tq), 1)) >> SLC_SHIFT
    forced = (blk == 0) | (blk == cur) | (blk == cur - 1)
    future = blk > cur
    ovt = ovt_ref[...]

    for h in range(NSA_KV_HEADS):
        kc = kc_ref[h]
        va = jnp.concatenate([vc_ref[h], ones], axis=1)
        psum = jnp.zeros((tq, ncp), F32)
        for g in range(GROUP):
            cols = slice(h * gw + g * HEAD_DIM, h * gw + (g + 1) * HEAD_DIM)
            s = jnp.where(vis, _dot_t(q_ref[:, cols], kc), NEG_INF)
            chunks = [s[:, c * LANES:(c + 1) * LANES] for c in range(ncp // LANES)]
            m = jnp.max(functools.reduce(jnp.maximum, chunks), axis=1, keepdims=True)
            p = jnp.where(vis, jnp.exp(s - m), 0.0)
            pv = _dot(p.astype(BF16), va)
            inv = 1.0 / jnp.maximum(pv[:, HEAD_DIM:], 1e-30)
            ocmp_ref[:, cols] = pv[:, :HEAD_DIM] * inv
            psum = psum + p * jnp.concatenate([inv] * (ncp // LANES), axis=1)

        p_hi = psum.astype(BF16)
        p_lo = (psum - p_hi.astype(F32)).astype(BF16)
        imp = _dot_t(ovt, p_hi) + _dot_t(ovt, p_lo)
        imp = jnp.where(forced, jnp.inf, imp)
        imp = jnp.where(future, -jnp.inf, imp)
        sel = jnp.zeros(imp.shape, F32)
        for _ in range(NSA_SLC_TOPK):
            mx = jnp.max(imp, axis=0, keepdims=True)
            idx = jnp.min(jnp.where(imp == mx, blkf, float(LANES)), axis=0, keepdims=True)
            pick = blkf == idx
            sel = jnp.where(pick, 1.0, sel)
            imp = jnp.where(pick, -jnp.inf, imp)
        sel_ref[h] = jnp.where(sel > 0.5, 0.0, NEG_INF).T.astype(sel_ref.dtype)


def _nsa_cmp_select(proj, kv_c, overlap_t, q_cb):
    s = proj.shape[0]
    tq = BAND_BLOCK
    nq = NSA_KV_HEADS * GROUP
    ncp = kv_c.shape[1]
    assert q_cb % nq == 0 and tq == LANES
    return pl.pallas_call(
        _nsa_cmp_kernel,
        grid=(s // tq,),
        in_specs=[
            pl.BlockSpec((tq, nq * HEAD_DIM), lambda i: (i, q_cb // nq)),
            pl.BlockSpec((NSA_KV_HEADS, ncp, HEAD_DIM), lambda i: (0, 0, 0)),
            pl.BlockSpec((NSA_KV_HEADS, ncp, HEAD_DIM), lambda i: (1, 0, 0)),
            pl.BlockSpec((LANES, ncp), lambda i: (0, 0)),
        ],
        out_specs=[
            pl.BlockSpec((tq, nq * HEAD_DIM), lambda i: (i, 0)),
            pl.BlockSpec((NSA_KV_HEADS, tq, LANES), lambda i: (0, i, 0)),
        ],
        out_shape=[
            jax.ShapeDtypeStruct((s, nq * HEAD_DIM), F32),
            jax.ShapeDtypeStruct((NSA_KV_HEADS, s, LANES), BF16),
        ],
        compiler_params=_cparams("arbitrary"),
        name="nsa_cmp_select",
    )(proj, kv_c, kv_c, overlap_t)


def _nsa_slc_kernel(q_ref, k_ref, kx_ref, v_ref, sel_ref, o_ref, qa_ref, *state, tk):
    qi = pl.program_id(0)
    tq = q_ref.shape[0]
    rows = GROUP * tq
    gw = GROUP * HEAD_DIM
    for h in range(NSA_KV_HEADS):
        selb = sel_ref[h]
        for g in range(GROUP):
            qa_ref[h, g * tq:(g + 1) * tq, 0:HEAD_DIM] = q_ref[:, h * gw + g * HEAD_DIM:h * gw + (g + 1) * HEAD_DIM]
            qa_ref[h, g * tq:(g + 1) * tq, HEAD_DIM:] = selb
    qas = [qa_ref[h] for h in range(NSA_KV_HEADS)]

    last = jnp.right_shift(qi, _log2(tk // tq))
    t = qi * tq + (lax.broadcasted_iota(jnp.int32, (rows, tk), 0) & (tq - 1))
    kpos = last * tk + lax.broadcasted_iota(jnp.int32, (rows, tk), 1)
    outs = _flash_attention(qas, k_ref, kx_ref, v_ref, state, last, tk, kpos <= t)
    for h in range(NSA_KV_HEADS):
        for g in range(GROUP):
            o_ref[:, h * gw + g * HEAD_DIM:h * gw + (g + 1) * HEAD_DIM] = outs[h][g * tq:(g + 1) * tq, :]


def _nsa_selected(proj, selb, q_cb, k_cb, v_cb):
    s = proj.shape[0]
    tq = 2 * BAND_BLOCK
    tk = min(FLASH_TILE, s)
    nq = NSA_KV_HEADS * GROUP
    kvw = NSA_KV_HEADS * HEAD_DIM
    assert tq & (tq - 1) == 0 and s % tk == 0 and q_cb % nq == 0 and k_cb % NSA_KV_HEADS == v_cb % NSA_KV_HEADS == 0
    return pl.pallas_call(
        functools.partial(_nsa_slc_kernel, tk=tk),
        grid=(s // tq,),
        in_specs=[
            pl.BlockSpec((tq, nq * HEAD_DIM), lambda i: (i, q_cb // nq)),
            _resident((s, kvw), lambda i: (0, k_cb // NSA_KV_HEADS)),
            _resident((s, LANES), lambda i: (0, 0)),
            _resident((s, kvw), lambda i: (0, v_cb // NSA_KV_HEADS)),
            pl.BlockSpec((NSA_KV_HEADS, tq, LANES), lambda i: (0, i, 0)),
        ],
        out_specs=pl.BlockSpec((tq, nq * HEAD_DIM), lambda i: (i, 0)),
        out_shape=jax.ShapeDtypeStruct((s, nq * HEAD_DIM), F32),
        scratch_shapes=[pltpu.VMEM((NSA_KV_HEADS, GROUP * tq, 2 * HEAD_DIM), BF16)]
        + _flash_state(NSA_KV_HEADS, GROUP * tq, tk),
        compiler_params=_cparams("arbitrary"),
        name="nsa_selected",
    )(proj, proj, _block_onehot(s, NSA_SLC_BLOCK), proj, selb)


def _nsa_win_kernel(q_ref, k_ref, v_ref, gate_ref, ocmp_ref, oslc_ref, o_ref, qs_ref):
    qi = pl.program_id(0)
    tq = q_ref.shape[0]
    gw = GROUP * HEAD_DIM
    rows = GROUP * tq
    n_prev = -(-(NSA_WINDOW - 1) // tq)
    row = lax.broadcasted_iota(jnp.int32, (rows, tq), 0) & (tq - 1)
    col = lax.broadcasted_iota(jnp.int32, (rows, tq), 1)
    ones = jnp.ones((tq, LANES), BF16)
    for h in range(NSA_KV_HEADS):
        for g in range(GROUP):
            qs_ref[h, g * tq:(g + 1) * tq, :] = q_ref[:, h * gw + g * HEAD_DIM:h * gw + (g + 1) * HEAD_DIM]
        q = qs_ref[h]
        cols = slice(h * HEAD_DIM, (h + 1) * HEAD_DIM)
        scores, starts = [], []
        for b in range(n_prev + 1):
            kb = qi - b
            start = pl.multiple_of(jnp.maximum(kb, 0) * tq, tq)
            s = _dot_t(q, k_ref[pl.ds(start, tq), cols])
            dist = row - col + b * tq
            if b * tq - (tq - 1) < 0:
                s = jnp.where(dist >= 0, s, NEG_INF)
            if b * tq + (tq - 1) >= NSA_WINDOW:
                s = jnp.where(dist < NSA_WINDOW, s, NEG_INF)
            if b > 0:
                s = s + jnp.where(kb >= 0, 0.0, NEG_INF)
            scores.append(s)
            starts.append(start)
        m = jnp.max(functools.reduce(jnp.maximum, scores), axis=1, keepdims=True)
        pv = None
        for s, start in zip(scores, starts):
            va = jnp.concatenate([v_ref[pl.ds(start, tq), cols], ones], axis=1)
            term = _dot(jnp.exp(s - m).astype(BF16), va)
            pv = term if pv is None else pv + term
        o_win = pv[:, :HEAD_DIM] / pv[:, HEAD_DIM:]
        sig = jax.nn.sigmoid(gate_ref[h])
        for g in range(GROUP):
            oc = slice(h * gw + g * HEAD_DIM, h * gw + (g + 1) * HEAD_DIM)
            o = (sig[:, 3 * g:3 * g + 1] * ocmp_ref[:, oc]
                 + sig[:, 3 * g + 1:3 * g + 2] * oslc_ref[:, oc]
                 + sig[:, 3 * g + 2:3 * g + 3] * o_win[g * tq:(g + 1) * tq, :])
            o_ref[:, oc] = o.astype(o_ref.dtype)


def _nsa_window_merge(proj, gates, o_cmp, o_slc, q_cb, k_cb, v_cb):
    s = proj.shape[0]
    tq = BAND_BLOCK
    nq = NSA_KV_HEADS * GROUP
    kvw = NSA_KV_HEADS * HEAD_DIM
    assert tq & (tq - 1) == 0 and q_cb % nq == 0 and k_cb % NSA_KV_HEADS == v_cb % NSA_KV_HEADS == 0
    wide = pl.BlockSpec((tq, nq * HEAD_DIM), lambda i: (i, 0))
    return pl.pallas_call(
        _nsa_win_kernel,
        grid=(s // tq,),
        in_specs=[
            pl.BlockSpec((tq, nq * HEAD_DIM), lambda i: (i, q_cb // nq)),
            _resident((s, kvw), lambda i: (0, k_cb // NSA_KV_HEADS)),
            _resident((s, kvw), lambda i: (0, v_cb // NSA_KV_HEADS)),
            pl.BlockSpec((NSA_KV_HEADS, tq, LANES), lambda i: (0, i, 0)),
            wide,
            wide,
        ],
        out_specs=wide,
        out_shape=jax.ShapeDtypeStruct((s, nq * HEAD_DIM), BF16),
        scratch_shapes=[pltpu.VMEM((NSA_KV_HEADS, GROUP * tq, HEAD_DIM), BF16)],
        compiler_params=_cparams("arbitrary"),
        name="nsa_window_merge",
    )(proj, proj, proj, gates, o_cmp, o_slc)


def _overlap_matrix(seq):
    n_pad = seq // NSA_CMP_STRIDE
    n_cmp = (seq - NSA_CMP_LEN) // NSA_CMP_STRIDE + 1
    c_start = np.arange(n_pad)[:, None] * NSA_CMP_STRIDE
    s_start = np.arange(LANES)[None, :] * NSA_SLC_BLOCK
    ov = (c_start < s_start + NSA_SLC_BLOCK) & (c_start + NSA_CMP_LEN > s_start)
    ov &= (np.arange(n_pad)[:, None] < n_cmp) & (np.arange(LANES)[None, :] < seq // NSA_SLC_BLOCK)
    return jnp.asarray(ov.T.astype(np.float32), dtype=BF16)


def _ctypes(spec):
    return jnp.asarray(np.concatenate([np.full(n, kind, np.int32) for n, kind in spec]))


def _even_layer(x, mods, norm_g, w_in, sinks, w_out, tables):
    sh1, sc1, g1 = mods
    heads = w_in.shape[1] // HEAD_DIM
    n_moba = 8
    ctypes = _ctypes([(n_moba, CT_ROPE_SCALE), (n_moba, CT_ROPE), (n_moba, CT_PLAIN),
                      (8, CT_ROPE_SCALE), (SWA_KV_HEADS, CT_ROPE), (SWA_KV_HEADS, CT_PLAIN)])
    assert ctypes.shape[0] == heads
    proj = _norm_proj(x, norm_g, sc1, sh1, w_in.astype(BF16), ctypes, tables, BF16, 512)
    nb = x.shape[0] // MOBA_BLOCK
    kmean = _moba_kmean(proj, 1, n_moba * HEAD_DIM)
    kmean = jnp.pad(kmean, ((0, LANES - nb), (0, 0))).astype(BF16)
    oa = _moba_attention(proj, kmean, n_moba, 0, n_moba, 2 * n_moba)
    ob = _swa_attention(proj, sinks, 3 * n_moba, 4 * n_moba, 4 * n_moba + SWA_KV_HEADS)
    return _out_proj(oa, ob, w_out.astype(BF16), x, g1)


def _odd_layer(x, mods, norm_g, w_in, forget_b, k_pos, k_w1, k_w2, v_pos, v_w1, v_w2, w_out, tables, overlap):
    sh1, sc1, g1 = mods
    s = x.shape[0]
    n_fox = 8
    hw = n_fox * HEAD_DIM
    kvw = NSA_KV_HEADS * HEAD_DIM
    o_fc = 3 * hw
    o_qd = o_fc + n_fox
    o_gd = o_qd + hw + 6 * kvw
    wb = w_in.astype(BF16)
    w_main = jnp.concatenate([wb[:, :o_fc], wb[:, o_qd:o_gd]], axis=1)
    n_aux = n_fox + 3 * 8
    w_aux = jnp.concatenate([wb[:, o_fc:o_qd], wb[:, o_gd:], jnp.zeros((w_in.shape[0], LANES - n_aux), BF16)], axis=1)
    ctypes = _ctypes([(8, CT_SCALE), (8, CT_PLAIN), (8, CT_PLAIN), (8, CT_ROPE_SCALE),
                      (2, CT_ROPE), (2, CT_PLAIN), (2, CT_ROPE), (2, CT_PLAIN), (2, CT_ROPE), (2, CT_PLAIN)])
    proj = _norm_proj(x, norm_g, sc1, sh1, w_main, ctypes, tables, BF16, 512)
    aux = _norm_proj(x, norm_g, sc1, sh1, w_aux, _ctypes([(1, CT_PLAIN)]), tables, F32, LANES)

    bias_row = jnp.pad(forget_b.astype(F32), (0, LANES - n_fox)).reshape(1, LANES)
    fox_qx, fox_kx = _fox_cum(aux, bias_row, n_fox)
    oc = _fox_attention(proj, fox_qx, fox_kx, n_fox, 0, 8, 16)

    c0 = 32
    cmp_in = proj[:, c0 * HEAD_DIM:(c0 + 4) * HEAD_DIM].reshape(s, 4, HEAD_DIM).transpose(1, 0, 2)
    cmp_in = cmp_in.reshape(4, s // NSA_CMP_STRIDE, NSA_CMP_STRIDE * HEAD_DIM)
    kv_c = _nsa_compress(
        cmp_in,
        jnp.stack([k_w1, v_w1]).astype(BF16),
        jnp.stack([k_pos.reshape(1, -1), v_pos.reshape(1, -1)]).astype(BF16),
        jnp.stack([k_w2, v_w2]).astype(BF16))
    o_cmp, sel = _nsa_cmp_select(proj, kv_c, overlap, 24)
    o_slc = _nsa_selected(proj, sel, 24, c0 + 4, c0 + 6)
    gates = aux[:, n_fox:n_aux].reshape(s, NSA_KV_HEADS, 3 * GROUP).transpose(1, 0, 2)
    gates = jnp.pad(gates, ((0, 0), (0, 0), (0, LANES - 3 * GROUP)))
    od = _nsa_window_merge(proj, gates, o_cmp, o_slc, 24, c0 + 8, c0 + 10)
    return _out_proj(oc, od, w_out.astype(BF16), x, g1)


def kernel(x, c, norm_mix_g, norm_mlp_g, ada_w, ada_b, mlp_up, mlp_down, even_w_in, even_sinks, even_w_out, odd_w_in, fox_forget_b, nsa_k_pos, nsa_k_w1, nsa_k_w2, nsa_v_pos, nsa_v_w1, nsa_v_w2, odd_w_out, final_norm_g):
    batch, seq, d = x.shape
    assert batch == 1
    depth = ada_w.shape[0]
    tables = _epilogue_tables(seq)
    overlap = _overlap_matrix(seq)
    mod = _ada_mod(c, ada_w, ada_b).reshape(depth, 6, 1, d)
    xs = x[0]
    for i in range(depth):
        sh1, sc1, g1, sh2, sc2, g2 = [mod[i, t] for t in range(6)]
        ng = norm_mix_g[i].reshape(1, d)
        j = i // 2
        if i % 2 == 0:
            xs = _even_layer(xs, (sh1, sc1, g1), ng, even_w_in[j], even_sinks[j], even_w_out[j], tables)
        else:
            xs = _odd_layer(xs, (sh1, sc1, g1), ng, odd_w_in[j], fox_forget_b[j], nsa_k_pos[j], nsa_k_w1[j],
                            nsa_k_w2[j], nsa_v_pos[j], nsa_v_w1[j], nsa_v_w2[j], odd_w_out[j], tables, overlap)
        xs = _mlp(xs, norm_mlp_g[i].reshape(1, d), sc2, sh2, mlp_up[i].astype(BF16), mlp_down[i].astype(BF16), g2)
    return _final_norm(xs, final_norm_g.reshape(1, d))[None]
```

```python
import functools
import math

import numpy as np
import jax
import jax.numpy as jnp
from jax import lax
from jax.experimental import pallas as pl
from jax.experimental.pallas import tpu as pltpu

HEAD_DIM = 128
ROPE_THETA = 10000.0
NORM_EPS = 1e-6
MOBA_BLOCK = 256
MOBA_TOPK = 3
SWA_KV_HEADS = 2
SWA_WINDOW = 128
NSA_KV_HEADS = 2
NSA_CMP_LEN = 32
NSA_CMP_STRIDE = 16
NSA_SLC_BLOCK = 64
SLC_SHIFT = 6
NSA_SLC_TOPK = 16
NSA_WINDOW = 512
BAND_BLOCK = 128
NEG_INF = -1e30
SCALE = HEAD_DIM ** -0.5
LANES = 128
GROUP = 4
VMEM_LIMIT = 48 * 1024 * 1024

F32 = jnp.float32
BF16 = jnp.bfloat16

CT_PLAIN, CT_ROPE, CT_ROPE_SCALE, CT_SCALE = 0, 1, 2, 3


def _cparams(*sem):
    return pltpu.CompilerParams(dimension_semantics=sem, vmem_limit_bytes=VMEM_LIMIT)


def _dot(a, b):
    return jnp.dot(a, b, preferred_element_type=F32)


def _dot_t(a, b):
    return lax.dot_general(a, b, (((1,), (1,)), ((), ())), preferred_element_type=F32)


def _ada_kernel(c_ref, w_ref, b_ref, o_ref):
    c = c_ref[...]
    cond = c * jax.nn.sigmoid(c)
    cond8 = jnp.broadcast_to(cond, (8, cond.shape[1])).astype(BF16)
    y = _dot(cond8, w_ref[...].astype(BF16))
    o_ref[...] = y[0:1] + b_ref[...]


def _ada_mod(c, ada_w, ada_b):
    depth, d, n = ada_w.shape
    tn = 1024
    return pl.pallas_call(
        _ada_kernel,
        grid=(depth, n // tn),
        in_specs=[
            pl.BlockSpec((1, d), lambda l, j: (0, 0)),
            pl.BlockSpec((None, d, tn), lambda l, j: (l, 0, j)),
            pl.BlockSpec((None, 1, tn), lambda l, j: (l, 0, j)),
        ],
        out_specs=pl.BlockSpec((None, 1, tn), lambda l, j: (l, 0, j)),
        out_shape=jax.ShapeDtypeStruct((depth, 1, n), F32),
        compiler_params=_cparams("arbitrary", "arbitrary"),
        name="ada_mod",
    )(c, ada_w, ada_b.reshape(depth, 1, n))


def _norm_mod(x, g, sc, sh):
    y = x * lax.rsqrt(jnp.mean(x * x, axis=-1, keepdims=True) + NORM_EPS)
    return (y * g) * (1.0 + sc) + sh


def _proj_kernel(ct_ref, x_ref, g_ref, sc_ref, sh_ref, w_ref, ta_ref, tb_ref, o_ref, h_ref):
    j = pl.program_id(1)

    @pl.when(j == 0)
    def _():
        h_ref[...] = _norm_mod(x_ref[...], g_ref[...], sc_ref[...], sh_ref[...]).astype(BF16)

    y = _dot(h_ref[...], w_ref[...].astype(BF16))
    groups = y.shape[1] // LANES
    for gi in range(groups):
        lo, hi = gi * LANES, (gi + 1) * LANES
        yg = y[:, lo:hi]
        ct = ct_ref[j * groups + gi]
        r = yg * ta_ref[ct] + pltpu.roll(yg, HEAD_DIM // 2, 1) * tb_ref[ct]
        o_ref[:, lo:hi] = r.astype(o_ref.dtype)


def _epilogue_tables(seq):
    f32 = np.float32
    inv = (f32(1.0) / (f32(ROPE_THETA) ** (np.arange(0, HEAD_DIM, 2, dtype=f32) / f32(HEAD_DIM)))).astype(f32)
    ang = np.arange(seq, dtype=f32)[:, None] * inv[None, :]
    cos, sin = np.cos(ang), np.sin(ang)
    cosf = np.concatenate([cos, cos], axis=-1)
    sinf = np.concatenate([-sin, sin], axis=-1)
    one, zero = np.ones_like(cosf), np.zeros_like(cosf)
    by_kind = {CT_PLAIN: (one, zero), CT_ROPE: (cosf, sinf), CT_ROPE_SCALE: (cosf * f32(SCALE), sinf * f32(SCALE)),
               CT_SCALE: (one * f32(SCALE), zero)}
    return (jnp.asarray(np.stack([by_kind[k][0] for k in range(4)])),
            jnp.asarray(np.stack([by_kind[k][1] for k in range(4)])))


def _norm_proj(x, g, sc, sh, w, ctypes, tables, out_dtype, tn, layer=None):
    s, d = x.shape
    n = w.shape[-1]
    tm = min(1024, s)
    kinds = tables[0].shape[0]
    if layer is None:
        w_spec = pl.BlockSpec((d, tn), lambda i, j, ct: (0, j))
    else:
        w_spec = pl.BlockSpec((None, d, tn), lambda i, j, ct: (layer, 0, j))
    grid_spec = pltpu.PrefetchScalarGridSpec(
        num_scalar_prefetch=1,
        grid=(s // tm, n // tn),
        in_specs=[
            pl.BlockSpec((tm, d), lambda i, j, ct: (i, 0)),
            pl.BlockSpec((1, d), lambda i, j, ct: (0, 0)),
            pl.BlockSpec((1, d), lambda i, j, ct: (0, 0)),
            pl.BlockSpec((1, d), lambda i, j, ct: (0, 0)),
            w_spec,
            pl.BlockSpec((kinds, tm, LANES), lambda i, j, ct: (0, i, 0)),
            pl.BlockSpec((kinds, tm, LANES), lambda i, j, ct: (0, i, 0)),
        ],
        out_specs=pl.BlockSpec((tm, tn), lambda i, j, ct: (i, j)),
        scratch_shapes=[pltpu.VMEM((tm, d), BF16)],
    )
    return pl.pallas_call(
        _proj_kernel,
        grid_spec=grid_spec,
        out_shape=jax.ShapeDtypeStruct((s, n), out_dtype),
        compiler_params=_cparams("arbitrary", "arbitrary"),
        name="norm_proj",
    )(ctypes, x, g, sc, sh, w, *tables)


def _outproj_kernel(a_ref, b_ref, w_ref, x_ref, g_ref, o_ref):
    ha = a_ref.shape[1]
    y = _dot(a_ref[...], w_ref[0:ha, :].astype(BF16)) + _dot(b_ref[...], w_ref[ha:, :].astype(BF16))
    o_ref[...] = x_ref[...] + g_ref[...] * y


def _out_proj(oa, ob, w, layer, x, gate):
    s, d = x.shape
    ha, hb = oa.shape[1], ob.shape[1]
    tm, tn = min(1024, s), 512
    return pl.pallas_call(
        _outproj_kernel,
        grid=(s // tm, d // tn),
        in_specs=[
            pl.BlockSpec((tm, ha), lambda i, j: (i, 0)),
            pl.BlockSpec((tm, hb), lambda i, j: (i, 0)),
            pl.BlockSpec((None, ha + hb, tn), lambda i, j: (layer, 0, j)),
            pl.BlockSpec((tm, tn), lambda i, j: (i, j)),
            pl.BlockSpec((1, tn), lambda i, j: (0, j)),
        ],
        out_specs=pl.BlockSpec((tm, tn), lambda i, j: (i, j)),
        out_shape=jax.ShapeDtypeStruct((s, d), F32),
        compiler_params=_cparams("arbitrary", "arbitrary"),
        name="out_proj",
    )(oa, ob, w, x, gate)


def _mlp_kernel(x_ref, g_ref, sc_ref, sh_ref, up_ref, down_ref, gate_ref, o_ref, h_ref):
    f = pl.program_id(1)

    @pl.when(f == 0)
    def _():
        h_ref[...] = _norm_mod(x_ref[...], g_ref[...], sc_ref[...], sh_ref[...]).astype(BF16)
        o_ref[...] = jnp.zeros_like(o_ref)

    hid = jnp.maximum(_dot(h_ref[...], up_ref[...].astype(BF16)), 0.0)
    o_ref[...] += _dot((hid * hid).astype(BF16), down_ref[...].astype(BF16))

    @pl.when(f == pl.num_programs(1) - 1)
    def _():
        o_ref[...] = x_ref[...] + gate_ref[...] * o_ref[...]


def _mlp(x, g, sc, sh, up, down, layer, gate):
    s, d = x.shape
    ff = up.shape[2]
    tm, tf = min(1024, s), 512
    return pl.pallas_call(
        _mlp_kernel,
        grid=(s // tm, ff // tf),
        in_specs=[
            pl.BlockSpec((tm, d), lambda i, f: (i, 0), pipeline_mode=pl.Buffered(1)),
            pl.BlockSpec((1, d), lambda i, f: (0, 0)),
            pl.BlockSpec((1, d), lambda i, f: (0, 0)),
            pl.BlockSpec((1, d), lambda i, f: (0, 0)),
            pl.BlockSpec((None, d, tf), lambda i, f: (layer, 0, f)),
            pl.BlockSpec((None, tf, d), lambda i, f: (layer, f, 0)),
            pl.BlockSpec((1, d), lambda i, f: (0, 0)),
        ],
        out_specs=pl.BlockSpec((tm, d), lambda i, f: (i, 0), pipeline_mode=pl.Buffered(1)),
        out_shape=jax.ShapeDtypeStruct((s, d), F32),
        scratch_shapes=[pltpu.VMEM((tm, d), BF16)],
        compiler_params=_cparams("arbitrary", "arbitrary"),
        name="mlp",
    )(x, g, sc, sh, up, down, gate)


def _final_norm_kernel(x_ref, g_ref, o_ref):
    x = x_ref[...]
    o_ref[...] = x * lax.rsqrt(jnp.mean(x * x, axis=-1, keepdims=True) + NORM_EPS) * g_ref[...]


def _final_norm(x, g):
    s, d = x.shape
    tm = min(512, s)
    return pl.pallas_call(
        _final_norm_kernel,
        grid=(s // tm,),
        in_specs=[pl.BlockSpec((tm, d), lambda i: (i, 0)), pl.BlockSpec((1, d), lambda i: (0, 0))],
        out_specs=pl.BlockSpec((tm, d), lambda i: (i, 0)),
        out_shape=jax.ShapeDtypeStruct((s, d), F32),
        compiler_params=_cparams("arbitrary"),
        name="final_norm",
    )(x, g)


def _rows(ref, start, size):
    return ref[pl.ds(pl.multiple_of(start, size), size), :]


def _log2(n):
    assert n > 0 and n & (n - 1) == 0
    return n.bit_length() - 1


FLASH_TILE = 512
HEADS_PER_STEP = 4


def _resident(shape, index_map):
    return pl.BlockSpec(shape, index_map, pipeline_mode=pl.Buffered(1))


def _flash_attention(qas, k_ref, kx_ref, v_ref, state, last, tk, diag_mask):
    m_ref, alpha_ref, p_ref, acc_ref = state
    chains = range(len(qas))
    ones = jnp.ones((tk, LANES), BF16)

    def score(start, mask, first):
        rows = pl.ds(pl.multiple_of(start, tk), tk)
        for c in chains:
            ka = jnp.concatenate([k_ref[rows, c * HEAD_DIM:(c + 1) * HEAD_DIM], kx_ref[rows, :]], axis=1)
            s = _dot_t(qas[c], ka)
            if mask is not None:
                s = jnp.where(mask, s, NEG_INF)
            chunks = [s[:, i * LANES:(i + 1) * LANES] for i in range(tk // LANES)]
            m_cur = jnp.max(functools.reduce(jnp.maximum, chunks), axis=1, keepdims=True)
            if first:
                m_new = jnp.broadcast_to(m_cur, (s.shape[0], LANES))
            else:
                m_prev = m_ref[c]
                m_new = jnp.maximum(m_prev, m_cur)
                alpha_ref[c] = jnp.exp(m_prev - m_new)
            p_ref[c] = jnp.concatenate([jnp.exp(ch - m_new) for ch in chunks], axis=1).astype(BF16)
            m_ref[c] = m_new

    def accumulate(start):
        rows = pl.ds(pl.multiple_of(start, tk), tk)
        for c in chains:
            va = jnp.concatenate([v_ref[rows, c * HEAD_DIM:(c + 1) * HEAD_DIM], ones], axis=1)
            alpha = alpha_ref[c]
            acc_ref[c] = jnp.concatenate([alpha, alpha], axis=1) * acc_ref[c] + _dot(p_ref[c], va)

    for c in chains:
        acc_ref[c] = jnp.zeros(acc_ref.shape[1:], F32)
        alpha_ref[c] = jnp.ones(alpha_ref.shape[1:], F32)
    score(last * tk, diag_mask, True)

    def body(p, carry):
        accumulate(jnp.where(p == 0, last, p - 1) * tk)
        score(p * tk, None, False)
        return carry

    lax.fori_loop(0, last, body, 0)
    accumulate(jnp.where(last == 0, last, last - 1) * tk)
    return [acc_ref[c][:, :HEAD_DIM] / acc_ref[c][:, HEAD_DIM:] for c in chains]


def _flash_state(chains, rows, tk):
    return [pltpu.VMEM((chains, rows, LANES), F32), pltpu.VMEM((chains, rows, LANES), F32),
            pltpu.VMEM((chains, rows, tk), BF16), pltpu.VMEM((chains, rows, 2 * HEAD_DIM), F32)]


def _block_onehot(seq, block):
    return jnp.asarray((np.arange(seq)[:, None] // block == np.arange(LANES)[None, :]).astype(np.float32), dtype=BF16)


def _kmean_kernel(k_ref, o_ref):
    k = k_ref[...].astype(F32)
    o_ref[...] = jnp.sum(k, axis=0, keepdims=True) * (1.0 / k.shape[0])


def _moba_kmean(proj, col_block, width):
    s = proj.shape[0]
    nb = s // MOBA_BLOCK
    out = pl.pallas_call(
        _kmean_kernel,
        grid=(nb,),
        in_specs=[pl.BlockSpec((MOBA_BLOCK, width), lambda b: (b, col_block))],
        out_specs=pl.BlockSpec((None, 1, width), lambda b: (b, 0, 0)),
        out_shape=jax.ShapeDtypeStruct((nb, 1, width), F32),
        compiler_params=_cparams("arbitrary"),
        name="moba_kmean",
    )(proj)
    return out.reshape(nb, width)


def _moba_kernel(q_ref, k_ref, kx_ref, v_ref, km_ref, o_ref, *state, tk, hp):
    qi = pl.program_id(1)
    tq = q_ref.shape[0]
    blk = lax.broadcasted_iota(jnp.int32, (tq, LANES), 1)
    blkf = blk.astype(F32)
    own = jnp.right_shift(qi * tq + lax.broadcasted_iota(jnp.int32, (tq, LANES), 0), _log2(MOBA_BLOCK))
    ownf = own.astype(F32)

    qas = []
    for hh in range(hp):
        cols = slice(hh * HEAD_DIM, (hh + 1) * HEAD_DIM)
        q = q_ref[:, cols]
        g = jnp.where(blk < own, _dot_t(q, km_ref[:, cols]), NEG_INF)
        sel = jnp.zeros((tq, LANES), F32)
        for _ in range(MOBA_TOPK):
            mx = jnp.max(g, axis=1, keepdims=True)
            idx = jnp.min(jnp.where(g == mx, blkf, float(LANES)), axis=1, keepdims=True)
            pick = blkf == idx
            sel = jnp.where(pick & (idx < ownf), 1.0, sel)
            g = jnp.where(pick, -jnp.inf, g)
        qx = jnp.where((sel > 0.5) | (blk == own), 0.0, NEG_INF).astype(BF16)
        qas.append(jnp.concatenate([q, qx], axis=1))

    last = jnp.right_shift(qi, _log2(tk // tq))
    t = qi * tq + lax.broadcasted_iota(jnp.int32, (tq, tk), 0)
    kpos = last * tk + lax.broadcasted_iota(jnp.int32, (tq, tk), 1)
    outs = _flash_attention(qas, k_ref, kx_ref, v_ref, state, last, tk, kpos <= t)
    for hh in range(hp):
        o_ref[:, hh * HEAD_DIM:(hh + 1) * HEAD_DIM] = outs[hh].astype(o_ref.dtype)


def _moba_attention(proj, kmean, n_heads, q_cb, k_cb, v_cb):
    s = proj.shape[0]
    tq = tk = min(FLASH_TILE, s)
    hp = HEADS_PER_STEP
    hw = hp * HEAD_DIM
    assert s % tk == 0 and tk % MOBA_BLOCK == 0 and n_heads % hp == 0 and q_cb % hp == k_cb % hp == v_cb % hp == 0
    return pl.pallas_call(
        functools.partial(_moba_kernel, tk=tk, hp=hp),
        grid=(n_heads // hp, s // tq),
        in_specs=[
            pl.BlockSpec((tq, hw), lambda h, i: (i, q_cb // hp + h)),
            _resident((s, hw), lambda h, i: (0, k_cb // hp + h)),
            _resident((s, LANES), lambda h, i: (0, 0)),
            _resident((s, hw), lambda h, i: (0, v_cb // hp + h)),
            pl.BlockSpec((LANES, hw), lambda h, i: (0, h)),
        ],
        out_specs=pl.BlockSpec((tq, hw), lambda h, i: (i, h)),
        out_shape=jax.ShapeDtypeStruct((s, n_heads * HEAD_DIM), BF16),
        scratch_shapes=_flash_state(hp, tq, tk),
        compiler_params=_cparams("arbitrary", "arbitrary"),
        name="moba_attn",
    )(proj, proj, _block_onehot(s, MOBA_BLOCK), proj, kmean)


def _swa_kernel(sink_ref, q_ref, k_ref, v_ref, o_ref, qs_ref):
    qi = pl.program_id(0)
    tq = q_ref.shape[0]
    tk = 2 * tq
    rows = GROUP * tq
    gw = GROUP * HEAD_DIM
    start = pl.multiple_of(jnp.maximum(qi - 1, 0) * tq, tq)
    t = qi * tq + (lax.broadcasted_iota(jnp.int32, (rows, tk), 0) & (tq - 1))
    kpos = start + lax.broadcasted_iota(jnp.int32, (rows, tk), 1)
    ok = (kpos <= t) & (kpos > t - SWA_WINDOW)
    ones = jnp.ones((tk, LANES), BF16)
    for h in range(SWA_KV_HEADS):
        for g in range(GROUP):
            qs_ref[h, g * tq:(g + 1) * tq, :] = q_ref[:, h * gw + g * HEAD_DIM:h * gw + (g + 1) * HEAD_DIM]
        cols = slice(h * HEAD_DIM, (h + 1) * HEAD_DIM)
        s = jnp.where(ok, _dot_t(qs_ref[h], k_ref[pl.ds(start, tk), cols]), NEG_INF)
        sink = jnp.concatenate([jnp.full((tq, 1), sink_ref[h * GROUP + g], F32) for g in range(GROUP)], axis=0)
        m = jnp.maximum(jnp.max(jnp.maximum(s[:, :tq], s[:, tq:]), axis=1, keepdims=True), sink)
        p = jnp.exp(s - m)
        pv = _dot(p.astype(BF16), jnp.concatenate([v_ref[pl.ds(start, tk), cols], ones], axis=1))
        o = pv[:, :HEAD_DIM] / (pv[:, HEAD_DIM:] + jnp.exp(sink - m))
        for g in range(GROUP):
            o_ref[:, h * gw + g * HEAD_DIM:h * gw + (g + 1) * HEAD_DIM] = o[g * tq:(g + 1) * tq, :].astype(o_ref.dtype)


def _swa_attention(proj, sinks, q_cb, k_cb, v_cb):
    s = proj.shape[0]
    tq = BAND_BLOCK
    nq = SWA_KV_HEADS * GROUP
    kvw = SWA_KV_HEADS * HEAD_DIM
    assert SWA_WINDOW <= tq and tq == LANES and q_cb % nq == 0 and k_cb % SWA_KV_HEADS == v_cb % SWA_KV_HEADS == 0
    return pl.pallas_call(
        _swa_kernel,
        grid=(s // tq,),
        in_specs=[
            pl.BlockSpec(memory_space=pltpu.SMEM),
            pl.BlockSpec((tq, nq * HEAD_DIM), lambda i: (i, q_cb // nq)),
            _resident((s, kvw), lambda i: (0, k_cb // SWA_KV_HEADS)),
            _resident((s, kvw), lambda i: (0, v_cb // SWA_KV_HEADS)),
        ],
        out_specs=pl.BlockSpec((tq, nq * HEAD_DIM), lambda i: (i, 0)),
        out_shape=jax.ShapeDtypeStruct((s, nq * HEAD_DIM), BF16),
        scratch_shapes=[pltpu.VMEM((SWA_KV_HEADS, GROUP * tq, HEAD_DIM), BF16)],
        compiler_params=_cparams("arbitrary"),
        name="swa_attn",
    )(sinks, proj, proj, proj)


FOX_LANES = 16


def _split3(x):
    x1 = x.astype(BF16)
    r1 = x - x1.astype(F32)
    x2 = r1.astype(BF16)
    return x1, x2, (r1 - x2.astype(F32)).astype(BF16)


def _fox_tables(n_heads):
    place = np.zeros((6, LANES, LANES), np.float32)
    ones_q = np.zeros((1, LANES), np.float32)
    ones_k = np.zeros((1, LANES), np.float32)
    for h in range(n_heads):
        for i in range(6):
            place[i, h, FOX_LANES * h + i] = 1.0
        ones_q[0, FOX_LANES * h + 3:FOX_LANES * h + 6] = 1.0
        ones_k[0, FOX_LANES * h:FOX_LANES * h + 3] = 1.0
    return jnp.asarray(place, dtype=BF16), jnp.asarray(ones_q), jnp.asarray(ones_k)


def _fox_cum_kernel(aux_ref, b_ref, place_ref, oq_ref, ok_ref, qx_ref, kx_ref, carry_ref):
    i = pl.program_id(0)

    @pl.when(i == 0)
    def _():
        carry_ref[...] = jnp.zeros_like(carry_ref)

    x = aux_ref[...] + b_ref[...]
    log_f = -(jnp.maximum(-x, 0.0) + jnp.log(1.0 + jnp.exp(-jnp.abs(x))))
    tb = x.shape[0]
    r = lax.broadcasted_iota(jnp.int32, (tb, tb), 0)
    c = lax.broadcasted_iota(jnp.int32, (tb, tb), 1)
    tri = jnp.where(c <= r, 1.0, 0.0).astype(BF16)
    x1, x2, x3 = _split3(log_f)
    cum = _dot(tri, x1) + _dot(tri, x2) + _dot(tri, x3) + carry_ref[...]
    carry_ref[...] = cum[tb - 1:tb, :]
    c1, c2, c3 = _split3(cum)
    qx = _dot(c1, place_ref[0]) + _dot(c2, place_ref[1]) + _dot(c3, place_ref[2]) + oq_ref[...]
    kx = ok_ref[...] - (_dot(c1, place_ref[3]) + _dot(c2, place_ref[4]) + _dot(c3, place_ref[5]))
    qx_ref[...] = qx.astype(BF16)
    kx_ref[...] = kx.astype(BF16)


def _fox_cum(aux, bias_row, n_heads):
    s, w = aux.shape
    tb = min(512, s)
    place, ones_q, ones_k = _fox_tables(n_heads)
    full = lambda shape: pl.BlockSpec(shape, lambda i: (0,) * len(shape))
    return pl.pallas_call(
        _fox_cum_kernel,
        grid=(s // tb,),
        in_specs=[pl.BlockSpec((tb, w), lambda i: (i, 0)), full((1, w)), full(place.shape), full((1, w)), full((1, w))],
        out_specs=[pl.BlockSpec((tb, w), lambda i: (i, 0)), pl.BlockSpec((tb, w), lambda i: (i, 0))],
        out_shape=[jax.ShapeDtypeStruct((s, w), BF16), jax.ShapeDtypeStruct((s, w), BF16)],
        scratch_shapes=[pltpu.VMEM((1, w), F32)],
        compiler_params=_cparams("arbitrary"),
        name="fox_cum",
    )(aux, bias_row, place, ones_q, ones_k)


def _fox_kernel(q_ref, qx_ref, k_ref, kx_ref, v_ref, o_ref, *state, tk, hp):
    hb = pl.program_id(0)
    qi = pl.program_id(1)
    tq = q_ref.shape[0]
    lane = lax.broadcasted_iota(jnp.int32, (tq, LANES), 1)
    qxf = qx_ref[...].astype(F32)
    qas = []
    for hh in range(hp):
        mine = jnp.right_shift(lane, 4) == hb * hp + hh
        qx = jnp.where(mine, qxf, 0.0).astype(BF16)
        qas.append(jnp.concatenate([q_ref[:, hh * HEAD_DIM:(hh + 1) * HEAD_DIM], qx], axis=1))

    last = jnp.right_shift(qi, _log2(tk // tq))
    t = qi * tq + lax.broadcasted_iota(jnp.int32, (tq, tk), 0)
    kpos = last * tk + lax.broadcasted_iota(jnp.int32, (tq, tk), 1)
    outs = _flash_attention(qas, k_ref, kx_ref, v_ref, state, last, tk, kpos <= t)
    for hh in range(hp):
        o_ref[:, hh * HEAD_DIM:(hh + 1) * HEAD_DIM] = outs[hh].astype(o_ref.dtype)


def _fox_attention(proj, qx, kx, n_heads, q_cb, k_cb, v_cb):
    s = proj.shape[0]
    tq = tk = min(FLASH_TILE, s)
    hp = HEADS_PER_STEP
    hw = hp * HEAD_DIM
    assert s % tk == 0 and FOX_LANES == 16 and n_heads % hp == 0 and q_cb % hp == k_cb % hp == v_cb % hp == 0
    return pl.pallas_call(
        functools.partial(_fox_kernel, tk=tk, hp=hp),
        grid=(n_heads // hp, s // tq),
        in_specs=[
            pl.BlockSpec((tq, hw), lambda h, i: (i, q_cb // hp + h)),
            pl.BlockSpec((tq, LANES), lambda h, i: (i, 0)),
            _resident((s, hw), lambda h, i: (0, k_cb // hp + h)),
            _resident((s, LANES), lambda h, i: (0, 0)),
            _resident((s, hw), lambda h, i: (0, v_cb // hp + h)),
        ],
        out_specs=pl.BlockSpec((tq, hw), lambda h, i: (i, h)),
        out_shape=jax.ShapeDtypeStruct((s, n_heads * HEAD_DIM), BF16),
        scratch_shapes=_flash_state(hp, tq, tk),
        compiler_params=_cparams("arbitrary", "arbitrary"),
        name="fox_attn",
    )(proj, qx, proj, kx, proj)


def _gelu_tanh(x):
    return 0.5 * x * (1.0 + jnp.tanh(math.sqrt(2.0 / math.pi) * (x + 0.044715 * (x * x * x))))


def _nsa_compress_kernel(a_ref, w1_ref, pos_ref, w2_ref, o_ref):
    a = a_ref[...]
    half = a.shape[1]
    n = a.shape[0]
    lo = _dot(a, w1_ref[0:half, :])
    hi = _dot(a, w1_ref[half:, :])
    pos8 = jnp.broadcast_to(pos_ref[...], (8, pos_ref.shape[1]))
    pterm = _dot(pos8, w1_ref[...])[0:1]
    pre = lo + pltpu.roll(hi, n - 1, 0) + pterm
    o_ref[...] = _dot(_gelu_tanh(pre).astype(BF16), w2_ref[...]).astype(o_ref.dtype)


def _nsa_compress(a, w1, pos, w2):
    n, rows, width = a.shape
    return pl.pallas_call(
        _nsa_compress_kernel,
        grid=(n,),
        in_specs=[
            pl.BlockSpec((None, rows, width), lambda i: (i, 0, 0)),
            pl.BlockSpec((None, 2 * width, HEAD_DIM), lambda i: (i // NSA_KV_HEADS, 0, 0)),
            pl.BlockSpec((None, 1, 2 * width), lambda i: (i // NSA_KV_HEADS, 0, 0)),
            pl.BlockSpec((None, HEAD_DIM, HEAD_DIM), lambda i: (i // NSA_KV_HEADS, 0, 0)),
        ],
        out_specs=pl.BlockSpec((None, rows, HEAD_DIM), lambda i: (i, 0, 0)),
        out_shape=jax.ShapeDtypeStruct((n, rows, HEAD_DIM), BF16),
        compiler_params=_cparams("arbitrary"),
        name="nsa_compress",
    )(a, w1, pos, w2)


def _nsa_cmp_kernel(q_ref, kc_ref, vc_ref, ovt_ref, ocmp_ref, sel_ref):
    qi = pl.program_id(0)
    tq = q_ref.shape[0]
    ncp = kc_ref.shape[1]
    gw = GROUP * HEAD_DIM
    t = qi * tq + lax.broadcasted_iota(jnp.int32, (tq, ncp), 0)
    n = lax.broadcasted_iota(jnp.int32, (tq, ncp), 1)
    vis = (n * NSA_CMP_STRIDE + (NSA_CMP_LEN - 1)) <= t
    ones = jnp.ones((ncp, LANES), BF16)
    blk = lax.broadcasted_iota(jnp.int32, (LANES, tq), 0)
    blkf = blk.astype(F32)
    cur = (qi * tq + lax.broadcasted_iota(jnp.int32, (LANES, tq), 1)) >> SLC_SHIFT
    forced = (blk == 0) | (blk == cur) | (blk == cur - 1)
    future = blk > cur
    ovt = ovt_ref[...]

    for h in range(NSA_KV_HEADS):
        kc = kc_ref[h]
        va = jnp.concatenate([vc_ref[h], ones], axis=1)
        psum = jnp.zeros((tq, ncp), F32)
        for g in range(GROUP):
            cols = slice(h * gw + g * HEAD_DIM, h * gw + (g + 1) * HEAD_DIM)
            s = jnp.where(vis, _dot_t(q_ref[:, cols], kc), NEG_INF)
            chunks = [s[:, c * LANES:(c + 1) * LANES] for c in range(ncp // LANES)]
            m = jnp.max(functools.reduce(jnp.maximum, chunks), axis=1, keepdims=True)
            p = jnp.where(vis, jnp.exp(s - m), 0.0)
            pv = _dot(p.astype(BF16), va)
            inv = 1.0 / jnp.maximum(pv[:, HEAD_DIM:], 1e-30)
            ocmp_ref[:, cols] = pv[:, :HEAD_DIM] * inv
            psum = psum + p * jnp.concatenate([inv] * (ncp // LANES), axis=1)

        p_hi = psum.astype(BF16)
        p_lo = (psum - p_hi.astype(F32)).astype(BF16)
        imp = _dot_t(ovt, p_hi) + _dot_t(ovt, p_lo)
        imp = jnp.where(forced, jnp.inf, imp)
        imp = jnp.where(future, -jnp.inf, imp)
        sel = jnp.zeros(imp.shape, F32)
        for _ in range(NSA_SLC_TOPK):
            mx = jnp.max(imp, axis=0, keepdims=True)
            idx = jnp.min(jnp.where(imp == mx, blkf, float(LANES)), axis=0, keepdims=True)
            pick = blkf == idx
            sel = jnp.where(pick, 1.0, sel)
            imp = jnp.where(pick, -jnp.inf, imp)
        sel_ref[h] = jnp.where(sel > 0.5, 0.0, NEG_INF).T.astype(sel_ref.dtype)


def _nsa_cmp_select(proj, kv_c, overlap_t, q_cb):
    s = proj.shape[0]
    tq = BAND_BLOCK
    nq = NSA_KV_HEADS * GROUP
    ncp = kv_c.shape[1]
    assert q_cb % nq == 0 and tq == LANES
    return pl.pallas_call(
        _nsa_cmp_kernel,
        grid=(s // tq,),
        in_specs=[
            pl.BlockSpec((tq, nq * HEAD_DIM), lambda i: (i, q_cb // nq)),
            pl.BlockSpec((NSA_KV_HEADS, ncp, HEAD_DIM), lambda i: (0, 0, 0)),
            pl.BlockSpec((NSA_KV_HEADS, ncp, HEAD_DIM), lambda i: (1, 0, 0)),
            pl.BlockSpec((LANES, ncp), lambda i: (0, 0)),
        ],
        out_specs=[
            pl.BlockSpec((tq, nq * HEAD_DIM), lambda i: (i, 0)),
            pl.BlockSpec((NSA_KV_HEADS, tq, LANES), lambda i: (0, i, 0)),
        ],
        out_shape=[
            jax.ShapeDtypeStruct((s, nq * HEAD_DIM), F32),
            jax.ShapeDtypeStruct((NSA_KV_HEADS, s, LANES), BF16),
        ],
        compiler_params=_cparams("arbitrary"),
        name="nsa_cmp_select",
    )(proj, kv_c, kv_c, overlap_t)


def _nsa_slc_kernel(q_ref, k_ref, kx_ref, v_ref, sel_ref, o_ref, qa_ref, *state, tk):
    qi = pl.program_id(0)
    tq = q_ref.shape[0]
    rows = GROUP * tq
    gw = GROUP * HEAD_DIM
    for h in range(NSA_KV_HEADS):
        selb = sel_ref[h]
        for g in range(GROUP):
            qa_ref[h, g * tq:(g + 1) * tq, 0:HEAD_DIM] = q_ref[:, h * gw + g * HEAD_DIM:h * gw + (g + 1) * HEAD_DIM]
            qa_ref[h, g * tq:(g + 1) * tq, HEAD_DIM:] = selb
    qas = [qa_ref[h] for h in range(NSA_KV_HEADS)]

    last = jnp.right_shift(qi, _log2(tk // tq))
    t = qi * tq + (lax.broadcasted_iota(jnp.int32, (rows, tk), 0) & (tq - 1))
    kpos = last * tk + lax.broadcasted_iota(jnp.int32, (rows, tk), 1)
    outs = _flash_attention(qas, k_ref, kx_ref, v_ref, state, last, tk, kpos <= t)
    for h in range(NSA_KV_HEADS):
        for g in range(GROUP):
            o_ref[:, h * gw + g * HEAD_DIM:h * gw + (g + 1) * HEAD_DIM] = outs[h][g * tq:(g + 1) * tq, :]


def _nsa_selected(proj, selb, q_cb, k_cb, v_cb):
    s = proj.shape[0]
    tq = 2 * BAND_BLOCK
    tk = min(FLASH_TILE, s)
    nq = NSA_KV_HEADS * GROUP
    kvw = NSA_KV_HEADS * HEAD_DIM
    assert tq & (tq - 1) == 0 and s % tk == 0 and q_cb % nq == 0 and k_cb % NSA_KV_HEADS == v_cb % NSA_KV_HEADS == 0
    return pl.pallas_call(
        functools.partial(_nsa_slc_kernel, tk=tk),
        grid=(s // tq,),
        in_specs=[
            pl.BlockSpec((tq, nq * HEAD_DIM), lambda i: (i, q_cb // nq)),
            _resident((s, kvw), lambda i: (0, k_cb // NSA_KV_HEADS)),
            _resident((s, LANES), lambda i: (0, 0)),
            _resident((s, kvw), lambda i: (0, v_cb // NSA_KV_HEADS)),
            pl.BlockSpec((NSA_KV_HEADS, tq, LANES), lambda i: (0, i, 0)),
        ],
        out_specs=pl.BlockSpec((tq, nq * HEAD_DIM), lambda i: (i, 0)),
        out_shape=jax.ShapeDtypeStruct((s, nq * HEAD_DIM), F32),
        scratch_shapes=[pltpu.VMEM((NSA_KV_HEADS, GROUP * tq, 2 * HEAD_DIM), BF16)]
        + _flash_state(NSA_KV_HEADS, GROUP * tq, tk),
        compiler_params=_cparams("arbitrary"),
        name="nsa_selected",
    )(proj, proj, _block_onehot(s, NSA_SLC_BLOCK), proj, selb)


def _nsa_win_kernel(q_ref, k_ref, v_ref, gate_ref, ocmp_ref, oslc_ref, o_ref, qs_ref):
    qi = pl.program_id(0)
    tq = q_ref.shape[0]
    gw = GROUP * HEAD_DIM
    rows = GROUP * tq
    n_prev = -(-(NSA_WINDOW - 1) // tq)
    row = lax.broadcasted_iota(jnp.int32, (rows, tq), 0) & (tq - 1)
    col = lax.broadcasted_iota(jnp.int32, (rows, tq), 1)
    ones = jnp.ones((tq, LANES), BF16)
    for h in range(NSA_KV_HEADS):
        for g in range(GROUP):
            qs_ref[h, g * tq:(g + 1) * tq, :] = q_ref[:, h * gw + g * HEAD_DIM:h * gw + (g + 1) * HEAD_DIM]
        q = qs_ref[h]
        cols = slice(h * HEAD_DIM, (h + 1) * HEAD_DIM)
        scores, starts = [], []
        for b in range(n_prev + 1):
            kb = qi - b
            start = pl.multiple_of(jnp.maximum(kb, 0) * tq, tq)
            s = _dot_t(q, k_ref[pl.ds(start, tq), cols])
            dist = row - col + b * tq
            if b * tq - (tq - 1) < 0:
                s = jnp.where(dist >= 0, s, NEG_INF)
            if b * tq + (tq - 1) >= NSA_WINDOW:
                s = jnp.where(dist < NSA_WINDOW, s, NEG_INF)
            if b > 0:
                s = s + jnp.where(kb >= 0, 0.0, NEG_INF)
            scores.append(s)
            starts.append(start)
        m = jnp.max(functools.reduce(jnp.maximum, scores), axis=1, keepdims=True)
        pv = None
        for s, start in zip(scores, starts):
            va = jnp.concatenate([v_ref[pl.ds(start, tq), cols], ones], axis=1)
            term = _dot(jnp.exp(s - m).astype(BF16), va)
            pv = term if pv is None else pv + term
        o_win = pv[:, :HEAD_DIM] / pv[:, HEAD_DIM:]
        sig = jax.nn.sigmoid(gate_ref[h])
        for g in range(GROUP):
            oc = slice(h * gw + g * HEAD_DIM, h * gw + (g + 1) * HEAD_DIM)
            o = (sig[:, 3 * g:3 * g + 1] * ocmp_ref[:, oc]
                 + sig[:, 3 * g + 1:3 * g + 2] * oslc_ref[:, oc]
                 + sig[:, 3 * g + 2:3 * g + 3] * o_win[g * tq:(g + 1) * tq, :])
            o_ref[:, oc] = o.astype(o_ref.dtype)


def _nsa_window_merge(proj, gates, o_cmp, o_slc, q_cb, k_cb, v_cb):
    s = proj.shape[0]
    tq = BAND_BLOCK
    nq = NSA_KV_HEADS * GROUP
    kvw = NSA_KV_HEADS * HEAD_DIM
    assert tq & (tq - 1) == 0 and q_cb % nq == 0 and k_cb % NSA_KV_HEADS == v_cb % NSA_KV_HEADS == 0
    wide = pl.BlockSpec((tq, nq * HEAD_DIM), lambda i: (i, 0))
    return pl.pallas_call(
        _nsa_win_kernel,
        grid=(s // tq,),
        in_specs=[
            pl.BlockSpec((tq, nq * HEAD_DIM), lambda i: (i, q_cb // nq)),
            _resident((s, kvw), lambda i: (0, k_cb // NSA_KV_HEADS)),
            _resident((s, kvw), lambda i: (0, v_cb // NSA_KV_HEADS)),
            pl.BlockSpec((NSA_KV_HEADS, tq, LANES), lambda i: (0, i, 0)),
            wide,
            wide,
        ],
        out_specs=wide,
        out_shape=jax.ShapeDtypeStruct((s, nq * HEAD_DIM), BF16),
        scratch_shapes=[pltpu.VMEM((NSA_KV_HEADS, GROUP * tq, HEAD_DIM), BF16)],
        compiler_params=_cparams("arbitrary"),
        name="nsa_window_merge",
    )(proj, proj, proj, gates, o_cmp, o_slc)


def _overlap_matrix(seq):
    n_pad = seq // NSA_CMP_STRIDE
    n_cmp = (seq - NSA_CMP_LEN) // NSA_CMP_STRIDE + 1
    c_start = np.arange(n_pad)[:, None] * NSA_CMP_STRIDE
    s_start = np.arange(LANES)[None, :] * NSA_SLC_BLOCK
    ov = (c_start < s_start + NSA_SLC_BLOCK) & (c_start + NSA_CMP_LEN > s_start)
    ov &= (np.arange(n_pad)[:, None] < n_cmp) & (np.arange(LANES)[None, :] < seq // NSA_SLC_BLOCK)
    return jnp.asarray(ov.T.astype(np.float32), dtype=BF16)


def _ctypes(spec):
    return jnp.asarray(np.concatenate([np.full(n, kind, np.int32) for n, kind in spec]))


def _even_layer(x, mods, norm_g, w_in, sinks, w_out, layer, tables):
    sh1, sc1, g1 = mods
    heads = w_in.shape[2] // HEAD_DIM
    n_moba = 8
    ctypes = _ctypes([(n_moba, CT_ROPE_SCALE), (n_moba, CT_ROPE), (n_moba, CT_PLAIN),
                      (8, CT_ROPE_SCALE), (SWA_KV_HEADS, CT_ROPE), (SWA_KV_HEADS, CT_PLAIN)])
    assert ctypes.shape[0] == heads
    proj = _norm_proj(x, norm_g, sc1, sh1, w_in, ctypes, tables, BF16, 512, layer=layer)
    nb = x.shape[0] // MOBA_BLOCK
    kmean = _moba_kmean(proj, 1, n_moba * HEAD_DIM)
    kmean = jnp.pad(kmean, ((0, LANES - nb), (0, 0))).astype(BF16)
    oa = _moba_attention(proj, kmean, n_moba, 0, n_moba, 2 * n_moba)
    ob = _swa_attention(proj, sinks, 3 * n_moba, 4 * n_moba, 4 * n_moba + SWA_KV_HEADS)
    return _out_proj(oa, ob, w_out, layer, x, g1)


def _odd_layer(x, mods, norm_g, w_in, forget_b, k_pos, k_w1, k_w2, v_pos, v_w1, v_w2, w_out, layer, tables, overlap):
    sh1, sc1, g1 = mods
    s = x.shape[0]
    n_fox = 8
    hw = n_fox * HEAD_DIM
    kvw = NSA_KV_HEADS * HEAD_DIM
    o_fc = 3 * hw
    o_qd = o_fc + n_fox
    o_gd = o_qd + hw + 6 * kvw
    w_main = jnp.concatenate([w_in[:, :o_fc], w_in[:, o_qd:o_gd]], axis=1).astype(BF16)
    n_aux = n_fox + 3 * 8
    w_aux = jnp.concatenate([w_in[:, o_fc:o_qd], w_in[:, o_gd:], jnp.zeros((w_in.shape[0], LANES - n_aux), F32)],
                            axis=1).astype(BF16)
    ctypes = _ctypes([(8, CT_SCALE), (8, CT_PLAIN), (8, CT_PLAIN), (8, CT_ROPE_SCALE),
                      (2, CT_ROPE), (2, CT_PLAIN), (2, CT_ROPE), (2, CT_PLAIN), (2, CT_ROPE), (2, CT_PLAIN)])
    proj = _norm_proj(x, norm_g, sc1, sh1, w_main, ctypes, tables, BF16, 512)
    aux = _norm_proj(x, norm_g, sc1, sh1, w_aux, _ctypes([(1, CT_PLAIN)]), tables, F32, LANES)

    bias_row = jnp.pad(forget_b.astype(F32), (0, LANES - n_fox)).reshape(1, LANES)
    fox_qx, fox_kx = _fox_cum(aux, bias_row, n_fox)
    oc = _fox_attention(proj, fox_qx, fox_kx, n_fox, 0, 8, 16)

    c0 = 32
    cmp_in = proj[:, c0 * HEAD_DIM:(c0 + 4) * HEAD_DIM].reshape(s, 4, HEAD_DIM).transpose(1, 0, 2)
    cmp_in = cmp_in.reshape(4, s // NSA_CMP_STRIDE, NSA_CMP_STRIDE * HEAD_DIM)
    kv_c = _nsa_compress(
        cmp_in,
        jnp.stack([k_w1, v_w1]).astype(BF16),
        jnp.stack([k_pos.reshape(1, -1), v_pos.reshape(1, -1)]).astype(BF16),
        jnp.stack([k_w2, v_w2]).astype(BF16))
    o_cmp, sel = _nsa_cmp_select(proj, kv_c, overlap, 24)
    o_slc = _nsa_selected(proj, sel, 24, c0 + 4, c0 + 6)
    gates = aux[:, n_fox:n_aux].reshape(s, NSA_KV_HEADS, 3 * GROUP).transpose(1, 0, 2)
    gates = jnp.pad(gates, ((0, 0), (0, 0), (0, LANES - 3 * GROUP)))
    od = _nsa_window_merge(proj, gates, o_cmp, o_slc, 24, c0 + 8, c0 + 10)
    return _out_proj(oc, od, w_out, layer, x, g1)


def kernel(x, c, norm_mix_g, norm_mlp_g, ada_w, ada_b, mlp_up, mlp_down, even_w_in, even_sinks, even_w_out, odd_w_in, fox_forget_b, nsa_k_pos, nsa_k_w1, nsa_k_w2, nsa_v_pos, nsa_v_w1, nsa_v_w2, odd_w_out, final_norm_g):
    batch, seq, d = x.shape
    assert batch == 1
    depth = ada_w.shape[0]
    tables = _epilogue_tables(seq)
    overlap = _overlap_matrix(seq)
    mod = _ada_mod(c, ada_w, ada_b).reshape(depth, 6, 1, d)
    xs = x[0]
    for i in range(depth):
        sh1, sc1, g1, sh2, sc2, g2 = [mod[i, t] for t in range(6)]
        ng = norm_mix_g[i].reshape(1, d)
        j = i // 2
        if i % 2 == 0:
            xs = _even_layer(xs, (sh1, sc1, g1), ng, even_w_in, even_sinks[j], even_w_out, j, tables)
        else:
            xs = _odd_layer(xs, (sh1, sc1, g1), ng, odd_w_in[j], fox_forget_b[j], nsa_k_pos[j], nsa_k_w1[j],
                            nsa_k_w2[j], nsa_v_pos[j], nsa_v_w1[j], nsa_v_w2[j], odd_w_out, j, tables, overlap)
        xs = _mlp(xs, norm_mlp_g[i].reshape(1, d), sc2, sh2, mlp_up, mlp_down, i, g2)
    return _final_norm(xs, final_norm_g.reshape(1, d))[None]
```

```python
import functools
import math

import numpy as np
import jax
import jax.numpy as jnp
from jax import lax
from jax.experimental import pallas as pl
from jax.experimental.pallas import tpu as pltpu

HEAD_DIM = 128
ROPE_THETA = 10000.0
NORM_EPS = 1e-6
MOBA_BLOCK = 256
MOBA_TOPK = 3
SWA_KV_HEADS = 2
SWA_WINDOW = 128
NSA_KV_HEADS = 2
NSA_CMP_LEN = 32
NSA_CMP_STRIDE = 16
NSA_SLC_BLOCK = 64
SLC_SHIFT = 6
NSA_SLC_TOPK = 16
NSA_WINDOW = 512
BAND_BLOCK = 128
NEG_INF = -1e30
SCALE = HEAD_DIM ** -0.5
LANES = 128
GROUP = 4
VMEM_LIMIT = 48 * 1024 * 1024

F32 = jnp.float32
BF16 = jnp.bfloat16

CT_PLAIN, CT_ROPE, CT_ROPE_SCALE, CT_SCALE = 0, 1, 2, 3


VMEM_LIMIT_MLP = 56 * 1024 * 1024


def _cparams(*sem, vmem=VMEM_LIMIT):
    return pltpu.CompilerParams(dimension_semantics=sem, vmem_limit_bytes=vmem)


def _dot(a, b):
    return jnp.dot(a, b, preferred_element_type=F32)


def _dot_t(a, b):
    return lax.dot_general(a, b, (((1,), (1,)), ((), ())), preferred_element_type=F32)


def _ada_kernel(c_ref, w_ref, b_ref, o_ref):
    c = c_ref[...]
    cond = c * jax.nn.sigmoid(c)
    cond8 = jnp.broadcast_to(cond, (8, cond.shape[1])).astype(BF16)
    y = _dot(cond8, w_ref[...].astype(BF16))
    o_ref[...] = y[0:1] + b_ref[...]


def _ada_mod(c, ada_w, ada_b):
    depth, d, n = ada_w.shape
    tn = 1024
    return pl.pallas_call(
        _ada_kernel,
        grid=(depth, n // tn),
        in_specs=[
            pl.BlockSpec((1, d), lambda l, j: (0, 0)),
            pl.BlockSpec((None, d, tn), lambda l, j: (l, 0, j)),
            pl.BlockSpec((None, 1, tn), lambda l, j: (l, 0, j)),
        ],
        out_specs=pl.BlockSpec((None, 1, tn), lambda l, j: (l, 0, j)),
        out_shape=jax.ShapeDtypeStruct((depth, 1, n), F32),
        compiler_params=_cparams("arbitrary", "arbitrary"),
        name="ada_mod",
    )(c, ada_w, ada_b.reshape(depth, 1, n))


def _norm_mod(x, g, sc, sh):
    y = x * lax.rsqrt(jnp.mean(x * x, axis=-1, keepdims=True) + NORM_EPS)
    return (y * g) * (1.0 + sc) + sh


def _proj_kernel(ct_ref, x_ref, g_ref, sc_ref, sh_ref, w_ref, ta_ref, tb_ref, *rest, aux):
    if aux:
        wa_ref, o_ref, oa_ref, h_ref = rest
    else:
        o_ref, h_ref = rest
    j = pl.program_id(1)

    @pl.when(j == 0)
    def _():
        h_ref[...] = _norm_mod(x_ref[...], g_ref[...], sc_ref[...], sh_ref[...]).astype(BF16)
        if aux:
            oa_ref[...] = _dot(h_ref[...], wa_ref[...])

    y = _dot(h_ref[...], w_ref[...].astype(BF16))
    groups = y.shape[1] // LANES
    for gi in range(groups):
        lo, hi = gi * LANES, (gi + 1) * LANES
        yg = y[:, lo:hi]
        ct = ct_ref[j * groups + gi]
        r = yg * ta_ref[ct] + pltpu.roll(yg, HEAD_DIM // 2, 1) * tb_ref[ct]
        o_ref[:, lo:hi] = r.astype(o_ref.dtype)


def _epilogue_tables(seq):
    f32 = np.float32
    inv = (f32(1.0) / (f32(ROPE_THETA) ** (np.arange(0, HEAD_DIM, 2, dtype=f32) / f32(HEAD_DIM)))).astype(f32)
    ang = np.arange(seq, dtype=f32)[:, None] * inv[None, :]
    cos, sin = np.cos(ang), np.sin(ang)
    cosf = np.concatenate([cos, cos], axis=-1)
    sinf = np.concatenate([-sin, sin], axis=-1)
    one, zero = np.ones_like(cosf), np.zeros_like(cosf)
    by_kind = {CT_PLAIN: (one, zero), CT_ROPE: (cosf, sinf), CT_ROPE_SCALE: (cosf * f32(SCALE), sinf * f32(SCALE)),
               CT_SCALE: (one * f32(SCALE), zero)}
    return (jnp.asarray(np.stack([by_kind[k][0] for k in range(4)])),
            jnp.asarray(np.stack([by_kind[k][1] for k in range(4)])))


def _norm_proj(x, g, sc, sh, w, ctypes, tables, tn, layer=None, w_aux=None):
    s, d = x.shape
    n = w.shape[-1]
    tm = min(1024, s)
    kinds = tables[0].shape[0]
    if layer is None:
        w_spec = pl.BlockSpec((d, tn), lambda i, j, ct: (0, j))
    else:
        w_spec = pl.BlockSpec((None, d, tn), lambda i, j, ct: (layer, 0, j))
    aux = w_aux is not None
    in_specs = [
        pl.BlockSpec((tm, d), lambda i, j, ct: (i, 0)),
        pl.BlockSpec((1, d), lambda i, j, ct: (0, 0)),
        pl.BlockSpec((1, d), lambda i, j, ct: (0, 0)),
        pl.BlockSpec((1, d), lambda i, j, ct: (0, 0)),
        w_spec,
        pl.BlockSpec((kinds, tm, LANES), lambda i, j, ct: (0, i, 0)),
        pl.BlockSpec((kinds, tm, LANES), lambda i, j, ct: (0, i, 0)),
    ]
    out_specs = pl.BlockSpec((tm, tn), lambda i, j, ct: (i, j))
    out_shape = jax.ShapeDtypeStruct((s, n), BF16)
    operands = [ctypes, x, g, sc, sh, w, *tables]
    if aux:
        in_specs.append(pl.BlockSpec((d, LANES), lambda i, j, ct: (0, 0)))
        out_specs = [out_specs, pl.BlockSpec((tm, LANES), lambda i, j, ct: (i, 0))]
        out_shape = [out_shape, jax.ShapeDtypeStruct((s, LANES), F32)]
        operands.append(w_aux)
    grid_spec = pltpu.PrefetchScalarGridSpec(
        num_scalar_prefetch=1,
        grid=(s // tm, n // tn),
        in_specs=in_specs,
        out_specs=out_specs,
        scratch_shapes=[pltpu.VMEM((tm, d), BF16)],
    )
    return pl.pallas_call(
        functools.partial(_proj_kernel, aux=aux),
        grid_spec=grid_spec,
        out_shape=out_shape,
        compiler_params=_cparams("arbitrary", "arbitrary"),
        name="norm_proj",
    )(*operands)


def _outproj_kernel(a_ref, b_ref, w_ref, x_ref, g_ref, o_ref):
    ha = a_ref.shape[1]
    y = _dot(a_ref[...], w_ref[0:ha, :].astype(BF16)) + _dot(b_ref[...], w_ref[ha:, :].astype(BF16))
    o_ref[...] = x_ref[...] + g_ref[...] * y


def _out_proj(oa, ob, w, layer, x, gate):
    s, d = x.shape
    ha, hb = oa.shape[1], ob.shape[1]
    tm, tn = min(1024, s), 512
    return pl.pallas_call(
        _outproj_kernel,
        grid=(s // tm, d // tn),
        in_specs=[
            pl.BlockSpec((tm, ha), lambda i, j: (i, 0)),
            pl.BlockSpec((tm, hb), lambda i, j: (i, 0)),
            pl.BlockSpec((None, ha + hb, tn), lambda i, j: (layer, 0, j)),
            pl.BlockSpec((tm, tn), lambda i, j: (i, j)),
            pl.BlockSpec((1, tn), lambda i, j: (0, j)),
        ],
        out_specs=pl.BlockSpec((tm, tn), lambda i, j: (i, j)),
        out_shape=jax.ShapeDtypeStruct((s, d), F32),
        compiler_params=_cparams("arbitrary", "arbitrary"),
        name="out_proj",
    )(oa, ob, w, x, gate)


def _rms_norm(x, g):
    return x * lax.rsqrt(jnp.mean(x * x, axis=-1, keepdims=True) + NORM_EPS) * g


def _mlp_kernel(x_ref, g_ref, sc_ref, sh_ref, up_ref, down_ref, gate_ref, fg_ref, o_ref, h_ref, *, final):
    f = pl.program_id(1)

    @pl.when(f == 0)
    def _():
        h_ref[...] = _norm_mod(x_ref[...], g_ref[...], sc_ref[...], sh_ref[...]).astype(BF16)
        o_ref[...] = jnp.zeros_like(o_ref)

    hid = jnp.maximum(_dot(h_ref[...], up_ref[...].astype(BF16)), 0.0)
    o_ref[...] += _dot((hid * hid).astype(BF16), down_ref[...].astype(BF16))

    @pl.when(f == pl.num_programs(1) - 1)
    def _():
        y = x_ref[...] + gate_ref[...] * o_ref[...]
        o_ref[...] = _rms_norm(y, fg_ref[...]) if final else y


def _mlp(x, g, sc, sh, up, down, layer, gate, final_g, final):
    s, d = x.shape
    ff = up.shape[2]
    tm, tf = min(1024, s), 512
    return pl.pallas_call(
        functools.partial(_mlp_kernel, final=final),
        grid=(s // tm, ff // tf),
        in_specs=[
            pl.BlockSpec((tm, d), lambda i, f: (i, 0)),
            pl.BlockSpec((1, d), lambda i, f: (0, 0)),
            pl.BlockSpec((1, d), lambda i, f: (0, 0)),
            pl.BlockSpec((1, d), lambda i, f: (0, 0)),
            pl.BlockSpec((None, d, tf), lambda i, f: (layer, 0, f)),
            pl.BlockSpec((None, tf, d), lambda i, f: (layer, f, 0)),
            pl.BlockSpec((1, d), lambda i, f: (0, 0)),
            pl.BlockSpec((1, d), lambda i, f: (0, 0)),
        ],
        out_specs=pl.BlockSpec((tm, d), lambda i, f: (i, 0), pipeline_mode=pl.Buffered(1)),
        out_shape=jax.ShapeDtypeStruct((s, d), F32),
        scratch_shapes=[pltpu.VMEM((tm, d), BF16)],
        compiler_params=_cparams("arbitrary", "arbitrary", vmem=VMEM_LIMIT_MLP),
        name="mlp",
    )(x, g, sc, sh, up, down, gate, final_g)


def _rows(ref, start, size):
    return ref[pl.ds(pl.multiple_of(start, size), size), :]


def _log2(n):
    assert n > 0 and n & (n - 1) == 0
    return n.bit_length() - 1


FLASH_TILE = 512
HEADS_PER_STEP = 4


def _resident(shape, index_map):
    return pl.BlockSpec(shape, index_map, pipeline_mode=pl.Buffered(1))


def _flash_attention(qas, k_ref, kx_ref, v_ref, state, last, tk, diag_mask):
    m_ref, alpha_ref, p_ref, acc_ref = state
    chains = range(len(qas))
    ones = jnp.ones((tk, LANES), BF16)

    def score(start, mask, first):
        rows = pl.ds(pl.multiple_of(start, tk), tk)
        for c in chains:
            ka = jnp.concatenate([k_ref[rows, c * HEAD_DIM:(c + 1) * HEAD_DIM], kx_ref[rows, :]], axis=1)
            s = _dot_t(qas[c], ka)
            if mask is not None:
                s = jnp.where(mask, s, NEG_INF)
            chunks = [s[:, i * LANES:(i + 1) * LANES] for i in range(tk // LANES)]
            m_cur = jnp.max(functools.reduce(jnp.maximum, chunks), axis=1, keepdims=True)
            if first:
                m_new = jnp.broadcast_to(m_cur, (s.shape[0], LANES))
            else:
                m_prev = m_ref[c]
                m_new = jnp.maximum(m_prev, m_cur)
                alpha_ref[c] = jnp.exp(m_prev - m_new)
            p_ref[c] = jnp.concatenate([jnp.exp(ch - m_new) for ch in chunks], axis=1).astype(BF16)
            m_ref[c] = m_new

    def accumulate(start):
        rows = pl.ds(pl.multiple_of(start, tk), tk)
        for c in chains:
            va = jnp.concatenate([v_ref[rows, c * HEAD_DIM:(c + 1) * HEAD_DIM], ones], axis=1)
            alpha = alpha_ref[c]
            acc_ref[c] = jnp.concatenate([alpha, alpha], axis=1) * acc_ref[c] + _dot(p_ref[c], va)

    for c in chains:
        acc_ref[c] = jnp.zeros(acc_ref.shape[1:], F32)
        alpha_ref[c] = jnp.ones(alpha_ref.shape[1:], F32)
    score(last * tk, diag_mask, True)

    def body(p, carry):
        accumulate(jnp.where(p == 0, last, p - 1) * tk)
        score(p * tk, None, False)
        return carry

    lax.fori_loop(0, last, body, 0)
    accumulate(jnp.where(last == 0, last, last - 1) * tk)
    return [acc_ref[c][:, :HEAD_DIM] / acc_ref[c][:, HEAD_DIM:] for c in chains]


def _flash_state(chains, rows, tk):
    return [pltpu.VMEM((chains, rows, LANES), F32), pltpu.VMEM((chains, rows, LANES), F32),
            pltpu.VMEM((chains, rows, tk), BF16), pltpu.VMEM((chains, rows, 2 * HEAD_DIM), F32)]


def _block_onehot(seq, block):
    return jnp.asarray((np.arange(seq)[:, None] // block == np.arange(LANES)[None, :]).astype(np.float32), dtype=BF16)


def _kmean_kernel(k_ref, o_ref):
    k = k_ref[...].astype(F32)
    o_ref[...] = jnp.sum(k, axis=0, keepdims=True) * (1.0 / k.shape[0])


def _moba_kmean(proj, col_block, width):
    s = proj.shape[0]
    nb = s // MOBA_BLOCK
    out = pl.pallas_call(
        _kmean_kernel,
        grid=(nb,),
        in_specs=[pl.BlockSpec((MOBA_BLOCK, width), lambda b: (b, col_block))],
        out_specs=pl.BlockSpec((None, 1, width), lambda b: (b, 0, 0)),
        out_shape=jax.ShapeDtypeStruct((nb, 1, width), F32),
        compiler_params=_cparams("arbitrary"),
        name="moba_kmean",
    )(proj)
    return out.reshape(nb, width)


def _moba_kernel(q_ref, k_ref, kx_ref, v_ref, km_ref, o_ref, *state, tk, hp):
    qi = pl.program_id(1)
    tq = q_ref.shape[0]
    blk = lax.broadcasted_iota(jnp.int32, (tq, LANES), 1)
    blkf = blk.astype(F32)
    own = jnp.right_shift(qi * tq + lax.broadcasted_iota(jnp.int32, (tq, LANES), 0), _log2(MOBA_BLOCK))
    ownf = own.astype(F32)

    qas = []
    for hh in range(hp):
        cols = slice(hh * HEAD_DIM, (hh + 1) * HEAD_DIM)
        q = q_ref[:, cols]
        g = jnp.where(blk < own, _dot_t(q, km_ref[:, cols]), NEG_INF)
        sel = jnp.zeros((tq, LANES), F32)
        for _ in range(MOBA_TOPK):
            mx = jnp.max(g, axis=1, keepdims=True)
            idx = jnp.min(jnp.where(g == mx, blkf, float(LANES)), axis=1, keepdims=True)
            pick = blkf == idx
            sel = jnp.where(pick & (idx < ownf), 1.0, sel)
            g = jnp.where(pick, -jnp.inf, g)
        qx = jnp.where((sel > 0.5) | (blk == own), 0.0, NEG_INF).astype(BF16)
        qas.append(jnp.concatenate([q, qx], axis=1))

    last = jnp.right_shift(qi, _log2(tk // tq))
    t = qi * tq + lax.broadcasted_iota(jnp.int32, (tq, tk), 0)
    kpos = last * tk + lax.broadcasted_iota(jnp.int32, (tq, tk), 1)
    outs = _flash_attention(qas, k_ref, kx_ref, v_ref, state, last, tk, kpos <= t)
    for hh in range(hp):
        o_ref[:, hh * HEAD_DIM:(hh + 1) * HEAD_DIM] = outs[hh].astype(o_ref.dtype)


def _moba_attention(proj, kmean, n_heads, q_cb, k_cb, v_cb):
    s = proj.shape[0]
    tq = tk = min(FLASH_TILE, s)
    hp = HEADS_PER_STEP
    hw = hp * HEAD_DIM
    assert s % tk == 0 and tk % MOBA_BLOCK == 0 and n_heads % hp == 0 and q_cb % hp == k_cb % hp == v_cb % hp == 0
    return pl.pallas_call(
        functools.partial(_moba_kernel, tk=tk, hp=hp),
        grid=(n_heads // hp, s // tq),
        in_specs=[
            pl.BlockSpec((tq, hw), lambda h, i: (i, q_cb // hp + h)),
            _resident((s, hw), lambda h, i: (0, k_cb // hp + h)),
            _resident((s, LANES), lambda h, i: (0, 0)),
            _resident((s, hw), lambda h, i: (0, v_cb // hp + h)),
            pl.BlockSpec((LANES, hw), lambda h, i: (0, h)),
        ],
        out_specs=pl.BlockSpec((tq, hw), lambda h, i: (i, h)),
        out_shape=jax.ShapeDtypeStruct((s, n_heads * HEAD_DIM), BF16),
        scratch_shapes=_flash_state(hp, tq, tk),
        compiler_params=_cparams("arbitrary", "arbitrary"),
        name="moba_attn",
    )(proj, proj, _block_onehot(s, MOBA_BLOCK), proj, kmean)


def _swa_kernel(sink_ref, q_ref, k_ref, v_ref, o_ref, qs_ref):
    qi = pl.program_id(0)
    tq = q_ref.shape[0]
    tk = 2 * tq
    rows = GROUP * tq
    gw = GROUP * HEAD_DIM
    start = pl.multiple_of(jnp.maximum(qi - 1, 0) * tq, tq)
    t = qi * tq + (lax.broadcasted_iota(jnp.int32, (rows, tk), 0) & (tq - 1))
    kpos = start + lax.broadcasted_iota(jnp.int32, (rows, tk), 1)
    ok = (kpos <= t) & (kpos > t - SWA_WINDOW)
    ones = jnp.ones((tk, LANES), BF16)
    for h in range(SWA_KV_HEADS):
        for g in range(GROUP):
            qs_ref[h, g * tq:(g + 1) * tq, :] = q_ref[:, h * gw + g * HEAD_DIM:h * gw + (g + 1) * HEAD_DIM]
        cols = slice(h * HEAD_DIM, (h + 1) * HEAD_DIM)
        s = jnp.where(ok, _dot_t(qs_ref[h], k_ref[pl.ds(start, tk), cols]), NEG_INF)
        sink = jnp.concatenate([jnp.full((tq, 1), sink_ref[h * GROUP + g], F32) for g in range(GROUP)], axis=0)
        m = jnp.maximum(jnp.max(jnp.maximum(s[:, :tq], s[:, tq:]), axis=1, keepdims=True), sink)
        p = jnp.exp(s - m)
        pv = _dot(p.astype(BF16), jnp.concatenate([v_ref[pl.ds(start, tk), cols], ones], axis=1))
        o = pv[:, :HEAD_DIM] / (pv[:, HEAD_DIM:] + jnp.exp(sink - m))
        for g in range(GROUP):
            o_ref[:, h * gw + g * HEAD_DIM:h * gw + (g + 1) * HEAD_DIM] = o[g * tq:(g + 1) * tq, :].astype(o_ref.dtype)


def _swa_attention(proj, sinks, q_cb, k_cb, v_cb):
    s = proj.shape[0]
    tq = BAND_BLOCK
    nq = SWA_KV_HEADS * GROUP
    kvw = SWA_KV_HEADS * HEAD_DIM
    assert SWA_WINDOW <= tq and tq == LANES and q_cb % nq == 0 and k_cb % SWA_KV_HEADS == v_cb % SWA_KV_HEADS == 0
    return pl.pallas_call(
        _swa_kernel,
        grid=(s // tq,),
        in_specs=[
            pl.BlockSpec(memory_space=pltpu.SMEM),
            pl.BlockSpec((tq, nq * HEAD_DIM), lambda i: (i, q_cb // nq)),
            _resident((s, kvw), lambda i: (0, k_cb // SWA_KV_HEADS)),
            _resident((s, kvw), lambda i: (0, v_cb // SWA_KV_HEADS)),
        ],
        out_specs=pl.BlockSpec((tq, nq * HEAD_DIM), lambda i: (i, 0)),
        out_shape=jax.ShapeDtypeStruct((s, nq * HEAD_DIM), BF16),
        scratch_shapes=[pltpu.VMEM((SWA_KV_HEADS, GROUP * tq, HEAD_DIM), BF16)],
        compiler_params=_cparams("arbitrary"),
        name="swa_attn",
    )(sinks, proj, proj, proj)


FOX_LANES = 16


def _split3(x):
    x1 = x.astype(BF16)
    r1 = x - x1.astype(F32)
    x2 = r1.astype(BF16)
    return x1, x2, (r1 - x2.astype(F32)).astype(BF16)


def _fox_tables(n_heads):
    place = np.zeros((6, LANES, LANES), np.float32)
    ones_q = np.zeros((1, LANES), np.float32)
    ones_k = np.zeros((1, LANES), np.float32)
    for h in range(n_heads):
        for i in range(6):
            place[i, h, FOX_LANES * h + i] = 1.0
        ones_q[0, FOX_LANES * h + 3:FOX_LANES * h + 6] = 1.0
        ones_k[0, FOX_LANES * h:FOX_LANES * h + 3] = 1.0
    return jnp.asarray(place, dtype=BF16), jnp.asarray(ones_q), jnp.asarray(ones_k)


def _fox_cum_kernel(aux_ref, b_ref, place_ref, oq_ref, ok_ref, qx_ref, kx_ref, carry_ref):
    i = pl.program_id(0)

    @pl.when(i == 0)
    def _():
        carry_ref[...] = jnp.zeros_like(carry_ref)

    x = aux_ref[...] + b_ref[...]
    log_f = -(jnp.maximum(-x, 0.0) + jnp.log(1.0 + jnp.exp(-jnp.abs(x))))
    tb = x.shape[0]
    r = lax.broadcasted_iota(jnp.int32, (tb, tb), 0)
    c = lax.broadcasted_iota(jnp.int32, (tb, tb), 1)
    tri = jnp.where(c <= r, 1.0, 0.0).astype(BF16)
    x1, x2, x3 = _split3(log_f)
    cum = _dot(tri, x1) + _dot(tri, x2) + _dot(tri, x3) + carry_ref[...]
    carry_ref[...] = cum[tb - 1:tb, :]
    c1, c2, c3 = _split3(cum)
    qx = _dot(c1, place_ref[0]) + _dot(c2, place_ref[1]) + _dot(c3, place_ref[2]) + oq_ref[...]
    kx = ok_ref[...] - (_dot(c1, place_ref[3]) + _dot(c2, place_ref[4]) + _dot(c3, place_ref[5]))
    qx_ref[...] = qx.astype(BF16)
    kx_ref[...] = kx.astype(BF16)


def _fox_cum(aux, bias_row, n_heads):
    s, w = aux.shape
    tb = min(512, s)
    place, ones_q, ones_k = _fox_tables(n_heads)
    full = lambda shape: pl.BlockSpec(shape, lambda i: (0,) * len(shape))
    return pl.pallas_call(
        _fox_cum_kernel,
        grid=(s // tb,),
        in_specs=[pl.BlockSpec((tb, w), lambda i: (i, 0)), full((1, w)), full(place.shape), full((1, w)), full((1, w))],
        out_specs=[pl.BlockSpec((tb, w), lambda i: (i, 0)), pl.BlockSpec((tb, w), lambda i: (i, 0))],
        out_shape=[jax.ShapeDtypeStruct((s, w), BF16), jax.ShapeDtypeStruct((s, w), BF16)],
        scratch_shapes=[pltpu.VMEM((1, w), F32)],
        compiler_params=_cparams("arbitrary"),
        name="fox_cum",
    )(aux, bias_row, place, ones_q, ones_k)


def _fox_kernel(q_ref, qx_ref, k_ref, kx_ref, v_ref, o_ref, *state, tk, hp):
    hb = pl.program_id(0)
    qi = pl.program_id(1)
    tq = q_ref.shape[0]
    lane = lax.broadcasted_iota(jnp.int32, (tq, LANES), 1)
    qxf = qx_ref[...].astype(F32)
    qas = []
    for hh in range(hp):
        mine = jnp.right_shift(lane, 4) == hb * hp + hh
        qx = jnp.where(mine, qxf, 0.0).astype(BF16)
        qas.append(jnp.concatenate([q_ref[:, hh * HEAD_DIM:(hh + 1) * HEAD_DIM], qx], axis=1))

    last = jnp.right_shift(qi, _log2(tk // tq))
    t = qi * tq + lax.broadcasted_iota(jnp.int32, (tq, tk), 0)
    kpos = last * tk + lax.broadcasted_iota(jnp.int32, (tq, tk), 1)
    outs = _flash_attention(qas, k_ref, kx_ref, v_ref, state, last, tk, kpos <= t)
    for hh in range(hp):
        o_ref[:, hh * HEAD_DIM:(hh + 1) * HEAD_DIM] = outs[hh].astype(o_ref.dtype)


def _fox_attention(proj, qx, kx, n_heads, q_cb, k_cb, v_cb):
    s = proj.shape[0]
    tq = tk = min(FLASH_TILE, s)
    hp = HEADS_PER_STEP
    hw = hp * HEAD_DIM
    assert s % tk == 0 and FOX_LANES == 16 and n_heads % hp == 0 and q_cb % hp == k_cb % hp == v_cb % hp == 0
    return pl.pallas_call(
        functools.partial(_fox_kernel, tk=tk, hp=hp),
        grid=(n_heads // hp, s // tq),
        in_specs=[
            pl.BlockSpec((tq, hw), lambda h, i: (i, q_cb // hp + h)),
            pl.BlockSpec((tq, LANES), lambda h, i: (i, 0)),
            _resident((s, hw), lambda h, i: (0, k_cb // hp + h)),
            _resident((s, LANES), lambda h, i: (0, 0)),
            _resident((s, hw), lambda h, i: (0, v_cb // hp + h)),
        ],
        out_specs=pl.BlockSpec((tq, hw), lambda h, i: (i, h)),
        out_shape=jax.ShapeDtypeStruct((s, n_heads * HEAD_DIM), BF16),
        scratch_shapes=_flash_state(hp, tq, tk),
        compiler_params=_cparams("arbitrary", "arbitrary"),
        name="fox_attn",
    )(proj, qx, proj, kx, proj)


def _gelu_tanh(x):
    return 0.5 * x * (1.0 + jnp.tanh(math.sqrt(2.0 / math.pi) * (x + 0.044715 * (x * x * x))))


def _nsa_compress_kernel(a_ref, w1_ref, pos_ref, w2_ref, o_ref):
    a = a_ref[...]
    half = a.shape[1]
    n = a.shape[0]
    lo = _dot(a, w1_ref[0:half, :])
    hi = _dot(a, w1_ref[half:, :])
    pos8 = jnp.broadcast_to(pos_ref[...], (8, pos_ref.shape[1]))
    pterm = _dot(pos8, w1_ref[...])[0:1]
    pre = lo + pltpu.roll(hi, n - 1, 0) + pterm
    o_ref[...] = _dot(_gelu_tanh(pre).astype(BF16), w2_ref[...]).astype(o_ref.dtype)


def _nsa_compress(a, w1, pos, w2):
    n, rows, width = a.shape
    return pl.pallas_call(
        _nsa_compress_kernel,
        grid=(n,),
        in_specs=[
            pl.BlockSpec((None, rows, width), lambda i: (i, 0, 0)),
            pl.BlockSpec((None, 2 * width, HEAD_DIM), lambda i: (i // NSA_KV_HEADS, 0, 0)),
            pl.BlockSpec((None, 1, 2 * width), lambda i: (i // NSA_KV_HEADS, 0, 0)),
            pl.BlockSpec((None, HEAD_DIM, HEAD_DIM), lambda i: (i // NSA_KV_HEADS, 0, 0)),
        ],
        out_specs=pl.BlockSpec((None, rows, HEAD_DIM), lambda i: (i, 0, 0)),
        out_shape=jax.ShapeDtypeStruct((n, rows, HEAD_DIM), BF16),
        compiler_params=_cparams("arbitrary"),
        name="nsa_compress",
    )(a, w1, pos, w2)


def _nsa_cmp_kernel(q_ref, kc_ref, vc_ref, ovt_ref, ocmp_ref, sel_ref):
    qi = pl.program_id(0)
    tq = q_ref.shape[0]
    ncp = kc_ref.shape[1]
    gw = GROUP * HEAD_DIM
    t = qi * tq + lax.broadcasted_iota(jnp.int32, (tq, ncp), 0)
    n = lax.broadcasted_iota(jnp.int32, (tq, ncp), 1)
    vis = (n * NSA_CMP_STRIDE + (NSA_CMP_LEN - 1)) <= t
    ones = jnp.ones((ncp, LANES), BF16)
    blk = lax.broadcasted_iota(jnp.int32, (LANES, tq), 0)
    blkf = blk.astype(F32)
    cur = (qi * tq + lax.broadcasted_iota(jnp.int32, (LANES, tq), 1)) >> SLC_SHIFT
    forced = (blk == 0) | (blk == cur) | (blk == cur - 1)
    future = blk > cur
    ovt = ovt_ref[...]

    for h in range(NSA_KV_HEADS):
        kc = kc_ref[h]
        va = jnp.concatenate([vc_ref[h], ones], axis=1)
        psum = jnp.zeros((tq, ncp), F32)
        for g in range(GROUP):
            cols = slice(h * gw + g * HEAD_DIM, h * gw + (g + 1) * HEAD_DIM)
            s = jnp.where(vis, _dot_t(q_ref[:, cols], kc), NEG_INF)
            chunks = [s[:, c * LANES:(c + 1) * LANES] for c in range(ncp // LANES)]
            m = jnp.max(functools.reduce(jnp.maximum, chunks), axis=1, keepdims=True)
            p = jnp.where(vis, jnp.exp(s - m), 0.0)
            pv = _dot(p.astype(BF16), va)
            inv = 1.0 / jnp.maximum(pv[:, HEAD_DIM:], 1e-30)
            ocmp_ref[:, cols] = pv[:, :HEAD_DIM] * inv
            psum = psum + p * jnp.concatenate([inv] * (ncp // LANES), axis=1)

        p_hi = psum.astype(BF16)
        p_lo = (psum - p_hi.astype(F32)).astype(BF16)
        imp = _dot_t(ovt, p_hi) + _dot_t(ovt, p_lo)
        imp = jnp.where(forced, jnp.inf, imp)
        imp = jnp.where(future, -jnp.inf, imp)
        sel = jnp.zeros(imp.shape, F32)
        for _ in range(NSA_SLC_TOPK):
            mx = jnp.max(imp, axis=0, keepdims=True)
            idx = jnp.min(jnp.where(imp == mx, blkf, float(LANES)), axis=0, keepdims=True)
            pick = blkf == idx
            sel = jnp.where(pick, 1.0, sel)
            imp = jnp.where(pick, -jnp.inf, imp)
        sel_ref[h] = jnp.where(sel > 0.5, 0.0, NEG_INF).T.astype(sel_ref.dtype)


def _nsa_cmp_select(proj, kv_c, overlap_t, q_cb):
    s = proj.shape[0]
    tq = BAND_BLOCK
    nq = NSA_KV_HEADS * GROUP
    ncp = kv_c.shape[1]
    assert q_cb % nq == 0 and tq == LANES
    return pl.pallas_call(
        _nsa_cmp_kernel,
        grid=(s // tq,),
        in_specs=[
            pl.BlockSpec((tq, nq * HEAD_DIM), lambda i: (i, q_cb // nq)),
            pl.BlockSpec((NSA_KV_HEADS, ncp, HEAD_DIM), lambda i: (0, 0, 0)),
            pl.BlockSpec((NSA_KV_HEADS, ncp, HEAD_DIM), lambda i: (1, 0, 0)),
            pl.BlockSpec((LANES, ncp), lambda i: (0, 0)),
        ],
        out_specs=[
            pl.BlockSpec((tq, nq * HEAD_DIM), lambda i: (i, 0)),
            pl.BlockSpec((NSA_KV_HEADS, tq, LANES), lambda i: (0, i, 0)),
        ],
        out_shape=[
            jax.ShapeDtypeStruct((s, nq * HEAD_DIM), F32),
            jax.ShapeDtypeStruct((NSA_KV_HEADS, s, LANES), BF16),
        ],
        compiler_params=_cparams("arbitrary"),
        name="nsa_cmp_select",
    )(proj, kv_c, kv_c, overlap_t)


def _nsa_slc_kernel(q_ref, k_ref, kx_ref, v_ref, sel_ref, o_ref, qa_ref, *state, tk):
    qi = pl.program_id(0)
    tq = q_ref.shape[0]
    rows = GROUP * tq
    gw = GROUP * HEAD_DIM
    for h in range(NSA_KV_HEADS):
        selb = sel_ref[h]
        for g in range(GROUP):
            qa_ref[h, g * tq:(g + 1) * tq, 0:HEAD_DIM] = q_ref[:, h * gw + g * HEAD_DIM:h * gw + (g + 1) * HEAD_DIM]
            qa_ref[h, g * tq:(g + 1) * tq, HEAD_DIM:] = selb
    qas = [qa_ref[h] for h in range(NSA_KV_HEADS)]

    last = jnp.right_shift(qi, _log2(tk // tq))
    t = qi * tq + (lax.broadcasted_iota(jnp.int32, (rows, tk), 0) & (tq - 1))
    kpos = last * tk + lax.broadcasted_iota(jnp.int32, (rows, tk), 1)
    outs = _flash_attention(qas, k_ref, kx_ref, v_ref, state, last, tk, kpos <= t)
    for h in range(NSA_KV_HEADS):
        for g in range(GROUP):
            o_ref[:, h * gw + g * HEAD_DIM:h * gw + (g + 1) * HEAD_DIM] = outs[h][g * tq:(g + 1) * tq, :]


def _nsa_selected(proj, selb, q_cb, k_cb, v_cb):
    s = proj.shape[0]
    tq = 2 * BAND_BLOCK
    tk = min(FLASH_TILE, s)
    nq = NSA_KV_HEADS * GROUP
    kvw = NSA_KV_HEADS * HEAD_DIM
    assert tq & (tq - 1) == 0 and s % tk == 0 and q_cb % nq == 0 and k_cb % NSA_KV_HEADS == v_cb % NSA_KV_HEADS == 0
    return pl.pallas_call(
        functools.partial(_nsa_slc_kernel, tk=tk),
        grid=(s // tq,),
        in_specs=[
            pl.BlockSpec((tq, nq * HEAD_DIM), lambda i: (i, q_cb // nq)),
            _resident((s, kvw), lambda i: (0, k_cb // NSA_KV_HEADS)),
            _resident((s, LANES), lambda i: (0, 0)),
            _resident((s, kvw), lambda i: (0, v_cb // NSA_KV_HEADS)),
            pl.BlockSpec((NSA_KV_HEADS, tq, LANES), lambda i: (0, i, 0)),
        ],
        out_specs=pl.BlockSpec((tq, nq * HEAD_DIM), lambda i: (i, 0)),
        out_shape=jax.ShapeDtypeStruct((s, nq * HEAD_DIM), F32),
        scratch_shapes=[pltpu.VMEM((NSA_KV_HEADS, GROUP * tq, 2 * HEAD_DIM), BF16)]
        + _flash_state(NSA_KV_HEADS, GROUP * tq, tk),
        compiler_params=_cparams("arbitrary"),
        name="nsa_selected",
    )(proj, proj, _block_onehot(s, NSA_SLC_BLOCK), proj, selb)


def _nsa_win_kernel(q_ref, k_ref, v_ref, gate_ref, ocmp_ref, oslc_ref, o_ref, qs_ref):
    qi = pl.program_id(0)
    tq = q_ref.shape[0]
    gw = GROUP * HEAD_DIM
    rows = GROUP * tq
    n_prev = -(-(NSA_WINDOW - 1) // tq)
    row = lax.broadcasted_iota(jnp.int32, (rows, tq), 0) & (tq - 1)
    col = lax.broadcasted_iota(jnp.int32, (rows, tq), 1)
    ones = jnp.ones((tq, LANES), BF16)
    for h in range(NSA_KV_HEADS):
        for g in range(GROUP):
            qs_ref[h, g * tq:(g + 1) * tq, :] = q_ref[:, h * gw + g * HEAD_DIM:h * gw + (g + 1) * HEAD_DIM]
        q = qs_ref[h]
        cols = slice(h * HEAD_DIM, (h + 1) * HEAD_DIM)
        scores, starts = [], []
        for b in range(n_prev + 1):
            kb = qi - b
            start = pl.multiple_of(jnp.maximum(kb, 0) * tq, tq)
            s = _dot_t(q, k_ref[pl.ds(start, tq), cols])
            dist = row - col + b * tq
            if b * tq - (tq - 1) < 0:
                s = jnp.where(dist >= 0, s, NEG_INF)
            if b * tq + (tq - 1) >= NSA_WINDOW:
                s = jnp.where(dist < NSA_WINDOW, s, NEG_INF)
            if b > 0:
                s = s + jnp.where(kb >= 0, 0.0, NEG_INF)
            scores.append(s)
            starts.append(start)
        m = jnp.max(functools.reduce(jnp.maximum, scores), axis=1, keepdims=True)
        pv = None
        for s, start in zip(scores, starts):
            va = jnp.concatenate([v_ref[pl.ds(start, tq), cols], ones], axis=1)
            term = _dot(jnp.exp(s - m).astype(BF16), va)
            pv = term if pv is None else pv + term
        o_win = pv[:, :HEAD_DIM] / pv[:, HEAD_DIM:]
        sig = jax.nn.sigmoid(gate_ref[h])
        for g in range(GROUP):
            oc = slice(h * gw + g * HEAD_DIM, h * gw + (g + 1) * HEAD_DIM)
            o = (sig[:, 3 * g:3 * g + 1] * ocmp_ref[:, oc]
                 + sig[:, 3 * g + 1:3 * g + 2] * oslc_ref[:, oc]
                 + sig[:, 3 * g + 2:3 * g + 3] * o_win[g * tq:(g + 1) * tq, :])
            o_ref[:, oc] = o.astype(o_ref.dtype)


def _nsa_window_merge(proj, gates, o_cmp, o_slc, q_cb, k_cb, v_cb):
    s = proj.shape[0]
    tq = BAND_BLOCK
    nq = NSA_KV_HEADS * GROUP
    kvw = NSA_KV_HEADS * HEAD_DIM
    assert tq & (tq - 1) == 0 and q_cb % nq == 0 and k_cb % NSA_KV_HEADS == v_cb % NSA_KV_HEADS == 0
    wide = pl.BlockSpec((tq, nq * HEAD_DIM), lambda i: (i, 0))
    return pl.pallas_call(
        _nsa_win_kernel,
        grid=(s // tq,),
        in_specs=[
            pl.BlockSpec((tq, nq * HEAD_DIM), lambda i: (i, q_cb // nq)),
            _resident((s, kvw), lambda i: (0, k_cb // NSA_KV_HEADS)),
            _resident((s, kvw), lambda i: (0, v_cb // NSA_KV_HEADS)),
            pl.BlockSpec((NSA_KV_HEADS, tq, LANES), lambda i: (0, i, 0)),
            wide,
            wide,
        ],
        out_specs=wide,
        out_shape=jax.ShapeDtypeStruct((s, nq * HEAD_DIM), BF16),
        scratch_shapes=[pltpu.VMEM((NSA_KV_HEADS, GROUP * tq, HEAD_DIM), BF16)],
        compiler_params=_cparams("arbitrary"),
        name="nsa_window_merge",
    )(proj, proj, proj, gates, o_cmp, o_slc)


def _overlap_matrix(seq):
    n_pad = seq // NSA_CMP_STRIDE
    n_cmp = (seq - NSA_CMP_LEN) // NSA_CMP_STRIDE + 1
    c_start = np.arange(n_pad)[:, None] * NSA_CMP_STRIDE
    s_start = np.arange(LANES)[None, :] * NSA_SLC_BLOCK
    ov = (c_start < s_start + NSA_SLC_BLOCK) & (c_start + NSA_CMP_LEN > s_start)
    ov &= (np.arange(n_pad)[:, None] < n_cmp) & (np.arange(LANES)[None, :] < seq // NSA_SLC_BLOCK)
    return jnp.asarray(ov.T.astype(np.float32), dtype=BF16)


def _ctypes(spec):
    return jnp.asarray(np.concatenate([np.full(n, kind, np.int32) for n, kind in spec]))


def _even_layer(x, mods, norm_g, w_in, sinks, w_out, layer, tables):
    sh1, sc1, g1 = mods
    heads = w_in.shape[2] // HEAD_DIM
    n_moba = 8
    ctypes = _ctypes([(n_moba, CT_ROPE_SCALE), (n_moba, CT_ROPE), (n_moba, CT_PLAIN),
                      (8, CT_ROPE_SCALE), (SWA_KV_HEADS, CT_ROPE), (SWA_KV_HEADS, CT_PLAIN)])
    assert ctypes.shape[0] == heads
    proj = _norm_proj(x, norm_g, sc1, sh1, w_in, ctypes, tables, 512, layer=layer)
    nb = x.shape[0] // MOBA_BLOCK
    kmean = _moba_kmean(proj, 1, n_moba * HEAD_DIM)
    kmean = jnp.pad(kmean, ((0, LANES - nb), (0, 0))).astype(BF16)
    oa = _moba_attention(proj, kmean, n_moba, 0, n_moba, 2 * n_moba)
    ob = _swa_attention(proj, sinks, 3 * n_moba, 4 * n_moba, 4 * n_moba + SWA_KV_HEADS)
    return _out_proj(oa, ob, w_out, layer, x, g1)


def _odd_layer(x, mods, norm_g, w_in, forget_b, k_pos, k_w1, k_w2, v_pos, v_w1, v_w2, w_out, layer, tables, overlap):
    sh1, sc1, g1 = mods
    s = x.shape[0]
    n_fox = 8
    hw = n_fox * HEAD_DIM
    kvw = NSA_KV_HEADS * HEAD_DIM
    o_fc = 3 * hw
    o_qd = o_fc + n_fox
    o_gd = o_qd + hw + 6 * kvw
    w_main = jnp.concatenate([w_in[:, :o_fc], w_in[:, o_qd:o_gd]], axis=1).astype(BF16)
    n_aux = n_fox + 3 * 8
    w_aux = jnp.concatenate([w_in[:, o_fc:o_qd], w_in[:, o_gd:], jnp.zeros((w_in.shape[0], LANES - n_aux), F32)],
                            axis=1).astype(BF16)
    ctypes = _ctypes([(8, CT_SCALE), (8, CT_PLAIN), (8, CT_PLAIN), (8, CT_ROPE_SCALE),
                      (2, CT_ROPE), (2, CT_PLAIN), (2, CT_ROPE), (2, CT_PLAIN), (2, CT_ROPE), (2, CT_PLAIN)])
    proj, aux = _norm_proj(x, norm_g, sc1, sh1, w_main, ctypes, tables, 512, w_aux=w_aux)

    bias_row = jnp.pad(forget_b.astype(F32), (0, LANES - n_fox)).reshape(1, LANES)
    fox_qx, fox_kx = _fox_cum(aux, bias_row, n_fox)
    oc = _fox_attention(proj, fox_qx, fox_kx, n_fox, 0, 8, 16)

    c0 = 32
    cmp_in = proj[:, c0 * HEAD_DIM:(c0 + 4) * HEAD_DIM].reshape(s, 4, HEAD_DIM).transpose(1, 0, 2)
    cmp_in = cmp_in.reshape(4, s // NSA_CMP_STRIDE, NSA_CMP_STRIDE * HEAD_DIM)
    kv_c = _nsa_compress(
        cmp_in,
        jnp.stack([k_w1, v_w1]).astype(BF16),
        jnp.stack([k_pos.reshape(1, -1), v_pos.reshape(1, -1)]).astype(BF16),
        jnp.stack([k_w2, v_w2]).astype(BF16))
    o_cmp, sel = _nsa_cmp_select(proj, kv_c, overlap, 24)
    o_slc = _nsa_selected(proj, sel, 24, c0 + 4, c0 + 6)
    gates = aux[:, n_fox:n_aux].reshape(s, NSA_KV_HEADS, 3 * GROUP).transpose(1, 0, 2)
    gates = jnp.pad(gates, ((0, 0), (0, 0), (0, LANES - 3 * GROUP)))
    od = _nsa_window_merge(proj, gates, o_cmp, o_slc, 24, c0 + 8, c0 + 10)
    return _out_proj(oc, od, w_out, layer, x, g1)


def kernel(x, c, norm_mix_g, norm_mlp_g, ada_w, ada_b, mlp_up, mlp_down, even_w_in, even_sinks, even_w_out, odd_w_in, fox_forget_b, nsa_k_pos, nsa_k_w1, nsa_k_w2, nsa_v_pos, nsa_v_w1, nsa_v_w2, odd_w_out, final_norm_g):
    batch, seq, d = x.shape
    assert batch == 1
    depth = ada_w.shape[0]
    tables = _epilogue_tables(seq)
    overlap = _overlap_matrix(seq)
    mod = _ada_mod(c, ada_w, ada_b).reshape(depth, 6, 1, d)
    xs = x[0]
    for i in range(depth):
        sh1, sc1, g1, sh2, sc2, g2 = [mod[i, t] for t in range(6)]
        ng = norm_mix_g[i].reshape(1, d)
        j = i // 2
        if i % 2 == 0:
            xs = _even_layer(xs, (sh1, sc1, g1), ng, even_w_in, even_sinks[j], even_w_out, j, tables)
        else:
            xs = _odd_layer(xs, (sh1, sc1, g1), ng, odd_w_in[j], fox_forget_b[j], nsa_k_pos[j], nsa_k_w1[j],
                            nsa_k_w2[j], nsa_v_pos[j], nsa_v_w1[j], nsa_v_w2[j], odd_w_out, j, tables, overlap)
        xs = _mlp(xs, norm_mlp_g[i].reshape(1, d), sc2, sh2, mlp_up, mlp_down, i, g2,
                  final_norm_g.reshape(1, d), final=(i == depth - 1))
    return xs[None]
```

```python
import functools
import math

import numpy as np
import jax
import jax.numpy as jnp
from jax import lax
from jax.experimental import pallas as pl
from jax.experimental.pallas import tpu as pltpu

HEAD_DIM = 128
ROPE_THETA = 10000.0
NORM_EPS = 1e-6
MOBA_BLOCK = 256
MOBA_TOPK = 3
SWA_KV_HEADS = 2
SWA_WINDOW = 128
NSA_KV_HEADS = 2
NSA_CMP_LEN = 32
NSA_CMP_STRIDE = 16
NSA_SLC_BLOCK = 64
SLC_SHIFT = 6
NSA_SLC_TOPK = 16
NSA_WINDOW = 512
BAND_BLOCK = 128
NEG_INF = -1e30
LOG2E = math.log2(math.e)
SCALE = HEAD_DIM ** -0.5 * LOG2E
LANES = 128
GROUP = 4
VMEM_LIMIT = 48 * 1024 * 1024

F32 = jnp.float32
BF16 = jnp.bfloat16

CT_PLAIN, CT_ROPE, CT_ROPE_SCALE, CT_SCALE = 0, 1, 2, 3


VMEM_LIMIT_MLP = 56 * 1024 * 1024


def _cparams(*sem, vmem=VMEM_LIMIT):
    return pltpu.CompilerParams(dimension_semantics=sem, vmem_limit_bytes=vmem)


def _dot(a, b):
    return jnp.dot(a, b, preferred_element_type=F32)


def _dot_t(a, b):
    return lax.dot_general(a, b, (((1,), (1,)), ((), ())), preferred_element_type=F32)


def _ada_kernel(c_ref, w_ref, b_ref, o_ref):
    c = c_ref[...]
    cond = c * jax.nn.sigmoid(c)
    cond8 = jnp.broadcast_to(cond, (8, cond.shape[1])).astype(BF16)
    y = _dot(cond8, w_ref[...].astype(BF16))
    o_ref[...] = y[0:1] + b_ref[...]


def _ada_mod(c, ada_w, ada_b):
    depth, d, n = ada_w.shape
    tn = 1024
    return pl.pallas_call(
        _ada_kernel,
        grid=(depth, n // tn),
        in_specs=[
            pl.BlockSpec((1, d), lambda l, j: (0, 0)),
            pl.BlockSpec((None, d, tn), lambda l, j: (l, 0, j)),
            pl.BlockSpec((None, 1, tn), lambda l, j: (l, 0, j)),
        ],
        out_specs=pl.BlockSpec((None, 1, tn), lambda l, j: (l, 0, j)),
        out_shape=jax.ShapeDtypeStruct((depth, 1, n), F32),
        compiler_params=_cparams("arbitrary", "arbitrary"),
        name="ada_mod",
    )(c, ada_w, ada_b.reshape(depth, 1, n))


def _norm_mod(x, g, sc, sh):
    y = x * lax.rsqrt(jnp.mean(x * x, axis=-1, keepdims=True) + NORM_EPS)
    return (y * g) * (1.0 + sc) + sh


PROJ_ROW_CHUNKS = 2


def _proj_kernel(ct_ref, x_ref, g_ref, sc_ref, sh_ref, w_ref, ta_ref, tb_ref, *rest, aux):
    if aux:
        wa_ref, o_ref, oa_ref, h_ref = rest
    else:
        o_ref, h_ref = rest
    j = pl.program_id(1)

    @pl.when(j == 0)
    def _():
        h_ref[...] = _norm_mod(x_ref[...], g_ref[...], sc_ref[...], sh_ref[...]).astype(BF16)
        if aux:
            oa_ref[...] = _dot(h_ref[...], wa_ref[...])

    w = w_ref[...].astype(BF16)
    groups = w.shape[1] // LANES
    tm = h_ref.shape[0]
    chunk = tm // PROJ_ROW_CHUNKS
    for ri in range(PROJ_ROW_CHUNKS):
        rows = slice(ri * chunk, (ri + 1) * chunk)
        y = _dot(h_ref[rows, :], w)
        for gi in range(groups):
            lo, hi = gi * LANES, (gi + 1) * LANES
            yg = y[:, lo:hi]
            ct = ct_ref[j * groups + gi]
            r = yg * ta_ref[ct, rows, :] + pltpu.roll(yg, HEAD_DIM // 2, 1) * tb_ref[ct, rows, :]
            o_ref[rows, lo:hi] = r.astype(o_ref.dtype)


def _epilogue_tables(seq):
    f32 = np.float32
    inv = (f32(1.0) / (f32(ROPE_THETA) ** (np.arange(0, HEAD_DIM, 2, dtype=f32) / f32(HEAD_DIM)))).astype(f32)
    ang = np.arange(seq, dtype=f32)[:, None] * inv[None, :]
    cos, sin = np.cos(ang), np.sin(ang)
    cosf = np.concatenate([cos, cos], axis=-1)
    sinf = np.concatenate([-sin, sin], axis=-1)
    one, zero = np.ones_like(cosf), np.zeros_like(cosf)
    by_kind = {CT_PLAIN: (one, zero), CT_ROPE: (cosf, sinf), CT_ROPE_SCALE: (cosf * f32(SCALE), sinf * f32(SCALE)),
               CT_SCALE: (one * f32(SCALE), zero)}
    return (jnp.asarray(np.stack([by_kind[k][0] for k in range(4)])),
            jnp.asarray(np.stack([by_kind[k][1] for k in range(4)])))


def _norm_proj(x, g, sc, sh, w, ctypes, tables, tn, layer=None, w_aux=None):
    s, d = x.shape
    n = w.shape[-1]
    tm = min(1024, s)
    kinds = tables[0].shape[0]
    if layer is None:
        w_spec = pl.BlockSpec((d, tn), lambda i, j, ct: (0, j))
    else:
        w_spec = pl.BlockSpec((None, d, tn), lambda i, j, ct: (layer, 0, j))
    aux = w_aux is not None
    in_specs = [
        pl.BlockSpec((tm, d), lambda i, j, ct: (i, 0)),
        pl.BlockSpec((1, d), lambda i, j, ct: (0, 0)),
        pl.BlockSpec((1, d), lambda i, j, ct: (0, 0)),
        pl.BlockSpec((1, d), lambda i, j, ct: (0, 0)),
        w_spec,
        pl.BlockSpec((kinds, tm, LANES), lambda i, j, ct: (0, i, 0)),
        pl.BlockSpec((kinds, tm, LANES), lambda i, j, ct: (0, i, 0)),
    ]
    out_specs = pl.BlockSpec((tm, tn), lambda i, j, ct: (i, j))
    out_shape = jax.ShapeDtypeStruct((s, n), BF16)
    operands = [ctypes, x, g, sc, sh, w, *tables]
    if aux:
        in_specs.append(pl.BlockSpec((d, LANES), lambda i, j, ct: (0, 0)))
        out_specs = [out_specs, pl.BlockSpec((tm, LANES), lambda i, j, ct: (i, 0))]
        out_shape = [out_shape, jax.ShapeDtypeStruct((s, LANES), F32)]
        operands.append(w_aux)
    grid_spec = pltpu.PrefetchScalarGridSpec(
        num_scalar_prefetch=1,
        grid=(s // tm, n // tn),
        in_specs=in_specs,
        out_specs=out_specs,
        scratch_shapes=[pltpu.VMEM((tm, d), BF16)],
    )
    return pl.pallas_call(
        functools.partial(_proj_kernel, aux=aux),
        grid_spec=grid_spec,
        out_shape=out_shape,
        compiler_params=_cparams("arbitrary", "arbitrary"),
        name="norm_proj",
    )(*operands)


def _outproj_kernel(a_ref, b_ref, w_ref, x_ref, g_ref, o_ref):
    ha = a_ref.shape[1]
    y = _dot(a_ref[...], w_ref[0:ha, :].astype(BF16)) + _dot(b_ref[...], w_ref[ha:, :].astype(BF16))
    o_ref[...] = x_ref[...] + g_ref[...] * y


def _out_proj(oa, ob, w, layer, x, gate):
    s, d = x.shape
    ha, hb = oa.shape[1], ob.shape[1]
    tm, tn = min(1024, s), 512
    return pl.pallas_call(
        _outproj_kernel,
        grid=(s // tm, d // tn),
        in_specs=[
            pl.BlockSpec((tm, ha), lambda i, j: (i, 0)),
            pl.BlockSpec((tm, hb), lambda i, j: (i, 0)),
            pl.BlockSpec((None, ha + hb, tn), lambda i, j: (layer, 0, j)),
            pl.BlockSpec((tm, tn), lambda i, j: (i, j)),
            pl.BlockSpec((1, tn), lambda i, j: (0, j)),
        ],
        out_specs=pl.BlockSpec((tm, tn), lambda i, j: (i, j)),
        out_shape=jax.ShapeDtypeStruct((s, d), F32),
        compiler_params=_cparams("arbitrary", "arbitrary"),
        name="out_proj",
    )(oa, ob, w, x, gate)


def _rms_norm(x, g):
    return x * lax.rsqrt(jnp.mean(x * x, axis=-1, keepdims=True) + NORM_EPS) * g


def _mlp_kernel(x_ref, g_ref, sc_ref, sh_ref, up_ref, down_ref, gate_ref, fg_ref, o_ref, h_ref, *, final):
    f = pl.program_id(1)

    @pl.when(f == 0)
    def _():
        h_ref[...] = _norm_mod(x_ref[...], g_ref[...], sc_ref[...], sh_ref[...]).astype(BF16)
        o_ref[...] = jnp.zeros_like(o_ref)

    hid = jnp.maximum(_dot(h_ref[...], up_ref[...].astype(BF16)), 0.0)
    o_ref[...] += _dot((hid * hid).astype(BF16), down_ref[...].astype(BF16))

    @pl.when(f == pl.num_programs(1) - 1)
    def _():
        y = x_ref[...] + gate_ref[...] * o_ref[...]
        o_ref[...] = _rms_norm(y, fg_ref[...]) if final else y


def _mlp(x, g, sc, sh, up, down, layer, gate, final_g, final):
    s, d = x.shape
    ff = up.shape[2]
    tm, tf = min(1024, s), 512
    return pl.pallas_call(
        functools.partial(_mlp_kernel, final=final),
        grid=(s // tm, ff // tf),
        in_specs=[
            pl.BlockSpec((tm, d), lambda i, f: (i, 0)),
            pl.BlockSpec((1, d), lambda i, f: (0, 0)),
            pl.BlockSpec((1, d), lambda i, f: (0, 0)),
            pl.BlockSpec((1, d), lambda i, f: (0, 0)),
            pl.BlockSpec((None, d, tf), lambda i, f: (layer, 0, f)),
            pl.BlockSpec((None, tf, d), lambda i, f: (layer, f, 0)),
            pl.BlockSpec((1, d), lambda i, f: (0, 0)),
            pl.BlockSpec((1, d), lambda i, f: (0, 0)),
        ],
        out_specs=pl.BlockSpec((tm, d), lambda i, f: (i, 0), pipeline_mode=pl.Buffered(1)),
        out_shape=jax.ShapeDtypeStruct((s, d), F32),
        scratch_shapes=[pltpu.VMEM((tm, d), BF16)],
        compiler_params=_cparams("arbitrary", "arbitrary", vmem=VMEM_LIMIT_MLP),
        name="mlp",
    )(x, g, sc, sh, up, down, gate, final_g)


def _rows(ref, start, size):
    return ref[pl.ds(pl.multiple_of(start, size), size), :]


def _log2(n):
    assert n > 0 and n & (n - 1) == 0
    return n.bit_length() - 1


FLASH_TILE = 512
HEADS_PER_STEP = 4


def _resident(shape, index_map):
    return pl.BlockSpec(shape, index_map, pipeline_mode=pl.Buffered(1))


def _flash_attention(qas, k_ref, kx_ref, v_ref, state, last, tk, diag_mask):
    m_ref, alpha_ref, p_ref, acc_ref = state
    chains = range(len(qas))
    ones = jnp.ones((tk, LANES), BF16)

    def score(start, mask, first):
        rows = pl.ds(pl.multiple_of(start, tk), tk)
        for c in chains:
            ka = jnp.concatenate([k_ref[rows, c * HEAD_DIM:(c + 1) * HEAD_DIM], kx_ref[rows, :]], axis=1)
            s = _dot_t(qas[c], ka)
            if mask is not None:
                s = jnp.where(mask, s, NEG_INF)
            chunks = [s[:, i * LANES:(i + 1) * LANES] for i in range(tk // LANES)]
            m_cur = jnp.max(functools.reduce(jnp.maximum, chunks), axis=1, keepdims=True)
            if first:
                m_new = jnp.broadcast_to(m_cur, (s.shape[0], LANES))
            else:
                m_prev = m_ref[c]
                m_new = jnp.maximum(m_prev, m_cur)
                alpha_ref[c] = jnp.exp2(m_prev - m_new)
            p_ref[c] = jnp.concatenate([jnp.exp2(ch - m_new) for ch in chunks], axis=1).astype(BF16)
            m_ref[c] = m_new

    def accumulate(start):
        rows = pl.ds(pl.multiple_of(start, tk), tk)
        for c in chains:
            va = jnp.concatenate([v_ref[rows, c * HEAD_DIM:(c + 1) * HEAD_DIM], ones], axis=1)
            alpha = alpha_ref[c]
            acc_ref[c] = jnp.concatenate([alpha, alpha], axis=1) * acc_ref[c] + _dot(p_ref[c], va)

    for c in chains:
        acc_ref[c] = jnp.zeros(acc_ref.shape[1:], F32)
        alpha_ref[c] = jnp.ones(alpha_ref.shape[1:], F32)
    score(last * tk, diag_mask, True)

    def body(p, carry):
        accumulate(jnp.where(p == 0, last, p - 1) * tk)
        score(p * tk, None, False)
        return carry

    lax.fori_loop(0, last, body, 0)
    accumulate(jnp.where(last == 0, last, last - 1) * tk)
    return [acc_ref[c][:, :HEAD_DIM] / acc_ref[c][:, HEAD_DIM:] for c in chains]


def _flash_state(chains, rows, tk):
    return [pltpu.VMEM((chains, rows, LANES), F32), pltpu.VMEM((chains, rows, LANES), F32),
            pltpu.VMEM((chains, rows, tk), BF16), pltpu.VMEM((chains, rows, 2 * HEAD_DIM), F32)]


def _block_onehot(seq, block):
    return jnp.asarray((np.arange(seq)[:, None] // block == np.arange(LANES)[None, :]).astype(np.float32), dtype=BF16)


def _kmean_kernel(k_ref, o_ref):
    k = k_ref[...].astype(F32)
    o_ref[...] = jnp.sum(k, axis=0, keepdims=True) * (1.0 / k.shape[0])


def _moba_kmean(proj, col_block, width):
    s = proj.shape[0]
    nb = s // MOBA_BLOCK
    out = pl.pallas_call(
        _kmean_kernel,
        grid=(nb,),
        in_specs=[pl.BlockSpec((MOBA_BLOCK, width), lambda b: (b, col_block))],
        out_specs=pl.BlockSpec((None, 1, width), lambda b: (b, 0, 0)),
        out_shape=jax.ShapeDtypeStruct((nb, 1, width), F32),
        compiler_params=_cparams("arbitrary"),
        name="moba_kmean",
    )(proj)
    return out.reshape(nb, width)


def _moba_kernel(q_ref, k_ref, kx_ref, v_ref, km_ref, o_ref, *state, tk, hp):
    qi = pl.program_id(1)
    tq = q_ref.shape[0]
    blk = lax.broadcasted_iota(jnp.int32, (LANES, tq), 0)
    blkf = blk.astype(F32)
    own = jnp.right_shift(qi * tq + lax.broadcasted_iota(jnp.int32, (LANES, tq), 1), _log2(MOBA_BLOCK))
    ownf = own.astype(F32)

    qas = []
    for hh in range(hp):
        cols = slice(hh * HEAD_DIM, (hh + 1) * HEAD_DIM)
        q = q_ref[:, cols]
        g = jnp.where(blk < own, _dot_t(km_ref[:, cols], q), NEG_INF)
        sel = jnp.zeros((LANES, tq), F32)
        for _ in range(MOBA_TOPK):
            mx = jnp.max(g, axis=0, keepdims=True)
            idx = jnp.min(jnp.where(g == mx, blkf, float(LANES)), axis=0, keepdims=True)
            pick = blkf == idx
            sel = jnp.where(pick & (idx < ownf), 1.0, sel)
            g = jnp.where(pick, -jnp.inf, g)
        qx = jnp.where((sel > 0.5) | (blk == own), 0.0, NEG_INF).T.astype(BF16)
        qas.append(jnp.concatenate([q, qx], axis=1))

    last = jnp.right_shift(qi, _log2(tk // tq))
    t = qi * tq + lax.broadcasted_iota(jnp.int32, (tq, tk), 0)
    kpos = last * tk + lax.broadcasted_iota(jnp.int32, (tq, tk), 1)
    outs = _flash_attention(qas, k_ref, kx_ref, v_ref, state, last, tk, kpos <= t)
    for hh in range(hp):
        o_ref[:, hh * HEAD_DIM:(hh + 1) * HEAD_DIM] = outs[hh].astype(o_ref.dtype)


def _moba_attention(proj, kmean, n_heads, q_cb, k_cb, v_cb):
    s = proj.shape[0]
    tq = tk = min(FLASH_TILE, s)
    hp = HEADS_PER_STEP
    hw = hp * HEAD_DIM
    assert s % tk == 0 and tk % MOBA_BLOCK == 0 and n_heads % hp == 0 and q_cb % hp == k_cb % hp == v_cb % hp == 0
    return pl.pallas_call(
        functools.partial(_moba_kernel, tk=tk, hp=hp),
        grid=(n_heads // hp, s // tq),
        in_specs=[
            pl.BlockSpec((tq, hw), lambda h, i: (i, q_cb // hp + h)),
            _resident((s, hw), lambda h, i: (0, k_cb // hp + h)),
            _resident((s, LANES), lambda h, i: (0, 0)),
            _resident((s, hw), lambda h, i: (0, v_cb // hp + h)),
            pl.BlockSpec((LANES, hw), lambda h, i: (0, h)),
        ],
        out_specs=pl.BlockSpec((tq, hw), lambda h, i: (i, h)),
        out_shape=jax.ShapeDtypeStruct((s, n_heads * HEAD_DIM), BF16),
        scratch_shapes=_flash_state(hp, tq, tk),
        compiler_params=_cparams("arbitrary", "arbitrary"),
        name="moba_attn",
    )(proj, proj, _block_onehot(s, MOBA_BLOCK), proj, kmean)


def _swa_kernel(sink_ref, q_ref, k_ref, v_ref, o_ref, qs_ref):
    qi = pl.program_id(0)
    tq = q_ref.shape[0]
    tk = 2 * tq
    rows = GROUP * tq
    gw = GROUP * HEAD_DIM
    start = pl.multiple_of(jnp.maximum(qi - 1, 0) * tq, tq)
    t = qi * tq + (lax.broadcasted_iota(jnp.int32, (rows, tk), 0) & (tq - 1))
    kpos = start + lax.broadcasted_iota(jnp.int32, (rows, tk), 1)
    ok = (kpos <= t) & (kpos > t - SWA_WINDOW)
    ones = jnp.ones((tk, LANES), BF16)
    for h in range(SWA_KV_HEADS):
        for g in range(GROUP):
            qs_ref[h, g * tq:(g + 1) * tq, :] = q_ref[:, h * gw + g * HEAD_DIM:h * gw + (g + 1) * HEAD_DIM]
        cols = slice(h * HEAD_DIM, (h + 1) * HEAD_DIM)
        s = jnp.where(ok, _dot_t(qs_ref[h], k_ref[pl.ds(start, tk), cols]), NEG_INF)
        sink = jnp.concatenate([jnp.full((tq, 1), sink_ref[h * GROUP + g] * LOG2E, F32) for g in range(GROUP)],
                               axis=0)
        m = jnp.maximum(jnp.max(jnp.maximum(s[:, :tq], s[:, tq:]), axis=1, keepdims=True), sink)
        p = jnp.exp2(s - m)
        pv = _dot(p.astype(BF16), jnp.concatenate([v_ref[pl.ds(start, tk), cols], ones], axis=1))
        o = pv[:, :HEAD_DIM] / (pv[:, HEAD_DIM:] + jnp.exp2(sink - m))
        for g in range(GROUP):
            o_ref[:, h * gw + g * HEAD_DIM:h * gw + (g + 1) * HEAD_DIM] = o[g * tq:(g + 1) * tq, :].astype(o_ref.dtype)


def _swa_attention(proj, sinks, q_cb, k_cb, v_cb):
    s = proj.shape[0]
    tq = BAND_BLOCK
    nq = SWA_KV_HEADS * GROUP
    kvw = SWA_KV_HEADS * HEAD_DIM
    assert SWA_WINDOW <= tq and tq == LANES and q_cb % nq == 0 and k_cb % SWA_KV_HEADS == v_cb % SWA_KV_HEADS == 0
    return pl.pallas_call(
        _swa_kernel,
        grid=(s // tq,),
        in_specs=[
            pl.BlockSpec(memory_space=pltpu.SMEM),
            pl.BlockSpec((tq, nq * HEAD_DIM), lambda i: (i, q_cb // nq)),
            _resident((s, kvw), lambda i: (0, k_cb // SWA_KV_HEADS)),
            _resident((s, kvw), lambda i: (0, v_cb // SWA_KV_HEADS)),
        ],
        out_specs=pl.BlockSpec((tq, nq * HEAD_DIM), lambda i: (i, 0)),
        out_shape=jax.ShapeDtypeStruct((s, nq * HEAD_DIM), BF16),
        scratch_shapes=[pltpu.VMEM((SWA_KV_HEADS, GROUP * tq, HEAD_DIM), BF16)],
        compiler_params=_cparams("arbitrary"),
        name="swa_attn",
    )(sinks, proj, proj, proj)


FOX_LANES = 16


def _split3(x):
    x1 = x.astype(BF16)
    r1 = x - x1.astype(F32)
    x2 = r1.astype(BF16)
    return x1, x2, (r1 - x2.astype(F32)).astype(BF16)


def _fox_tables(n_heads):
    place = np.zeros((6, LANES, LANES), np.float32)
    ones_q = np.zeros((1, LANES), np.float32)
    ones_k = np.zeros((1, LANES), np.float32)
    for h in range(n_heads):
        for i in range(6):
            place[i, h, FOX_LANES * h + i] = 1.0
        ones_q[0, FOX_LANES * h + 3:FOX_LANES * h + 6] = 1.0
        ones_k[0, FOX_LANES * h:FOX_LANES * h + 3] = 1.0
    return jnp.asarray(place, dtype=BF16), jnp.asarray(ones_q), jnp.asarray(ones_k)


def _fox_cum_kernel(aux_ref, b_ref, place_ref, oq_ref, ok_ref, qx_ref, kx_ref, carry_ref):
    i = pl.program_id(0)

    @pl.when(i == 0)
    def _():
        carry_ref[...] = jnp.zeros_like(carry_ref)

    x = aux_ref[...] + b_ref[...]
    log_f = -(jnp.maximum(-x, 0.0) + jnp.log(1.0 + jnp.exp(-jnp.abs(x))))
    tb = x.shape[0]
    r = lax.broadcasted_iota(jnp.int32, (tb, tb), 0)
    c = lax.broadcasted_iota(jnp.int32, (tb, tb), 1)
    tri = jnp.where(c <= r, 1.0, 0.0).astype(BF16)
    x1, x2, x3 = _split3(log_f)
    cum = _dot(tri, x1) + _dot(tri, x2) + _dot(tri, x3) + carry_ref[...]
    carry_ref[...] = cum[tb - 1:tb, :]
    c1, c2, c3 = _split3(cum * LOG2E)
    qx = _dot(c1, place_ref[0]) + _dot(c2, place_ref[1]) + _dot(c3, place_ref[2]) + oq_ref[...]
    kx = ok_ref[...] - (_dot(c1, place_ref[3]) + _dot(c2, place_ref[4]) + _dot(c3, place_ref[5]))
    qx_ref[...] = qx.astype(BF16)
    kx_ref[...] = kx.astype(BF16)


def _fox_cum(aux, bias_row, n_heads):
    s, w = aux.shape
    tb = min(512, s)
    place, ones_q, ones_k = _fox_tables(n_heads)
    full = lambda shape: pl.BlockSpec(shape, lambda i: (0,) * len(shape))
    return pl.pallas_call(
        _fox_cum_kernel,
        grid=(s // tb,),
        in_specs=[pl.BlockSpec((tb, w), lambda i: (i, 0)), full((1, w)), full(place.shape), full((1, w)), full((1, w))],
        out_specs=[pl.BlockSpec((tb, w), lambda i: (i, 0)), pl.BlockSpec((tb, w), lambda i: (i, 0))],
        out_shape=[jax.ShapeDtypeStruct((s, w), BF16), jax.ShapeDtypeStruct((s, w), BF16)],
        scratch_shapes=[pltpu.VMEM((1, w), F32)],
        compiler_params=_cparams("arbitrary"),
        name="fox_cum",
    )(aux, bias_row, place, ones_q, ones_k)


def _fox_kernel(q_ref, qx_ref, k_ref, kx_ref, v_ref, o_ref, *state, tk, hp):
    hb = pl.program_id(0)
    qi = pl.program_id(1)
    tq = q_ref.shape[0]
    lane = lax.broadcasted_iota(jnp.int32, (tq, LANES), 1)
    qxf = qx_ref[...].astype(F32)
    qas = []
    for hh in range(hp):
        mine = jnp.right_shift(lane, 4) == hb * hp + hh
        qx = jnp.where(mine, qxf, 0.0).astype(BF16)
        qas.append(jnp.concatenate([q_ref[:, hh * HEAD_DIM:(hh + 1) * HEAD_DIM], qx], axis=1))

    last = jnp.right_shift(qi, _log2(tk // tq))
    t = qi * tq + lax.broadcasted_iota(jnp.int32, (tq, tk), 0)
    kpos = last * tk + lax.broadcasted_iota(jnp.int32, (tq, tk), 1)
    outs = _flash_attention(qas, k_ref, kx_ref, v_ref, state, last, tk, kpos <= t)
    for hh in range(hp):
        o_ref[:, hh * HEAD_DIM:(hh + 1) * HEAD_DIM] = outs[hh].astype(o_ref.dtype)


def _fox_attention(proj, qx, kx, n_heads, q_cb, k_cb, v_cb):
    s = proj.shape[0]
    tq = tk = min(FLASH_TILE, s)
    hp = HEADS_PER_STEP
    hw = hp * HEAD_DIM
    assert s % tk == 0 and FOX_LANES == 16 and n_heads % hp == 0 and q_cb % hp == k_cb % hp == v_cb % hp == 0
    return pl.pallas_call(
        functools.partial(_fox_kernel, tk=tk, hp=hp),
        grid=(n_heads // hp, s // tq),
        in_specs=[
            pl.BlockSpec((tq, hw), lambda h, i: (i, q_cb // hp + h)),
            pl.BlockSpec((tq, LANES), lambda h, i: (i, 0)),
            _resident((s, hw), lambda h, i: (0, k_cb // hp + h)),
            _resident((s, LANES), lambda h, i: (0, 0)),
            _resident((s, hw), lambda h, i: (0, v_cb // hp + h)),
        ],
        out_specs=pl.BlockSpec((tq, hw), lambda h, i: (i, h)),
        out_shape=jax.ShapeDtypeStruct((s, n_heads * HEAD_DIM), BF16),
        scratch_shapes=_flash_state(hp, tq, tk),
        compiler_params=_cparams("arbitrary", "arbitrary"),
        name="fox_attn",
    )(proj, qx, proj, kx, proj)


def _gelu_tanh(x):
    return 0.5 * x * (1.0 + jnp.tanh(math.sqrt(2.0 / math.pi) * (x + 0.044715 * (x * x * x))))


def _nsa_compress_kernel(a_ref, w1_ref, pos_ref, w2_ref, o_ref):
    a = a_ref[...]
    half = a.shape[1]
    n = a.shape[0]
    lo = _dot(a, w1_ref[0:half, :])
    hi = _dot(a, w1_ref[half:, :])
    pos8 = jnp.broadcast_to(pos_ref[...], (8, pos_ref.shape[1]))
    pterm = _dot(pos8, w1_ref[...])[0:1]
    pre = lo + pltpu.roll(hi, n - 1, 0) + pterm
    o_ref[...] = _dot(_gelu_tanh(pre).astype(BF16), w2_ref[...]).astype(o_ref.dtype)


def _nsa_compress(a, w1, pos, w2):
    n, rows, width = a.shape
    return pl.pallas_call(
        _nsa_compress_kernel,
        grid=(n,),
        in_specs=[
            pl.BlockSpec((None, rows, width), lambda i: (i, 0, 0)),
            pl.BlockSpec((None, 2 * width, HEAD_DIM), lambda i: (i // NSA_KV_HEADS, 0, 0)),
            pl.BlockSpec((None, 1, 2 * width), lambda i: (i // NSA_KV_HEADS, 0, 0)),
            pl.BlockSpec((None, HEAD_DIM, HEAD_DIM), lambda i: (i // NSA_KV_HEADS, 0, 0)),
        ],
        out_specs=pl.BlockSpec((None, rows, HEAD_DIM), lambda i: (i, 0, 0)),
        out_shape=jax.ShapeDtypeStruct((n, rows, HEAD_DIM), BF16),
        compiler_params=_cparams("arbitrary"),
        name="nsa_compress",
    )(a, w1, pos, w2)


def _nsa_cmp_kernel(q_ref, kc_ref, vc_ref, ovt_ref, ocmp_ref, sel_ref):
    qi = pl.program_id(0)
    tq = q_ref.shape[0]
    ncp = kc_ref.shape[1]
    gw = GROUP * HEAD_DIM
    t = qi * tq + lax.broadcasted_iota(jnp.int32, (tq, ncp), 0)
    n = lax.broadcasted_iota(jnp.int32, (tq, ncp), 1)
    vis = (n * NSA_CMP_STRIDE + (NSA_CMP_LEN - 1)) <= t
    ones = jnp.ones((ncp, LANES), BF16)
    blk = lax.broadcasted_iota(jnp.int32, (LANES, tq), 0)
    blkf = blk.astype(F32)
    cur = (qi * tq + lax.broadcasted_iota(jnp.int32, (LANES, tq), 1)) >> SLC_SHIFT
    forced = (blk == 0) | (blk == cur) | (blk == cur - 1)
    future = blk > cur
    ovt = ovt_ref[...]

    for h in range(NSA_KV_HEADS):
        kc = kc_ref[h]
        va = jnp.concatenate([vc_ref[h], ones], axis=1)
        psum = jnp.zeros((tq, ncp), F32)
        for g in range(GROUP):
            cols = slice(h * gw + g * HEAD_DIM, h * gw + (g + 1) * HEAD_DIM)
            s = jnp.where(vis, _dot_t(q_ref[:, cols], kc), NEG_INF)
            chunks = [s[:, c * LANES:(c + 1) * LANES] for c in range(ncp // LANES)]
            m = jnp.max(functools.reduce(jnp.maximum, chunks), axis=1, keepdims=True)
            p = jnp.where(vis, jnp.exp2(s - m), 0.0)
            pv = _dot(p.astype(BF16), va)
            inv = 1.0 / jnp.maximum(pv[:, HEAD_DIM:], 1e-30)
            ocmp_ref[:, cols] = pv[:, :HEAD_DIM] * inv
            psum = psum + p * jnp.concatenate([inv] * (ncp // LANES), axis=1)

        p_hi = psum.astype(BF16)
        p_lo = (psum - p_hi.astype(F32)).astype(BF16)
        imp = _dot_t(ovt, p_hi) + _dot_t(ovt, p_lo)
        imp = jnp.where(forced, jnp.inf, imp)
        imp = jnp.where(future, -jnp.inf, imp)
        sel = jnp.zeros(imp.shape, F32)
        for _ in range(NSA_SLC_TOPK):
            mx = jnp.max(imp, axis=0, keepdims=True)
            idx = jnp.min(jnp.where(imp == mx, blkf, float(LANES)), axis=0, keepdims=True)
            pick = blkf == idx
            sel = jnp.where(pick, 1.0, sel)
            imp = jnp.where(pick, -jnp.inf, imp)
        sel_ref[h] = jnp.where(sel > 0.5, 0.0, NEG_INF).T.astype(sel_ref.dtype)


def _nsa_cmp_select(proj, kv_c, overlap_t, q_cb):
    s = proj.shape[0]
    tq = BAND_BLOCK
    nq = NSA_KV_HEADS * GROUP
    ncp = kv_c.shape[1]
    assert q_cb % nq == 0 and tq == LANES
    return pl.pallas_call(
        _nsa_cmp_kernel,
        grid=(s // tq,),
        in_specs=[
            pl.BlockSpec((tq, nq * HEAD_DIM), lambda i: (i, q_cb // nq)),
            pl.BlockSpec((NSA_KV_HEADS, ncp, HEAD_DIM), lambda i: (0, 0, 0)),
            pl.BlockSpec((NSA_KV_HEADS, ncp, HEAD_DIM), lambda i: (1, 0, 0)),
            pl.BlockSpec((LANES, ncp), lambda i: (0, 0)),
        ],
        out_specs=[
            pl.BlockSpec((tq, nq * HEAD_DIM), lambda i: (i, 0)),
            pl.BlockSpec((NSA_KV_HEADS, tq, LANES), lambda i: (0, i, 0)),
        ],
        out_shape=[
            jax.ShapeDtypeStruct((s, nq * HEAD_DIM), F32),
            jax.ShapeDtypeStruct((NSA_KV_HEADS, s, LANES), BF16),
        ],
        compiler_params=_cparams("arbitrary"),
        name="nsa_cmp_select",
    )(proj, kv_c, kv_c, overlap_t)


def _nsa_slc_kernel(q_ref, k_ref, kx_ref, v_ref, sel_ref, o_ref, qa_ref, *state, tk):
    qi = pl.program_id(0)
    tq = q_ref.shape[0]
    rows = GROUP * tq
    gw = GROUP * HEAD_DIM
    for h in range(NSA_KV_HEADS):
        selb = sel_ref[h]
        for g in range(GROUP):
            qa_ref[h, g * tq:(g + 1) * tq, 0:HEAD_DIM] = q_ref[:, h * gw + g * HEAD_DIM:h * gw + (g + 1) * HEAD_DIM]
            qa_ref[h, g * tq:(g + 1) * tq, HEAD_DIM:] = selb
    qas = [qa_ref[h] for h in range(NSA_KV_HEADS)]

    last = jnp.right_shift(qi, _log2(tk // tq))
    t = qi * tq + (lax.broadcasted_iota(jnp.int32, (rows, tk), 0) & (tq - 1))
    kpos = last * tk + lax.broadcasted_iota(jnp.int32, (rows, tk), 1)
    outs = _flash_attention(qas, k_ref, kx_ref, v_ref, state, last, tk, kpos <= t)
    for h in range(NSA_KV_HEADS):
        for g in range(GROUP):
            o_ref[:, h * gw + g * HEAD_DIM:h * gw + (g + 1) * HEAD_DIM] = outs[h][g * tq:(g + 1) * tq, :]


def _nsa_selected(proj, selb, q_cb, k_cb, v_cb):
    s = proj.shape[0]
    tq = 2 * BAND_BLOCK
    tk = min(FLASH_TILE, s)
    nq = NSA_KV_HEADS * GROUP
    kvw = NSA_KV_HEADS * HEAD_DIM
    assert tq & (tq - 1) == 0 and s % tk == 0 and q_cb % nq == 0 and k_cb % NSA_KV_HEADS == v_cb % NSA_KV_HEADS == 0
    return pl.pallas_call(
        functools.partial(_nsa_slc_kernel, tk=tk),
        grid=(s // tq,),
        in_specs=[
            pl.BlockSpec((tq, nq * HEAD_DIM), lambda i: (i, q_cb // nq)),
            _resident((s, kvw), lambda i: (0, k_cb // NSA_KV_HEADS)),
            _resident((s, LANES), lambda i: (0, 0)),
            _resident((s, kvw), lambda i: (0, v_cb // NSA_KV_HEADS)),
            pl.BlockSpec((NSA_KV_HEADS, tq, LANES), lambda i: (0, i, 0)),
        ],
        out_specs=pl.BlockSpec((tq, nq * HEAD_DIM), lambda i: (i, 0)),
        out_shape=jax.ShapeDtypeStruct((s, nq * HEAD_DIM), F32),
        scratch_shapes=[pltpu.VMEM((NSA_KV_HEADS, GROUP * tq, 2 * HEAD_DIM), BF16)]
        + _flash_state(NSA_KV_HEADS, GROUP * tq, tk),
        compiler_params=_cparams("arbitrary"),
        name="nsa_selected",
    )(proj, proj, _block_onehot(s, NSA_SLC_BLOCK), proj, selb)


def _nsa_win_kernel(q_ref, k_ref, v_ref, gate_ref, ocmp_ref, oslc_ref, o_ref, qs_ref):
    qi = pl.program_id(0)
    tq = q_ref.shape[0]
    gw = GROUP * HEAD_DIM
    rows = GROUP * tq
    n_prev = -(-(NSA_WINDOW - 1) // tq)
    row = lax.broadcasted_iota(jnp.int32, (rows, tq), 0) & (tq - 1)
    col = lax.broadcasted_iota(jnp.int32, (rows, tq), 1)
    ones = jnp.ones((tq, LANES), BF16)
    for h in range(NSA_KV_HEADS):
        for g in range(GROUP):
            qs_ref[h, g * tq:(g + 1) * tq, :] = q_ref[:, h * gw + g * HEAD_DIM:h * gw + (g + 1) * HEAD_DIM]
        q = qs_ref[h]
        cols = slice(h * HEAD_DIM, (h + 1) * HEAD_DIM)
        scores, starts = [], []
        for b in range(n_prev + 1):
            kb = qi - b
            start = pl.multiple_of(jnp.maximum(kb, 0) * tq, tq)
            s = _dot_t(q, k_ref[pl.ds(start, tq), cols])
            dist = row - col + b * tq
            if b * tq - (tq - 1) < 0:
                s = jnp.where(dist >= 0, s, NEG_INF)
            if b * tq + (tq - 1) >= NSA_WINDOW:
                s = jnp.where(dist < NSA_WINDOW, s, NEG_INF)
            if b > 0:
                s = s + jnp.where(kb >= 0, 0.0, NEG_INF)
            scores.append(s)
            starts.append(start)
        m = jnp.max(functools.reduce(jnp.maximum, scores), axis=1, keepdims=True)
        pv = None
        for s, start in zip(scores, starts):
            va = jnp.concatenate([v_ref[pl.ds(start, tq), cols], ones], axis=1)
            term = _dot(jnp.exp2(s - m).astype(BF16), va)
            pv = term if pv is None else pv + term
        o_win = pv[:, :HEAD_DIM] / pv[:, HEAD_DIM:]
        sig = jax.nn.sigmoid(gate_ref[h])
        for g in range(GROUP):
            oc = slice(h * gw + g * HEAD_DIM, h * gw + (g + 1) * HEAD_DIM)
            o = (sig[:, 3 * g:3 * g + 1] * ocmp_ref[:, oc]
                 + sig[:, 3 * g + 1:3 * g + 2] * oslc_ref[:, oc]
                 + sig[:, 3 * g + 2:3 * g + 3] * o_win[g * tq:(g + 1) * tq, :])
            o_ref[:, oc] = o.astype(o_ref.dtype)


def _nsa_window_merge(proj, gates, o_cmp, o_slc, q_cb, k_cb, v_cb):
    s = proj.shape[0]
    tq = BAND_BLOCK
    nq = NSA_KV_HEADS * GROUP
    kvw = NSA_KV_HEADS * HEAD_DIM
    assert tq & (tq - 1) == 0 and q_cb % nq == 0 and k_cb % NSA_KV_HEADS == v_cb % NSA_KV_HEADS == 0
    wide = pl.BlockSpec((tq, nq * HEAD_DIM), lambda i: (i, 0))
    return pl.pallas_call(
        _nsa_win_kernel,
        grid=(s // tq,),
        in_specs=[
            pl.BlockSpec((tq, nq * HEAD_DIM), lambda i: (i, q_cb // nq)),
            _resident((s, kvw), lambda i: (0, k_cb // NSA_KV_HEADS)),
            _resident((s, kvw), lambda i: (0, v_cb // NSA_KV_HEADS)),
            pl.BlockSpec((NSA_KV_HEADS, tq, LANES), lambda i: (0, i, 0)),
            wide,
            wide,
        ],
        out_specs=wide,
        out_shape=jax.ShapeDtypeStruct((s, nq * HEAD_DIM), BF16),
        scratch_shapes=[pltpu.VMEM((NSA_KV_HEADS, GROUP * tq, HEAD_DIM), BF16)],
        compiler_params=_cparams("arbitrary"),
        name="nsa_window_merge",
    )(proj, proj, proj, gates, o_cmp, o_slc)


def _overlap_matrix(seq):
    n_pad = seq // NSA_CMP_STRIDE
    n_cmp = (seq - NSA_CMP_LEN) // NSA_CMP_STRIDE + 1
    c_start = np.arange(n_pad)[:, None] * NSA_CMP_STRIDE
    s_start = np.arange(LANES)[None, :] * NSA_SLC_BLOCK
    ov = (c_start < s_start + NSA_SLC_BLOCK) & (c_start + NSA_CMP_LEN > s_start)
    ov &= (np.arange(n_pad)[:, None] < n_cmp) & (np.arange(LANES)[None, :] < seq // NSA_SLC_BLOCK)
    return jnp.asarray(ov.T.astype(np.float32), dtype=BF16)


def _ctypes(spec):
    return jnp.asarray(np.concatenate([np.full(n, kind, np.int32) for n, kind in spec]))


def _even_layer(x, mods, norm_g, w_in, sinks, w_out, layer, tables):
    sh1, sc1, g1 = mods
    heads = w_in.shape[2] // HEAD_DIM
    n_moba = 8
    ctypes = _ctypes([(n_moba, CT_ROPE_SCALE), (n_moba, CT_ROPE), (n_moba, CT_PLAIN),
                      (8, CT_ROPE_SCALE), (SWA_KV_HEADS, CT_ROPE), (SWA_KV_HEADS, CT_PLAIN)])
    assert ctypes.shape[0] == heads
    proj = _norm_proj(x, norm_g, sc1, sh1, w_in, ctypes, tables, 512, layer=layer)
    nb = x.shape[0] // MOBA_BLOCK
    kmean = _moba_kmean(proj, 1, n_moba * HEAD_DIM)
    kmean = jnp.pad(kmean, ((0, LANES - nb), (0, 0))).astype(BF16)
    oa = _moba_attention(proj, kmean, n_moba, 0, n_moba, 2 * n_moba)
    ob = _swa_attention(proj, sinks, 3 * n_moba, 4 * n_moba, 4 * n_moba + SWA_KV_HEADS)
    return _out_proj(oa, ob, w_out, layer, x, g1)


def _odd_layer(x, mods, norm_g, w_in, forget_b, k_pos, k_w1, k_w2, v_pos, v_w1, v_w2, w_out, layer, tables, overlap):
    sh1, sc1, g1 = mods
    s = x.shape[0]
    n_fox = 8
    hw = n_fox * HEAD_DIM
    kvw = NSA_KV_HEADS * HEAD_DIM
    o_fc = 3 * hw
    o_qd = o_fc + n_fox
    o_gd = o_qd + hw + 6 * kvw
    w_main = jnp.concatenate([w_in[:, :o_fc], w_in[:, o_qd:o_gd]], axis=1).astype(BF16)
    n_aux = n_fox + 3 * 8
    w_aux = jnp.concatenate([w_in[:, o_fc:o_qd], w_in[:, o_gd:], jnp.zeros((w_in.shape[0], LANES - n_aux), F32)],
                            axis=1).astype(BF16)
    ctypes = _ctypes([(8, CT_SCALE), (8, CT_PLAIN), (8, CT_PLAIN), (8, CT_ROPE_SCALE),
                      (2, CT_ROPE), (2, CT_PLAIN), (2, CT_ROPE), (2, CT_PLAIN), (2, CT_ROPE), (2, CT_PLAIN)])
    proj, aux = _norm_proj(x, norm_g, sc1, sh1, w_main, ctypes, tables, 512, w_aux=w_aux)

    bias_row = jnp.pad(forget_b.astype(F32), (0, LANES - n_fox)).reshape(1, LANES)
    fox_qx, fox_kx = _fox_cum(aux, bias_row, n_fox)
    oc = _fox_attention(proj, fox_qx, fox_kx, n_fox, 0, 8, 16)

    c0 = 32
    cmp_in = proj[:, c0 * HEAD_DIM:(c0 + 4) * HEAD_DIM].reshape(s, 4, HEAD_DIM).transpose(1, 0, 2)
    cmp_in = cmp_in.reshape(4, s // NSA_CMP_STRIDE, NSA_CMP_STRIDE * HEAD_DIM)
    kv_c = _nsa_compress(
        cmp_in,
        jnp.stack([k_w1, v_w1]).astype(BF16),
        jnp.stack([k_pos.reshape(1, -1), v_pos.reshape(1, -1)]).astype(BF16),
        jnp.stack([k_w2, v_w2]).astype(BF16))
    o_cmp, sel = _nsa_cmp_select(proj, kv_c, overlap, 24)
    o_slc = _nsa_selected(proj, sel, 24, c0 + 4, c0 + 6)
    gates = aux[:, n_fox:n_aux].reshape(s, NSA_KV_HEADS, 3 * GROUP).transpose(1, 0, 2)
    gates = jnp.pad(gates, ((0, 0), (0, 0), (0, LANES - 3 * GROUP)))
    od = _nsa_window_merge(proj, gates, o_cmp, o_slc, 24, c0 + 8, c0 + 10)
    return _out_proj(oc, od, w_out, layer, x, g1)


def kernel(x, c, norm_mix_g, norm_mlp_g, ada_w, ada_b, mlp_up, mlp_down, even_w_in, even_sinks, even_w_out, odd_w_in, fox_forget_b, nsa_k_pos, nsa_k_w1, nsa_k_w2, nsa_v_pos, nsa_v_w1, nsa_v_w2, odd_w_out, final_norm_g):
    batch, seq, d = x.shape
    assert batch == 1
    depth = ada_w.shape[0]
    tables = _epilogue_tables(seq)
    overlap = _overlap_matrix(seq)
    mod = _ada_mod(c, ada_w, ada_b).reshape(depth, 6, 1, d)
    xs = x[0]
    for i in range(depth):
        sh1, sc1, g1, sh2, sc2, g2 = [mod[i, t] for t in range(6)]
        ng = norm_mix_g[i].reshape(1, d)
        j = i // 2
        if i % 2 == 0:
            xs = _even_layer(xs, (sh1, sc1, g1), ng, even_w_in, even_sinks[j], even_w_out, j, tables)
        else:
            xs = _odd_layer(xs, (sh1, sc1, g1), ng, odd_w_in[j], fox_forget_b[j], nsa_k_pos[j], nsa_k_w1[j],
                            nsa_k_w2[j], nsa_v_pos[j], nsa_v_w1[j], nsa_v_w2[j], odd_w_out, j, tables, overlap)
        xs = _mlp(xs, norm_mlp_g[i].reshape(1, d), sc2, sh2, mlp_up, mlp_down, i, g2,
                  final_norm_g.reshape(1, d), final=(i == depth - 1))
    return xs[None]
```

```python
import functools
import math

import numpy as np
import jax
import jax.numpy as jnp
from jax import lax
from jax.experimental import pallas as pl
from jax.experimental.pallas import tpu as pltpu

HEAD_DIM = 128
ROPE_THETA = 10000.0
NORM_EPS = 1e-6
MOBA_BLOCK = 256
MOBA_TOPK = 3
SWA_KV_HEADS = 2
SWA_WINDOW = 128
NSA_KV_HEADS = 2
NSA_CMP_LEN = 32
NSA_CMP_STRIDE = 16
NSA_SLC_BLOCK = 64
SLC_SHIFT = 6
NSA_SLC_TOPK = 16
NSA_WINDOW = 512
BAND_BLOCK = 128
NEG_INF = -1e30
LOG2E = math.log2(math.e)
SCALE = HEAD_DIM ** -0.5 * LOG2E
LANES = 128
GROUP = 4
VMEM_LIMIT = 48 * 1024 * 1024

F32 = jnp.float32
BF16 = jnp.bfloat16

CT_PLAIN, CT_ROPE, CT_ROPE_SCALE, CT_SCALE = 0, 1, 2, 3


VMEM_LIMIT_MLP = 56 * 1024 * 1024


def _cparams(*sem, vmem=VMEM_LIMIT):
    return pltpu.CompilerParams(dimension_semantics=sem, vmem_limit_bytes=vmem)


def _dot(a, b):
    return jnp.dot(a, b, preferred_element_type=F32)


def _dot_t(a, b):
    return lax.dot_general(a, b, (((1,), (1,)), ((), ())), preferred_element_type=F32)


def _ada_kernel(c_ref, w_ref, b_ref, o_ref):
    c = c_ref[...]
    cond = c * jax.nn.sigmoid(c)
    cond8 = jnp.broadcast_to(cond, (8, cond.shape[1])).astype(BF16)
    y = _dot(cond8, w_ref[...].astype(BF16))
    o_ref[...] = y[0:1] + b_ref[...]


def _ada_mod(c, ada_w, ada_b):
    depth, d, n = ada_w.shape
    tn = 1024
    return pl.pallas_call(
        _ada_kernel,
        grid=(depth, n // tn),
        in_specs=[
            pl.BlockSpec((1, d), lambda l, j: (0, 0)),
            pl.BlockSpec((None, d, tn), lambda l, j: (l, 0, j)),
            pl.BlockSpec((None, 1, tn), lambda l, j: (l, 0, j)),
        ],
        out_specs=pl.BlockSpec((None, 1, tn), lambda l, j: (l, 0, j)),
        out_shape=jax.ShapeDtypeStruct((depth, 1, n), F32),
        compiler_params=_cparams("arbitrary", "arbitrary"),
        name="ada_mod",
    )(c, ada_w, ada_b.reshape(depth, 1, n))


def _norm_mod(x, g, sc, sh):
    y = x * lax.rsqrt(jnp.mean(x * x, axis=-1, keepdims=True) + NORM_EPS)
    return (y * g) * (1.0 + sc) + sh


PROJ_ROW_CHUNKS = 2


def _proj_kernel(ct_ref, x_ref, g_ref, sc_ref, sh_ref, w_ref, ta_ref, tb_ref, *rest, aux):
    if aux:
        wa_ref, o_ref, oa_ref, h_ref = rest
    else:
        o_ref, h_ref = rest
    j = pl.program_id(1)

    @pl.when(j == 0)
    def _():
        h_ref[...] = _norm_mod(x_ref[...], g_ref[...], sc_ref[...], sh_ref[...]).astype(BF16)
        if aux:
            oa_ref[...] = _dot(h_ref[...], wa_ref[...])

    w = w_ref[...].astype(BF16)
    groups = w.shape[1] // LANES
    tm = h_ref.shape[0]
    chunk = tm // PROJ_ROW_CHUNKS
    for ri in range(PROJ_ROW_CHUNKS):
        rows = slice(ri * chunk, (ri + 1) * chunk)
        y = _dot(h_ref[rows, :], w)
        for gi in range(groups):
            lo, hi = gi * LANES, (gi + 1) * LANES
            yg = y[:, lo:hi]
            ct = ct_ref[j * groups + gi]
            r = yg * ta_ref[ct, rows, :] + pltpu.roll(yg, HEAD_DIM // 2, 1) * tb_ref[ct, rows, :]
            o_ref[rows, lo:hi] = r.astype(o_ref.dtype)


def _epilogue_tables(seq):
    f32 = np.float32
    inv = (f32(1.0) / (f32(ROPE_THETA) ** (np.arange(0, HEAD_DIM, 2, dtype=f32) / f32(HEAD_DIM)))).astype(f32)
    ang = np.arange(seq, dtype=f32)[:, None] * inv[None, :]
    cos, sin = np.cos(ang), np.sin(ang)
    cosf = np.concatenate([cos, cos], axis=-1)
    sinf = np.concatenate([-sin, sin], axis=-1)
    one, zero = np.ones_like(cosf), np.zeros_like(cosf)
    by_kind = {CT_PLAIN: (one, zero), CT_ROPE: (cosf, sinf), CT_ROPE_SCALE: (cosf * f32(SCALE), sinf * f32(SCALE)),
               CT_SCALE: (one * f32(SCALE), zero)}
    return (jnp.asarray(np.stack([by_kind[k][0] for k in range(4)])),
            jnp.asarray(np.stack([by_kind[k][1] for k in range(4)])))


def _norm_proj(x, g, sc, sh, w, ctypes, tables, tn, layer=None, w_aux=None):
    s, d = x.shape
    n = w.shape[-1]
    tm = min(1024, s)
    kinds = tables[0].shape[0]
    if layer is None:
        w_spec = pl.BlockSpec((d, tn), lambda i, j, ct: (0, j))
    else:
        w_spec = pl.BlockSpec((None, d, tn), lambda i, j, ct: (layer, 0, j))
    aux = w_aux is not None
    in_specs = [
        pl.BlockSpec((tm, d), lambda i, j, ct: (i, 0)),
        pl.BlockSpec((1, d), lambda i, j, ct: (0, 0)),
        pl.BlockSpec((1, d), lambda i, j, ct: (0, 0)),
        pl.BlockSpec((1, d), lambda i, j, ct: (0, 0)),
        w_spec,
        pl.BlockSpec((kinds, tm, LANES), lambda i, j, ct: (0, i, 0)),
        pl.BlockSpec((kinds, tm, LANES), lambda i, j, ct: (0, i, 0)),
    ]
    out_specs = pl.BlockSpec((tm, tn), lambda i, j, ct: (i, j))
    out_shape = jax.ShapeDtypeStruct((s, n), BF16)
    operands = [ctypes, x, g, sc, sh, w, *tables]
    if aux:
        in_specs.append(pl.BlockSpec((d, LANES), lambda i, j, ct: (0, 0)))
        out_specs = [out_specs, pl.BlockSpec((tm, LANES), lambda i, j, ct: (i, 0))]
        out_shape = [out_shape, jax.ShapeDtypeStruct((s, LANES), F32)]
        operands.append(w_aux)
    grid_spec = pltpu.PrefetchScalarGridSpec(
        num_scalar_prefetch=1,
        grid=(s // tm, n // tn),
        in_specs=in_specs,
        out_specs=out_specs,
        scratch_shapes=[pltpu.VMEM((tm, d), BF16)],
    )
    return pl.pallas_call(
        functools.partial(_proj_kernel, aux=aux),
        grid_spec=grid_spec,
        out_shape=out_shape,
        compiler_params=_cparams("arbitrary", "arbitrary"),
        name="norm_proj",
    )(*operands)


def _outproj_kernel(a_ref, b_ref, w_ref, x_ref, g_ref, o_ref):
    ha = a_ref.shape[1]
    y = _dot(a_ref[...], w_ref[0:ha, :]) + _dot(b_ref[...], w_ref[ha:, :])
    o_ref[...] = x_ref[...] + g_ref[...] * y


def _out_proj(oa, ob, w, x, gate):
    s, d = x.shape
    ha, hb = oa.shape[1], ob.shape[1]
    tm = min(512, s)
    return pl.pallas_call(
        _outproj_kernel,
        grid=(s // tm,),
        in_specs=[
            pl.BlockSpec((tm, ha), lambda i: (i, 0)),
            pl.BlockSpec((tm, hb), lambda i: (i, 0)),
            _resident((ha + hb, d), lambda i: (0, 0)),
            pl.BlockSpec((tm, d), lambda i: (i, 0)),
            pl.BlockSpec((1, d), lambda i: (0, 0)),
        ],
        out_specs=pl.BlockSpec((tm, d), lambda i: (i, 0)),
        out_shape=jax.ShapeDtypeStruct((s, d), F32),
        compiler_params=_cparams("arbitrary"),
        name="out_proj",
    )(oa, ob, w, x, gate)


def _rms_norm(x, g):
    return x * lax.rsqrt(jnp.mean(x * x, axis=-1, keepdims=True) + NORM_EPS) * g


def _mlp_kernel(x_ref, g_ref, sc_ref, sh_ref, up_ref, down_ref, gate_ref, fg_ref, o_ref, h_ref, *, final):
    f = pl.program_id(1)

    @pl.when(f == 0)
    def _():
        h_ref[...] = _norm_mod(x_ref[...], g_ref[...], sc_ref[...], sh_ref[...]).astype(BF16)
        o_ref[...] = jnp.zeros_like(o_ref)

    hid = jnp.maximum(_dot(h_ref[...], up_ref[...].astype(BF16)), 0.0)
    o_ref[...] += _dot((hid * hid).astype(BF16), down_ref[...].astype(BF16))

    @pl.when(f == pl.num_programs(1) - 1)
    def _():
        y = x_ref[...] + gate_ref[...] * o_ref[...]
        o_ref[...] = _rms_norm(y, fg_ref[...]) if final else y


def _mlp(x, g, sc, sh, up, down, layer, gate, final_g, final):
    s, d = x.shape
    ff = up.shape[2]
    tm, tf = min(1024, s), 512
    return pl.pallas_call(
        functools.partial(_mlp_kernel, final=final),
        grid=(s // tm, ff // tf),
        in_specs=[
            pl.BlockSpec((tm, d), lambda i, f: (i, 0)),
            pl.BlockSpec((1, d), lambda i, f: (0, 0)),
            pl.BlockSpec((1, d), lambda i, f: (0, 0)),
            pl.BlockSpec((1, d), lambda i, f: (0, 0)),
            pl.BlockSpec((None, d, tf), lambda i, f: (layer, 0, f)),
            pl.BlockSpec((None, tf, d), lambda i, f: (layer, f, 0)),
            pl.BlockSpec((1, d), lambda i, f: (0, 0)),
            pl.BlockSpec((1, d), lambda i, f: (0, 0)),
        ],
        out_specs=pl.BlockSpec((tm, d), lambda i, f: (i, 0), pipeline_mode=pl.Buffered(1)),
        out_shape=jax.ShapeDtypeStruct((s, d), F32),
        scratch_shapes=[pltpu.VMEM((tm, d), BF16)],
        compiler_params=_cparams("arbitrary", "arbitrary", vmem=VMEM_LIMIT_MLP),
        name="mlp",
    )(x, g, sc, sh, up, down, gate, final_g)


def _rows(ref, start, size):
    return ref[pl.ds(pl.multiple_of(start, size), size), :]


def _log2(n):
    assert n > 0 and n & (n - 1) == 0
    return n.bit_length() - 1


FLASH_TILE = 512
HEADS_PER_STEP = 4


def _resident(shape, index_map):
    return pl.BlockSpec(shape, index_map, pipeline_mode=pl.Buffered(1))


def _flash_attention(qas, k_ref, kx_ref, v_ref, state, last, tk, diag_mask):
    m_ref, alpha_ref, p_ref, acc_ref = state
    chains = range(len(qas))
    ones = jnp.ones((tk, LANES), BF16)

    def score(start, mask, first):
        rows = pl.ds(pl.multiple_of(start, tk), tk)
        for c in chains:
            ka = jnp.concatenate([k_ref[rows, c * HEAD_DIM:(c + 1) * HEAD_DIM], kx_ref[rows, :]], axis=1)
            s = _dot_t(qas[c], ka)
            if mask is not None:
                s = jnp.where(mask, s, NEG_INF)
            chunks = [s[:, i * LANES:(i + 1) * LANES] for i in range(tk // LANES)]
            m_cur = jnp.max(functools.reduce(jnp.maximum, chunks), axis=1, keepdims=True)
            if first:
                m_new = jnp.broadcast_to(m_cur, (s.shape[0], LANES))
            else:
                m_prev = m_ref[c]
                m_new = jnp.maximum(m_prev, m_cur)
                alpha_ref[c] = jnp.exp2(m_prev - m_new)
            p_ref[c] = jnp.concatenate([jnp.exp2(ch - m_new) for ch in chunks], axis=1).astype(BF16)
            m_ref[c] = m_new

    def accumulate(start):
        rows = pl.ds(pl.multiple_of(start, tk), tk)
        for c in chains:
            va = jnp.concatenate([v_ref[rows, c * HEAD_DIM:(c + 1) * HEAD_DIM], ones], axis=1)
            alpha = alpha_ref[c]
            acc_ref[c] = jnp.concatenate([alpha, alpha], axis=1) * acc_ref[c] + _dot(p_ref[c], va)

    for c in chains:
        acc_ref[c] = jnp.zeros(acc_ref.shape[1:], F32)
        alpha_ref[c] = jnp.ones(alpha_ref.shape[1:], F32)
    score(last * tk, diag_mask, True)

    def body(p, carry):
        accumulate(jnp.where(p == 0, last, p - 1) * tk)
        score(p * tk, None, False)
        return carry

    lax.fori_loop(0, last, body, 0)
    accumulate(jnp.where(last == 0, last, last - 1) * tk)
    return [acc_ref[c][:, :HEAD_DIM] / acc_ref[c][:, HEAD_DIM:] for c in chains]


def _flash_state(chains, rows, tk):
    return [pltpu.VMEM((chains, rows, LANES), F32), pltpu.VMEM((chains, rows, LANES), F32),
            pltpu.VMEM((chains, rows, tk), BF16), pltpu.VMEM((chains, rows, 2 * HEAD_DIM), F32)]


def _block_onehot(seq, block):
    return jnp.asarray((np.arange(seq)[:, None] // block == np.arange(LANES)[None, :]).astype(np.float32), dtype=BF16)


def _kmean_kernel(k_ref, o_ref):
    k = k_ref[...].astype(F32)
    o_ref[...] = jnp.sum(k, axis=0, keepdims=True) * (1.0 / k.shape[0])


def _moba_kmean(proj, col_block, width):
    s = proj.shape[0]
    nb = s // MOBA_BLOCK
    out = pl.pallas_call(
        _kmean_kernel,
        grid=(nb,),
        in_specs=[pl.BlockSpec((MOBA_BLOCK, width), lambda b: (b, col_block))],
        out_specs=pl.BlockSpec((None, 1, width), lambda b: (b, 0, 0)),
        out_shape=jax.ShapeDtypeStruct((nb, 1, width), F32),
        compiler_params=_cparams("arbitrary"),
        name="moba_kmean",
    )(proj)
    return out.reshape(nb, width)


def _moba_kernel(q_ref, k_ref, kx_ref, v_ref, km_ref, o_ref, *state, tk, hp):
    qi = pl.program_id(1)
    tq = q_ref.shape[0]
    blk = lax.broadcasted_iota(jnp.int32, (LANES, tq), 0)
    blkf = blk.astype(F32)
    own = jnp.right_shift(qi * tq + lax.broadcasted_iota(jnp.int32, (LANES, tq), 1), _log2(MOBA_BLOCK))
    ownf = own.astype(F32)

    qas = []
    for hh in range(hp):
        cols = slice(hh * HEAD_DIM, (hh + 1) * HEAD_DIM)
        q = q_ref[:, cols]
        g = jnp.where(blk < own, _dot_t(km_ref[:, cols], q), NEG_INF)
        sel = jnp.zeros((LANES, tq), F32)
        for _ in range(MOBA_TOPK):
            mx = jnp.max(g, axis=0, keepdims=True)
            idx = jnp.min(jnp.where(g == mx, blkf, float(LANES)), axis=0, keepdims=True)
            pick = blkf == idx
            sel = jnp.where(pick & (idx < ownf), 1.0, sel)
            g = jnp.where(pick, -jnp.inf, g)
        qx = jnp.where((sel > 0.5) | (blk == own), 0.0, NEG_INF).T.astype(BF16)
        qas.append(jnp.concatenate([q, qx], axis=1))

    last = jnp.right_shift(qi, _log2(tk // tq))
    t = qi * tq + lax.broadcasted_iota(jnp.int32, (tq, tk), 0)
    kpos = last * tk + lax.broadcasted_iota(jnp.int32, (tq, tk), 1)
    outs = _flash_attention(qas, k_ref, kx_ref, v_ref, state, last, tk, kpos <= t)
    for hh in range(hp):
        o_ref[:, hh * HEAD_DIM:(hh + 1) * HEAD_DIM] = outs[hh].astype(o_ref.dtype)


def _moba_attention(proj, kmean, n_heads, q_cb, k_cb, v_cb):
    s = proj.shape[0]
    tq = tk = min(FLASH_TILE, s)
    hp = HEADS_PER_STEP
    hw = hp * HEAD_DIM
    assert s % tk == 0 and tk % MOBA_BLOCK == 0 and n_heads % hp == 0 and q_cb % hp == k_cb % hp == v_cb % hp == 0
    return pl.pallas_call(
        functools.partial(_moba_kernel, tk=tk, hp=hp),
        grid=(n_heads // hp, s // tq),
        in_specs=[
            pl.BlockSpec((tq, hw), lambda h, i: (i, q_cb // hp + h)),
            _resident((s, hw), lambda h, i: (0, k_cb // hp + h)),
            _resident((s, LANES), lambda h, i: (0, 0)),
            _resident((s, hw), lambda h, i: (0, v_cb // hp + h)),
            pl.BlockSpec((LANES, hw), lambda h, i: (0, h)),
        ],
        out_specs=pl.BlockSpec((tq, hw), lambda h, i: (i, h)),
        out_shape=jax.ShapeDtypeStruct((s, n_heads * HEAD_DIM), BF16),
        scratch_shapes=_flash_state(hp, tq, tk),
        compiler_params=_cparams("arbitrary", "arbitrary"),
        name="moba_attn",
    )(proj, proj, _block_onehot(s, MOBA_BLOCK), proj, kmean)


def _swa_kernel(sink_ref, q_ref, k_ref, v_ref, o_ref, qs_ref):
    qi = pl.program_id(0)
    tq = q_ref.shape[0]
    tk = 2 * tq
    rows = GROUP * tq
    gw = GROUP * HEAD_DIM
    start = pl.multiple_of(jnp.maximum(qi - 1, 0) * tq, tq)
    t = qi * tq + (lax.broadcasted_iota(jnp.int32, (rows, tk), 0) & (tq - 1))
    kpos = start + lax.broadcasted_iota(jnp.int32, (rows, tk), 1)
    ok = (kpos <= t) & (kpos > t - SWA_WINDOW)
    ones = jnp.ones((tk, LANES), BF16)
    for h in range(SWA_KV_HEADS):
        for g in range(GROUP):
            qs_ref[h, g * tq:(g + 1) * tq, :] = q_ref[:, h * gw + g * HEAD_DIM:h * gw + (g + 1) * HEAD_DIM]
        cols = slice(h * HEAD_DIM, (h + 1) * HEAD_DIM)
        s = jnp.where(ok, _dot_t(qs_ref[h], k_ref[pl.ds(start, tk), cols]), NEG_INF)
        sink = jnp.concatenate([jnp.full((tq, 1), sink_ref[h * GROUP + g] * LOG2E, F32) for g in range(GROUP)],
                               axis=0)
        m = jnp.maximum(jnp.max(jnp.maximum(s[:, :tq], s[:, tq:]), axis=1, keepdims=True), sink)
        p = jnp.exp2(s - m)
        pv = _dot(p.astype(BF16), jnp.concatenate([v_ref[pl.ds(start, tk), cols], ones], axis=1))
        o = pv[:, :HEAD_DIM] / (pv[:, HEAD_DIM:] + jnp.exp2(sink - m))
        for g in range(GROUP):
            o_ref[:, h * gw + g * HEAD_DIM:h * gw + (g + 1) * HEAD_DIM] = o[g * tq:(g + 1) * tq, :].astype(o_ref.dtype)


def _swa_attention(proj, sinks, q_cb, k_cb, v_cb):
    s = proj.shape[0]
    tq = BAND_BLOCK
    nq = SWA_KV_HEADS * GROUP
    kvw = SWA_KV_HEADS * HEAD_DIM
    assert SWA_WINDOW <= tq and tq == LANES and q_cb % nq == 0 and k_cb % SWA_KV_HEADS == v_cb % SWA_KV_HEADS == 0
    return pl.pallas_call(
        _swa_kernel,
        grid=(s // tq,),
        in_specs=[
            pl.BlockSpec(memory_space=pltpu.SMEM),
            pl.BlockSpec((tq, nq * HEAD_DIM), lambda i: (i, q_cb // nq)),
            _resident((s, kvw), lambda i: (0, k_cb // SWA_KV_HEADS)),
            _resident((s, kvw), lambda i: (0, v_cb // SWA_KV_HEADS)),
        ],
        out_specs=pl.BlockSpec((tq, nq * HEAD_DIM), lambda i: (i, 0)),
        out_shape=jax.ShapeDtypeStruct((s, nq * HEAD_DIM), BF16),
        scratch_shapes=[pltpu.VMEM((SWA_KV_HEADS, GROUP * tq, HEAD_DIM), BF16)],
        compiler_params=_cparams("arbitrary"),
        name="swa_attn",
    )(sinks, proj, proj, proj)


FOX_LANES = 16


def _split3(x):
    x1 = x.astype(BF16)
    r1 = x - x1.astype(F32)
    x2 = r1.astype(BF16)
    return x1, x2, (r1 - x2.astype(F32)).astype(BF16)


def _fox_tables(n_heads):
    place = np.zeros((6, LANES, LANES), np.float32)
    ones_q = np.zeros((1, LANES), np.float32)
    ones_k = np.zeros((1, LANES), np.float32)
    for h in range(n_heads):
        for i in range(6):
            place[i, h, FOX_LANES * h + i] = 1.0
        ones_q[0, FOX_LANES * h + 3:FOX_LANES * h + 6] = 1.0
        ones_k[0, FOX_LANES * h:FOX_LANES * h + 3] = 1.0
    return jnp.asarray(place, dtype=BF16), jnp.asarray(ones_q), jnp.asarray(ones_k)


def _fox_cum_kernel(aux_ref, b_ref, place_ref, oq_ref, ok_ref, qx_ref, kx_ref, carry_ref):
    i = pl.program_id(0)

    @pl.when(i == 0)
    def _():
        carry_ref[...] = jnp.zeros_like(carry_ref)

    x = aux_ref[...] + b_ref[...]
    log_f = -(jnp.maximum(-x, 0.0) + jnp.log(1.0 + jnp.exp(-jnp.abs(x))))
    tb = x.shape[0]
    r = lax.broadcasted_iota(jnp.int32, (tb, tb), 0)
    c = lax.broadcasted_iota(jnp.int32, (tb, tb), 1)
    tri = jnp.where(c <= r, 1.0, 0.0).astype(BF16)
    x1, x2, x3 = _split3(log_f)
    cum = _dot(tri, x1) + _dot(tri, x2) + _dot(tri, x3) + carry_ref[...]
    carry_ref[...] = cum[tb - 1:tb, :]
    c1, c2, c3 = _split3(cum * LOG2E)
    qx = _dot(c1, place_ref[0]) + _dot(c2, place_ref[1]) + _dot(c3, place_ref[2]) + oq_ref[...]
    kx = ok_ref[...] - (_dot(c1, place_ref[3]) + _dot(c2, place_ref[4]) + _dot(c3, place_ref[5]))
    qx_ref[...] = qx.astype(BF16)
    kx_ref[...] = kx.astype(BF16)


def _fox_cum(aux, bias_row, n_heads):
    s, w = aux.shape
    tb = min(512, s)
    place, ones_q, ones_k = _fox_tables(n_heads)
    full = lambda shape: pl.BlockSpec(shape, lambda i: (0,) * len(shape))
    return pl.pallas_call(
        _fox_cum_kernel,
        grid=(s // tb,),
        in_specs=[pl.BlockSpec((tb, w), lambda i: (i, 0)), full((1, w)), full(place.shape), full((1, w)), full((1, w))],
        out_specs=[pl.BlockSpec((tb, w), lambda i: (i, 0)), pl.BlockSpec((tb, w), lambda i: (i, 0))],
        out_shape=[jax.ShapeDtypeStruct((s, w), BF16), jax.ShapeDtypeStruct((s, w), BF16)],
        scratch_shapes=[pltpu.VMEM((1, w), F32)],
        compiler_params=_cparams("arbitrary"),
        name="fox_cum",
    )(aux, bias_row, place, ones_q, ones_k)


def _fox_kernel(q_ref, qx_ref, k_ref, kx_ref, v_ref, o_ref, *state, tk, hp):
    hb = pl.program_id(0)
    qi = pl.program_id(1)
    tq = q_ref.shape[0]
    lane = lax.broadcasted_iota(jnp.int32, (tq, LANES), 1)
    qxf = qx_ref[...].astype(F32)
    qas = []
    for hh in range(hp):
        mine = jnp.right_shift(lane, 4) == hb * hp + hh
        qx = jnp.where(mine, qxf, 0.0).astype(BF16)
        qas.append(jnp.concatenate([q_ref[:, hh * HEAD_DIM:(hh + 1) * HEAD_DIM], qx], axis=1))

    last = jnp.right_shift(qi, _log2(tk // tq))
    t = qi * tq + lax.broadcasted_iota(jnp.int32, (tq, tk), 0)
    kpos = last * tk + lax.broadcasted_iota(jnp.int32, (tq, tk), 1)
    outs = _flash_attention(qas, k_ref, kx_ref, v_ref, state, last, tk, kpos <= t)
    for hh in range(hp):
        o_ref[:, hh * HEAD_DIM:(hh + 1) * HEAD_DIM] = outs[hh].astype(o_ref.dtype)


def _fox_attention(proj, qx, kx, n_heads, q_cb, k_cb, v_cb):
    s = proj.shape[0]
    tq = tk = min(FLASH_TILE, s)
    hp = HEADS_PER_STEP
    hw = hp * HEAD_DIM
    assert s % tk == 0 and FOX_LANES == 16 and n_heads % hp == 0 and q_cb % hp == k_cb % hp == v_cb % hp == 0
    return pl.pallas_call(
        functools.partial(_fox_kernel, tk=tk, hp=hp),
        grid=(n_heads // hp, s // tq),
        in_specs=[
            pl.BlockSpec((tq, hw), lambda h, i: (i, q_cb // hp + h)),
            pl.BlockSpec((tq, LANES), lambda h, i: (i, 0)),
            _resident((s, hw), lambda h, i: (0, k_cb // hp + h)),
            _resident((s, LANES), lambda h, i: (0, 0)),
            _resident((s, hw), lambda h, i: (0, v_cb // hp + h)),
        ],
        out_specs=pl.BlockSpec((tq, hw), lambda h, i: (i, h)),
        out_shape=jax.ShapeDtypeStruct((s, n_heads * HEAD_DIM), BF16),
        scratch_shapes=_flash_state(hp, tq, tk),
        compiler_params=_cparams("arbitrary", "arbitrary"),
        name="fox_attn",
    )(proj, qx, proj, kx, proj)


def _gelu_tanh(x):
    return 0.5 * x * (1.0 + jnp.tanh(math.sqrt(2.0 / math.pi) * (x + 0.044715 * (x * x * x))))


def _nsa_compress_kernel(a_ref, w1_ref, pos_ref, w2_ref, o_ref):
    a = a_ref[...]
    half = a.shape[1]
    n = a.shape[0]
    lo = _dot(a, w1_ref[0:half, :])
    hi = _dot(a, w1_ref[half:, :])
    pos8 = jnp.broadcast_to(pos_ref[...], (8, pos_ref.shape[1]))
    pterm = _dot(pos8, w1_ref[...])[0:1]
    pre = lo + pltpu.roll(hi, n - 1, 0) + pterm
    o_ref[...] = _dot(_gelu_tanh(pre).astype(BF16), w2_ref[...]).astype(o_ref.dtype)


def _nsa_compress(a, w1, pos, w2):
    n, rows, width = a.shape
    return pl.pallas_call(
        _nsa_compress_kernel,
        grid=(n,),
        in_specs=[
            pl.BlockSpec((None, rows, width), lambda i: (i, 0, 0)),
            pl.BlockSpec((None, 2 * width, HEAD_DIM), lambda i: (i // NSA_KV_HEADS, 0, 0)),
            pl.BlockSpec((None, 1, 2 * width), lambda i: (i // NSA_KV_HEADS, 0, 0)),
            pl.BlockSpec((None, HEAD_DIM, HEAD_DIM), lambda i: (i // NSA_KV_HEADS, 0, 0)),
        ],
        out_specs=pl.BlockSpec((None, rows, HEAD_DIM), lambda i: (i, 0, 0)),
        out_shape=jax.ShapeDtypeStruct((n, rows, HEAD_DIM), BF16),
        compiler_params=_cparams("arbitrary"),
        name="nsa_compress",
    )(a, w1, pos, w2)


def _nsa_cmp_kernel(q_ref, kc_ref, vc_ref, ovt_ref, ocmp_ref, sel_ref):
    qi = pl.program_id(0)
    tq = q_ref.shape[0]
    ncp = kc_ref.shape[1]
    gw = GROUP * HEAD_DIM
    t = qi * tq + lax.broadcasted_iota(jnp.int32, (tq, ncp), 0)
    n = lax.broadcasted_iota(jnp.int32, (tq, ncp), 1)
    vis = (n * NSA_CMP_STRIDE + (NSA_CMP_LEN - 1)) <= t
    ones = jnp.ones((ncp, LANES), BF16)
    blk = lax.broadcasted_iota(jnp.int32, (LANES, tq), 0)
    blkf = blk.astype(F32)
    cur = (qi * tq + lax.broadcasted_iota(jnp.int32, (LANES, tq), 1)) >> SLC_SHIFT
    forced = (blk == 0) | (blk == cur) | (blk == cur - 1)
    future = blk > cur
    ovt = ovt_ref[...]

    for h in range(NSA_KV_HEADS):
        kc = kc_ref[h]
        va = jnp.concatenate([vc_ref[h], ones], axis=1)
        psum = jnp.zeros((tq, ncp), F32)
        for g in range(GROUP):
            cols = slice(h * gw + g * HEAD_DIM, h * gw + (g + 1) * HEAD_DIM)
            s = jnp.where(vis, _dot_t(q_ref[:, cols], kc), NEG_INF)
            chunks = [s[:, c * LANES:(c + 1) * LANES] for c in range(ncp // LANES)]
            m = jnp.max(functools.reduce(jnp.maximum, chunks), axis=1, keepdims=True)
            p = jnp.where(vis, jnp.exp2(s - m), 0.0)
            pv = _dot(p.astype(BF16), va)
            inv = 1.0 / jnp.maximum(pv[:, HEAD_DIM:], 1e-30)
            ocmp_ref[:, cols] = pv[:, :HEAD_DIM] * inv
            psum = psum + p * jnp.concatenate([inv] * (ncp // LANES), axis=1)

        p_hi = psum.astype(BF16)
        p_lo = (psum - p_hi.astype(F32)).astype(BF16)
        imp = _dot_t(ovt, p_hi) + _dot_t(ovt, p_lo)
        imp = jnp.where(forced, jnp.inf, imp)
        imp = jnp.where(future, -jnp.inf, imp)
        sel = jnp.zeros(imp.shape, F32)
        for _ in range(NSA_SLC_TOPK):
            mx = jnp.max(imp, axis=0, keepdims=True)
            idx = jnp.min(jnp.where(imp == mx, blkf, float(LANES)), axis=0, keepdims=True)
            pick = blkf == idx
            sel = jnp.where(pick, 1.0, sel)
            imp = jnp.where(pick, -jnp.inf, imp)
        sel_ref[h] = jnp.where(sel > 0.5, 0.0, NEG_INF).T.astype(sel_ref.dtype)


def _nsa_cmp_select(proj, kv_c, overlap_t, q_cb):
    s = proj.shape[0]
    tq = BAND_BLOCK
    nq = NSA_KV_HEADS * GROUP
    ncp = kv_c.shape[1]
    assert q_cb % nq == 0 and tq == LANES
    return pl.pallas_call(
        _nsa_cmp_kernel,
        grid=(s // tq,),
        in_specs=[
            pl.BlockSpec((tq, nq * HEAD_DIM), lambda i: (i, q_cb // nq)),
            pl.BlockSpec((NSA_KV_HEADS, ncp, HEAD_DIM), lambda i: (0, 0, 0)),
            pl.BlockSpec((NSA_KV_HEADS, ncp, HEAD_DIM), lambda i: (1, 0, 0)),
            pl.BlockSpec((LANES, ncp), lambda i: (0, 0)),
        ],
        out_specs=[
            pl.BlockSpec((tq, nq * HEAD_DIM), lambda i: (i, 0)),
            pl.BlockSpec((NSA_KV_HEADS, tq, LANES), lambda i: (0, i, 0)),
        ],
        out_shape=[
            jax.ShapeDtypeStruct((s, nq * HEAD_DIM), F32),
            jax.ShapeDtypeStruct((NSA_KV_HEADS, s, LANES), BF16),
        ],
        compiler_params=_cparams("arbitrary"),
        name="nsa_cmp_select",
    )(proj, kv_c, kv_c, overlap_t)


def _nsa_slc_kernel(q_ref, k_ref, kx_ref, v_ref, sel_ref, o_ref, qa_ref, *state, tk):
    qi = pl.program_id(0)
    tq = q_ref.shape[0]
    rows = GROUP * tq
    gw = GROUP * HEAD_DIM
    for h in range(NSA_KV_HEADS):
        selb = sel_ref[h]
        for g in range(GROUP):
            qa_ref[h, g * tq:(g + 1) * tq, 0:HEAD_DIM] = q_ref[:, h * gw + g * HEAD_DIM:h * gw + (g + 1) * HEAD_DIM]
            qa_ref[h, g * tq:(g + 1) * tq, HEAD_DIM:] = selb
    qas = [qa_ref[h] for h in range(NSA_KV_HEADS)]

    last = jnp.right_shift(qi, _log2(tk // tq))
    t = qi * tq + (lax.broadcasted_iota(jnp.int32, (rows, tk), 0) & (tq - 1))
    kpos = last * tk + lax.broadcasted_iota(jnp.int32, (rows, tk), 1)
    outs = _flash_attention(qas, k_ref, kx_ref, v_ref, state, last, tk, kpos <= t)
    for h in range(NSA_KV_HEADS):
        for g in range(GROUP):
            o_ref[:, h * gw + g * HEAD_DIM:h * gw + (g + 1) * HEAD_DIM] = outs[h][g * tq:(g + 1) * tq, :]


def _nsa_selected(proj, selb, q_cb, k_cb, v_cb):
    s = proj.shape[0]
    tq = 2 * BAND_BLOCK
    tk = min(FLASH_TILE, s)
    nq = NSA_KV_HEADS * GROUP
    kvw = NSA_KV_HEADS * HEAD_DIM
    assert tq & (tq - 1) == 0 and s % tk == 0 and q_cb % nq == 0 and k_cb % NSA_KV_HEADS == v_cb % NSA_KV_HEADS == 0
    return pl.pallas_call(
        functools.partial(_nsa_slc_kernel, tk=tk),
        grid=(s // tq,),
        in_specs=[
            pl.BlockSpec((tq, nq * HEAD_DIM), lambda i: (i, q_cb // nq)),
            _resident((s, kvw), lambda i: (0, k_cb // NSA_KV_HEADS)),
            _resident((s, LANES), lambda i: (0, 0)),
            _resident((s, kvw), lambda i: (0, v_cb // NSA_KV_HEADS)),
            pl.BlockSpec((NSA_KV_HEADS, tq, LANES), lambda i: (0, i, 0)),
        ],
        out_specs=pl.BlockSpec((tq, nq * HEAD_DIM), lambda i: (i, 0)),
        out_shape=jax.ShapeDtypeStruct((s, nq * HEAD_DIM), F32),
        scratch_shapes=[pltpu.VMEM((NSA_KV_HEADS, GROUP * tq, 2 * HEAD_DIM), BF16)]
        + _flash_state(NSA_KV_HEADS, GROUP * tq, tk),
        compiler_params=_cparams("arbitrary"),
        name="nsa_selected",
    )(proj, proj, _block_onehot(s, NSA_SLC_BLOCK), proj, selb)


def _nsa_win_kernel(q_ref, k_ref, v_ref, gate_ref, ocmp_ref, oslc_ref, o_ref, qs_ref, *, gate_lane):
    qi = pl.program_id(0)
    tq = q_ref.shape[0]
    gw = GROUP * HEAD_DIM
    rows = GROUP * tq
    n_prev = -(-(NSA_WINDOW - 1) // tq)
    row = lax.broadcasted_iota(jnp.int32, (rows, tq), 0) & (tq - 1)
    col = lax.broadcasted_iota(jnp.int32, (rows, tq), 1)
    ones = jnp.ones((tq, LANES), BF16)
    for h in range(NSA_KV_HEADS):
        for g in range(GROUP):
            qs_ref[h, g * tq:(g + 1) * tq, :] = q_ref[:, h * gw + g * HEAD_DIM:h * gw + (g + 1) * HEAD_DIM]
        q = qs_ref[h]
        cols = slice(h * HEAD_DIM, (h + 1) * HEAD_DIM)
        scores, starts = [], []
        for b in range(n_prev + 1):
            kb = qi - b
            start = pl.multiple_of(jnp.maximum(kb, 0) * tq, tq)
            s = _dot_t(q, k_ref[pl.ds(start, tq), cols])
            dist = row - col + b * tq
            if b * tq - (tq - 1) < 0:
                s = jnp.where(dist >= 0, s, NEG_INF)
            if b * tq + (tq - 1) >= NSA_WINDOW:
                s = jnp.where(dist < NSA_WINDOW, s, NEG_INF)
            if b > 0:
                s = s + jnp.where(kb >= 0, 0.0, NEG_INF)
            scores.append(s)
            starts.append(start)
        m = jnp.max(functools.reduce(jnp.maximum, scores), axis=1, keepdims=True)
        pv = None
        for s, start in zip(scores, starts):
            va = jnp.concatenate([v_ref[pl.ds(start, tq), cols], ones], axis=1)
            term = _dot(jnp.exp2(s - m).astype(BF16), va)
            pv = term if pv is None else pv + term
        o_win = pv[:, :HEAD_DIM] / pv[:, HEAD_DIM:]
        sig = jax.nn.sigmoid(gate_ref[...])
        for g in range(GROUP):
            oc = slice(h * gw + g * HEAD_DIM, h * gw + (g + 1) * HEAD_DIM)
            c = gate_lane + 3 * (h * GROUP + g)
            o = (sig[:, c:c + 1] * ocmp_ref[:, oc]
                 + sig[:, c + 1:c + 2] * oslc_ref[:, oc]
                 + sig[:, c + 2:c + 3] * o_win[g * tq:(g + 1) * tq, :])
            o_ref[:, oc] = o.astype(o_ref.dtype)


def _nsa_window_merge(proj, aux, gate_lane, o_cmp, o_slc, q_cb, k_cb, v_cb):
    s = proj.shape[0]
    tq = BAND_BLOCK
    nq = NSA_KV_HEADS * GROUP
    kvw = NSA_KV_HEADS * HEAD_DIM
    assert tq & (tq - 1) == 0 and q_cb % nq == 0 and k_cb % NSA_KV_HEADS == v_cb % NSA_KV_HEADS == 0
    wide = pl.BlockSpec((tq, nq * HEAD_DIM), lambda i: (i, 0))
    return pl.pallas_call(
        functools.partial(_nsa_win_kernel, gate_lane=gate_lane),
        grid=(s // tq,),
        in_specs=[
            pl.BlockSpec((tq, nq * HEAD_DIM), lambda i: (i, q_cb // nq)),
            _resident((s, kvw), lambda i: (0, k_cb // NSA_KV_HEADS)),
            _resident((s, kvw), lambda i: (0, v_cb // NSA_KV_HEADS)),
            pl.BlockSpec((tq, LANES), lambda i: (i, 0)),
            wide,
            wide,
        ],
        out_specs=wide,
        out_shape=jax.ShapeDtypeStruct((s, nq * HEAD_DIM), BF16),
        scratch_shapes=[pltpu.VMEM((NSA_KV_HEADS, GROUP * tq, HEAD_DIM), BF16)],
        compiler_params=_cparams("arbitrary"),
        name="nsa_window_merge",
    )(proj, proj, proj, aux, o_cmp, o_slc)


def _overlap_matrix(seq):
    n_pad = seq // NSA_CMP_STRIDE
    n_cmp = (seq - NSA_CMP_LEN) // NSA_CMP_STRIDE + 1
    c_start = np.arange(n_pad)[:, None] * NSA_CMP_STRIDE
    s_start = np.arange(LANES)[None, :] * NSA_SLC_BLOCK
    ov = (c_start < s_start + NSA_SLC_BLOCK) & (c_start + NSA_CMP_LEN > s_start)
    ov &= (np.arange(n_pad)[:, None] < n_cmp) & (np.arange(LANES)[None, :] < seq // NSA_SLC_BLOCK)
    return jnp.asarray(ov.T.astype(np.float32), dtype=BF16)


def _ctypes(spec):
    return jnp.asarray(np.concatenate([np.full(n, kind, np.int32) for n, kind in spec]))


def _even_layer(x, mods, norm_g, w_in, sinks, w_out, layer, tables):
    sh1, sc1, g1 = mods
    heads = w_in.shape[2] // HEAD_DIM
    n_moba = 8
    ctypes = _ctypes([(n_moba, CT_ROPE_SCALE), (n_moba, CT_ROPE), (n_moba, CT_PLAIN),
                      (8, CT_ROPE_SCALE), (SWA_KV_HEADS, CT_ROPE), (SWA_KV_HEADS, CT_PLAIN)])
    assert ctypes.shape[0] == heads
    proj = _norm_proj(x, norm_g, sc1, sh1, w_in, ctypes, tables, 512, layer=layer)
    nb = x.shape[0] // MOBA_BLOCK
    kmean = _moba_kmean(proj, 1, n_moba * HEAD_DIM)
    kmean = jnp.pad(kmean, ((0, LANES - nb), (0, 0))).astype(BF16)
    oa = _moba_attention(proj, kmean, n_moba, 0, n_moba, 2 * n_moba)
    ob = _swa_attention(proj, sinks, 3 * n_moba, 4 * n_moba, 4 * n_moba + SWA_KV_HEADS)
    return _out_proj(oa, ob, w_out[layer].astype(BF16), x, g1)


def _odd_layer(x, mods, norm_g, w_in, forget_b, k_pos, k_w1, k_w2, v_pos, v_w1, v_w2, w_out, layer, tables, overlap):
    sh1, sc1, g1 = mods
    s = x.shape[0]
    n_fox = 8
    hw = n_fox * HEAD_DIM
    kvw = NSA_KV_HEADS * HEAD_DIM
    o_fc = 3 * hw
    o_qd = o_fc + n_fox
    o_gd = o_qd + hw + 6 * kvw
    w_main = jnp.concatenate([w_in[:, :o_fc], w_in[:, o_qd:o_gd]], axis=1).astype(BF16)
    n_aux = n_fox + 3 * 8
    w_aux = jnp.concatenate([w_in[:, o_fc:o_qd], w_in[:, o_gd:], jnp.zeros((w_in.shape[0], LANES - n_aux), F32)],
                            axis=1).astype(BF16)
    ctypes = _ctypes([(8, CT_SCALE), (8, CT_PLAIN), (8, CT_PLAIN), (8, CT_ROPE_SCALE),
                      (2, CT_ROPE), (2, CT_PLAIN), (2, CT_ROPE), (2, CT_PLAIN), (2, CT_ROPE), (2, CT_PLAIN)])
    proj, aux = _norm_proj(x, norm_g, sc1, sh1, w_main, ctypes, tables, 512, w_aux=w_aux)

    bias_row = jnp.pad(forget_b.astype(F32), (0, LANES - n_fox)).reshape(1, LANES)
    fox_qx, fox_kx = _fox_cum(aux, bias_row, n_fox)
    oc = _fox_attention(proj, fox_qx, fox_kx, n_fox, 0, 8, 16)

    c0 = 32
    cmp_in = proj[:, c0 * HEAD_DIM:(c0 + 4) * HEAD_DIM].reshape(s, 4, HEAD_DIM).transpose(1, 0, 2)
    cmp_in = cmp_in.reshape(4, s // NSA_CMP_STRIDE, NSA_CMP_STRIDE * HEAD_DIM)
    kv_c = _nsa_compress(
        cmp_in,
        jnp.stack([k_w1, v_w1]).astype(BF16),
        jnp.stack([k_pos.reshape(1, -1), v_pos.reshape(1, -1)]).astype(BF16),
        jnp.stack([k_w2, v_w2]).astype(BF16))
    o_cmp, sel = _nsa_cmp_select(proj, kv_c, overlap, 24)
    o_slc = _nsa_selected(proj, sel, 24, c0 + 4, c0 + 6)
    od = _nsa_window_merge(proj, aux, n_fox, o_cmp, o_slc, 24, c0 + 8, c0 + 10)
    return _out_proj(oc, od, w_out[layer].astype(BF16), x, g1)


def kernel(x, c, norm_mix_g, norm_mlp_g, ada_w, ada_b, mlp_up, mlp_down, even_w_in, even_sinks, even_w_out, odd_w_in, fox_forget_b, nsa_k_pos, nsa_k_w1, nsa_k_w2, nsa_v_pos, nsa_v_w1, nsa_v_w2, odd_w_out, final_norm_g):
    batch, seq, d = x.shape
    assert batch == 1
    depth = ada_w.shape[0]
    tables = _epilogue_tables(seq)
    overlap = _overlap_matrix(seq)
    mod = _ada_mod(c, ada_w, ada_b).reshape(depth, 6, 1, d)
    xs = x[0]
    for i in range(depth):
        sh1, sc1, g1, sh2, sc2, g2 = [mod[i, t] for t in range(6)]
        ng = norm_mix_g[i].reshape(1, d)
        j = i // 2
        if i % 2 == 0:
            xs = _even_layer(xs, (sh1, sc1, g1), ng, even_w_in, even_sinks[j], even_w_out, j, tables)
        else:
            xs = _odd_layer(xs, (sh1, sc1, g1), ng, odd_w_in[j], fox_forget_b[j], nsa_k_pos[j], nsa_k_w1[j],
                            nsa_k_w2[j], nsa_v_pos[j], nsa_v_w1[j], nsa_v_w2[j], odd_w_out, j, tables, overlap)
        xs = _mlp(xs, norm_mlp_g[i].reshape(1, d), sc2, sh2, mlp_up, mlp_down, i, g2,
                  final_norm_g.reshape(1, d), final=(i == depth - 1))
    return xs[None]
```

```python
import functools
import math

import numpy as np
import jax
import jax.numpy as jnp
from jax import lax
from jax.experimental import pallas as pl
from jax.experimental.pallas import tpu as pltpu

HEAD_DIM = 128
ROPE_THETA = 10000.0
NORM_EPS = 1e-6
MOBA_BLOCK = 256
MOBA_TOPK = 3
SWA_KV_HEADS = 2
SWA_WINDOW = 128
NSA_KV_HEADS = 2
NSA_CMP_LEN = 32
NSA_CMP_STRIDE = 16
NSA_SLC_BLOCK = 64
SLC_SHIFT = 6
NSA_SLC_TOPK = 16
NSA_WINDOW = 512
BAND_BLOCK = 128
NEG_INF = -1e30
LOG2E = math.log2(math.e)
SCALE = HEAD_DIM ** -0.5 * LOG2E
LANES = 128
GROUP = 4
VMEM_LIMIT = 48 * 1024 * 1024

F32 = jnp.float32
BF16 = jnp.bfloat16

CT_PLAIN, CT_ROPE, CT_ROPE_SCALE, CT_SCALE = 0, 1, 2, 3


VMEM_LIMIT_MLP = 56 * 1024 * 1024


def _cparams(*sem, vmem=VMEM_LIMIT):
    return pltpu.CompilerParams(dimension_semantics=sem, vmem_limit_bytes=vmem)


def _dot(a, b):
    return jnp.dot(a, b, preferred_element_type=F32)


def _dot_t(a, b):
    return lax.dot_general(a, b, (((1,), (1,)), ((), ())), preferred_element_type=F32)


def _ada_kernel(c_ref, w_ref, b_ref, o_ref):
    c = c_ref[...]
    cond = c * jax.nn.sigmoid(c)
    cond8 = jnp.broadcast_to(cond, (8, cond.shape[1])).astype(BF16)
    y = _dot(cond8, w_ref[...].astype(BF16))
    o_ref[...] = y[0:1] + b_ref[...]


def _ada_mod(c, ada_w, ada_b):
    depth, d, n = ada_w.shape
    tn = 1024
    return pl.pallas_call(
        _ada_kernel,
        grid=(depth, n // tn),
        in_specs=[
            pl.BlockSpec((1, d), lambda l, j: (0, 0)),
            pl.BlockSpec((None, d, tn), lambda l, j: (l, 0, j)),
            pl.BlockSpec((None, 1, tn), lambda l, j: (l, 0, j)),
        ],
        out_specs=pl.BlockSpec((None, 1, tn), lambda l, j: (l, 0, j)),
        out_shape=jax.ShapeDtypeStruct((depth, 1, n), F32),
        compiler_params=_cparams("arbitrary", "arbitrary"),
        name="ada_mod",
    )(c, ada_w, ada_b.reshape(depth, 1, n))


def _norm_mod(x, g, sc, sh):
    y = x * lax.rsqrt(jnp.mean(x * x, axis=-1, keepdims=True) + NORM_EPS)
    return (y * g) * (1.0 + sc) + sh


PROJ_ROW_CHUNKS = 2


def _proj_kernel(ct_ref, x_ref, g_ref, sc_ref, sh_ref, w_ref, ta_ref, tb_ref, *rest, aux):
    if aux:
        wa_ref, o_ref, oa_ref, h_ref = rest
    else:
        o_ref, h_ref = rest
    j = pl.program_id(1)

    @pl.when(j == 0)
    def _():
        h_ref[...] = _norm_mod(x_ref[...], g_ref[...], sc_ref[...], sh_ref[...]).astype(BF16)
        if aux:
            oa_ref[...] = _dot(h_ref[...], wa_ref[...])

    w = w_ref[...].astype(BF16)
    groups = w.shape[1] // LANES
    tm = h_ref.shape[0]
    chunk = tm // PROJ_ROW_CHUNKS
    for ri in range(PROJ_ROW_CHUNKS):
        rows = slice(ri * chunk, (ri + 1) * chunk)
        y = _dot(h_ref[rows, :], w)
        for gi in range(groups):
            lo, hi = gi * LANES, (gi + 1) * LANES
            yg = y[:, lo:hi]
            ct = ct_ref[j * groups + gi]
            r = yg * ta_ref[ct, rows, :] + pltpu.roll(yg, HEAD_DIM // 2, 1) * tb_ref[ct, rows, :]
            o_ref[rows, lo:hi] = r.astype(o_ref.dtype)


def _epilogue_tables(seq):
    f32 = np.float32
    inv = (f32(1.0) / (f32(ROPE_THETA) ** (np.arange(0, HEAD_DIM, 2, dtype=f32) / f32(HEAD_DIM)))).astype(f32)
    ang = np.arange(seq, dtype=f32)[:, None] * inv[None, :]
    cos, sin = np.cos(ang), np.sin(ang)
    cosf = np.concatenate([cos, cos], axis=-1)
    sinf = np.concatenate([-sin, sin], axis=-1)
    one, zero = np.ones_like(cosf), np.zeros_like(cosf)
    by_kind = {CT_PLAIN: (one, zero), CT_ROPE: (cosf, sinf), CT_ROPE_SCALE: (cosf * f32(SCALE), sinf * f32(SCALE)),
               CT_SCALE: (one * f32(SCALE), zero)}
    return (jnp.asarray(np.stack([by_kind[k][0] for k in range(4)])),
            jnp.asarray(np.stack([by_kind[k][1] for k in range(4)])))


def _norm_proj(x, g, sc, sh, w, ctypes, tables, tn, layer=None, w_aux=None):
    s, d = x.shape
    n = w.shape[-1]
    tm = min(1024, s)
    kinds = tables[0].shape[0]
    if layer is None:
        w_spec = pl.BlockSpec((d, tn), lambda i, j, ct: (0, j))
    else:
        w_spec = pl.BlockSpec((None, d, tn), lambda i, j, ct: (layer, 0, j))
    aux = w_aux is not None
    in_specs = [
        pl.BlockSpec((tm, d), lambda i, j, ct: (i, 0)),
        pl.BlockSpec((1, d), lambda i, j, ct: (0, 0)),
        pl.BlockSpec((1, d), lambda i, j, ct: (0, 0)),
        pl.BlockSpec((1, d), lambda i, j, ct: (0, 0)),
        w_spec,
        pl.BlockSpec((kinds, tm, LANES), lambda i, j, ct: (0, i, 0)),
        pl.BlockSpec((kinds, tm, LANES), lambda i, j, ct: (0, i, 0)),
    ]
    out_specs = pl.BlockSpec((tm, tn), lambda i, j, ct: (i, j))
    out_shape = jax.ShapeDtypeStruct((s, n), BF16)
    operands = [ctypes, x, g, sc, sh, w, *tables]
    if aux:
        in_specs.append(pl.BlockSpec((d, LANES), lambda i, j, ct: (0, 0)))
        out_specs = [out_specs, pl.BlockSpec((tm, LANES), lambda i, j, ct: (i, 0))]
        out_shape = [out_shape, jax.ShapeDtypeStruct((s, LANES), F32)]
        operands.append(w_aux)
    grid_spec = pltpu.PrefetchScalarGridSpec(
        num_scalar_prefetch=1,
        grid=(s // tm, n // tn),
        in_specs=in_specs,
        out_specs=out_specs,
        scratch_shapes=[pltpu.VMEM((tm, d), BF16)],
    )
    return pl.pallas_call(
        functools.partial(_proj_kernel, aux=aux),
        grid_spec=grid_spec,
        out_shape=out_shape,
        compiler_params=_cparams("arbitrary", "arbitrary"),
        name="norm_proj",
    )(*operands)


def _outproj_kernel(a_ref, b_ref, w_ref, x_ref, g_ref, o_ref):
    ha = a_ref.shape[1]
    y = _dot(a_ref[...], w_ref[0:ha, :]) + _dot(b_ref[...], w_ref[ha:, :])
    o_ref[...] = x_ref[...] + g_ref[...] * y


def _out_proj(oa, ob, w, x, gate):
    s, d = x.shape
    ha, hb = oa.shape[1], ob.shape[1]
    tm = min(512, s)
    return pl.pallas_call(
        _outproj_kernel,
        grid=(s // tm,),
        in_specs=[
            pl.BlockSpec((tm, ha), lambda i: (i, 0)),
            pl.BlockSpec((tm, hb), lambda i: (i, 0)),
            _resident((ha + hb, d), lambda i: (0, 0)),
            pl.BlockSpec((tm, d), lambda i: (i, 0)),
            pl.BlockSpec((1, d), lambda i: (0, 0)),
        ],
        out_specs=pl.BlockSpec((tm, d), lambda i: (i, 0)),
        out_shape=jax.ShapeDtypeStruct((s, d), F32),
        compiler_params=_cparams("arbitrary"),
        name="out_proj",
    )(oa, ob, w, x, gate)


def _rms_norm(x, g):
    return x * lax.rsqrt(jnp.mean(x * x, axis=-1, keepdims=True) + NORM_EPS) * g


def _mlp_kernel(x_ref, g_ref, sc_ref, sh_ref, up_ref, down_ref, gate_ref, fg_ref, o_ref, h_ref, *, final):
    f = pl.program_id(1)

    @pl.when(f == 0)
    def _():
        h_ref[...] = _norm_mod(x_ref[...], g_ref[...], sc_ref[...], sh_ref[...]).astype(BF16)
        o_ref[...] = jnp.zeros_like(o_ref)

    hid = jnp.maximum(_dot(h_ref[...], up_ref[...].astype(BF16)), 0.0)
    o_ref[...] += _dot((hid * hid).astype(BF16), down_ref[...].astype(BF16))

    @pl.when(f == pl.num_programs(1) - 1)
    def _():
        y = x_ref[...] + gate_ref[...] * o_ref[...]
        o_ref[...] = _rms_norm(y, fg_ref[...]) if final else y


def _mlp(x, g, sc, sh, up, down, layer, gate, final_g, final):
    s, d = x.shape
    ff = up.shape[2]
    tm, tf = min(1024, s), 512
    return pl.pallas_call(
        functools.partial(_mlp_kernel, final=final),
        grid=(s // tm, ff // tf),
        in_specs=[
            pl.BlockSpec((tm, d), lambda i, f: (i, 0)),
            pl.BlockSpec((1, d), lambda i, f: (0, 0)),
            pl.BlockSpec((1, d), lambda i, f: (0, 0)),
            pl.BlockSpec((1, d), lambda i, f: (0, 0)),
            pl.BlockSpec((None, d, tf), lambda i, f: (layer, 0, f)),
            pl.BlockSpec((None, tf, d), lambda i, f: (layer, f, 0)),
            pl.BlockSpec((1, d), lambda i, f: (0, 0)),
            pl.BlockSpec((1, d), lambda i, f: (0, 0)),
        ],
        out_specs=pl.BlockSpec((tm, d), lambda i, f: (i, 0), pipeline_mode=pl.Buffered(1)),
        out_shape=jax.ShapeDtypeStruct((s, d), F32),
        scratch_shapes=[pltpu.VMEM((tm, d), BF16)],
        compiler_params=_cparams("arbitrary", "arbitrary", vmem=VMEM_LIMIT_MLP),
        name="mlp",
    )(x, g, sc, sh, up, down, gate, final_g)


def _rows(ref, start, size):
    return ref[pl.ds(pl.multiple_of(start, size), size), :]


def _log2(n):
    assert n > 0 and n & (n - 1) == 0
    return n.bit_length() - 1


FLASH_TILE = 512
HEADS_PER_STEP = 4


def _resident(shape, index_map):
    return pl.BlockSpec(shape, index_map, pipeline_mode=pl.Buffered(1))


def _flash_attention(qas, k_ref, kx_ref, v_ref, state, last, tk, diag_mask):
    m_ref, alpha_ref, p_ref, acc_ref = state
    chains = range(len(qas))
    ones = jnp.ones((tk, LANES), BF16)

    def score(start, mask, first):
        rows = pl.ds(pl.multiple_of(start, tk), tk)
        for c in chains:
            ka = jnp.concatenate([k_ref[rows, c * HEAD_DIM:(c + 1) * HEAD_DIM], kx_ref[rows, :]], axis=1)
            s = _dot_t(qas[c], ka)
            if mask is not None:
                s = jnp.where(mask, s, NEG_INF)
            chunks = [s[:, i * LANES:(i + 1) * LANES] for i in range(tk // LANES)]
            m_cur = jnp.max(functools.reduce(jnp.maximum, chunks), axis=1, keepdims=True)
            if first:
                m_new = jnp.broadcast_to(m_cur, (s.shape[0], LANES))
            else:
                m_prev = m_ref[c]
                m_new = jnp.maximum(m_prev, m_cur)
                alpha_ref[c] = jnp.exp2(m_prev - m_new)
            p_ref[c] = jnp.concatenate([jnp.exp2(ch - m_new) for ch in chunks], axis=1).astype(BF16)
            m_ref[c] = m_new

    def accumulate(start):
        rows = pl.ds(pl.multiple_of(start, tk), tk)
        for c in chains:
            va = jnp.concatenate([v_ref[rows, c * HEAD_DIM:(c + 1) * HEAD_DIM], ones], axis=1)
            alpha = alpha_ref[c]
            acc_ref[c] = jnp.concatenate([alpha, alpha], axis=1) * acc_ref[c] + _dot(p_ref[c], va)

    for c in chains:
        acc_ref[c] = jnp.zeros(acc_ref.shape[1:], F32)
        alpha_ref[c] = jnp.ones(alpha_ref.shape[1:], F32)
    score(last * tk, diag_mask, True)

    def body(p, carry):
        accumulate(jnp.where(p == 0, last, p - 1) * tk)
        score(p * tk, None, False)
        return carry

    lax.fori_loop(0, last, body, 0)
    accumulate(jnp.where(last == 0, last, last - 1) * tk)
    return [acc_ref[c][:, :HEAD_DIM] / acc_ref[c][:, HEAD_DIM:] for c in chains]


def _flash_state(chains, rows, tk):
    return [pltpu.VMEM((chains, rows, LANES), F32), pltpu.VMEM((chains, rows, LANES), F32),
            pltpu.VMEM((chains, rows, tk), BF16), pltpu.VMEM((chains, rows, 2 * HEAD_DIM), F32)]


def _block_onehot(seq, block):
    return jnp.asarray((np.arange(seq)[:, None] // block == np.arange(LANES)[None, :]).astype(np.float32), dtype=BF16)


def _kmean_kernel(k_ref, o_ref):
    k = k_ref[...].astype(F32)
    o_ref[...] = jnp.sum(k, axis=0, keepdims=True) * (1.0 / k.shape[0])


def _moba_kmean(proj, col_block, width):
    s = proj.shape[0]
    nb = s // MOBA_BLOCK
    out = pl.pallas_call(
        _kmean_kernel,
        grid=(nb,),
        in_specs=[pl.BlockSpec((MOBA_BLOCK, width), lambda b: (b, col_block))],
        out_specs=pl.BlockSpec((None, 1, width), lambda b: (b, 0, 0)),
        out_shape=jax.ShapeDtypeStruct((nb, 1, width), F32),
        compiler_params=_cparams("arbitrary"),
        name="moba_kmean",
    )(proj)
    return out.reshape(nb, width)


def _moba_kernel(q_ref, k_ref, kx_ref, v_ref, km_ref, o_ref, *state, tk, hp):
    qi = pl.program_id(1)
    tq = q_ref.shape[0]
    blk = lax.broadcasted_iota(jnp.int32, (LANES, tq), 0)
    blkf = blk.astype(F32)
    own = jnp.right_shift(qi * tq + lax.broadcasted_iota(jnp.int32, (LANES, tq), 1), _log2(MOBA_BLOCK))
    ownf = own.astype(F32)

    qas = []
    for hh in range(hp):
        cols = slice(hh * HEAD_DIM, (hh + 1) * HEAD_DIM)
        q = q_ref[:, cols]
        g = jnp.where(blk < own, _dot_t(km_ref[:, cols], q), NEG_INF)
        sel = jnp.zeros((LANES, tq), F32)
        for _ in range(MOBA_TOPK):
            mx = jnp.max(g, axis=0, keepdims=True)
            idx = jnp.min(jnp.where(g == mx, blkf, float(LANES)), axis=0, keepdims=True)
            pick = blkf == idx
            sel = jnp.where(pick & (idx < ownf), 1.0, sel)
            g = jnp.where(pick, -jnp.inf, g)
        qx = jnp.where((sel > 0.5) | (blk == own), 0.0, NEG_INF).T.astype(BF16)
        qas.append(jnp.concatenate([q, qx], axis=1))

    last = jnp.right_shift(qi, _log2(tk // tq))
    t = qi * tq + lax.broadcasted_iota(jnp.int32, (tq, tk), 0)
    kpos = last * tk + lax.broadcasted_iota(jnp.int32, (tq, tk), 1)
    outs = _flash_attention(qas, k_ref, kx_ref, v_ref, state, last, tk, kpos <= t)
    for hh in range(hp):
        o_ref[:, hh * HEAD_DIM:(hh + 1) * HEAD_DIM] = outs[hh].astype(o_ref.dtype)


def _moba_attention(proj, kmean, n_heads, q_cb, k_cb, v_cb):
    s = proj.shape[0]
    tq = tk = min(FLASH_TILE, s)
    hp = HEADS_PER_STEP
    hw = hp * HEAD_DIM
    assert s % tk == 0 and tk % MOBA_BLOCK == 0 and n_heads % hp == 0 and q_cb % hp == k_cb % hp == v_cb % hp == 0
    return pl.pallas_call(
        functools.partial(_moba_kernel, tk=tk, hp=hp),
        grid=(n_heads // hp, s // tq),
        in_specs=[
            pl.BlockSpec((tq, hw), lambda h, i: (i, q_cb // hp + h)),
            _resident((s, hw), lambda h, i: (0, k_cb // hp + h)),
            _resident((s, LANES), lambda h, i: (0, 0)),
            _resident((s, hw), lambda h, i: (0, v_cb // hp + h)),
            pl.BlockSpec((LANES, hw), lambda h, i: (0, h)),
        ],
        out_specs=pl.BlockSpec((tq, hw), lambda h, i: (i, h)),
        out_shape=jax.ShapeDtypeStruct((s, n_heads * HEAD_DIM), BF16),
        scratch_shapes=_flash_state(hp, tq, tk),
        compiler_params=_cparams("arbitrary", "arbitrary"),
        name="moba_attn",
    )(proj, proj, _block_onehot(s, MOBA_BLOCK), proj, kmean)


def _swa_kernel(sink_ref, q_ref, k_ref, v_ref, o_ref, qs_ref):
    qi = pl.program_id(0)
    tq = q_ref.shape[0]
    tk = 2 * tq
    rows = GROUP * tq
    gw = GROUP * HEAD_DIM
    start = pl.multiple_of(jnp.maximum(qi - 1, 0) * tq, tq)
    t = qi * tq + (lax.broadcasted_iota(jnp.int32, (rows, tk), 0) & (tq - 1))
    kpos = start + lax.broadcasted_iota(jnp.int32, (rows, tk), 1)
    ok = (kpos <= t) & (kpos > t - SWA_WINDOW)
    ones = jnp.ones((tk, LANES), BF16)
    for h in range(SWA_KV_HEADS):
        for g in range(GROUP):
            qs_ref[h, g * tq:(g + 1) * tq, :] = q_ref[:, h * gw + g * HEAD_DIM:h * gw + (g + 1) * HEAD_DIM]
        cols = slice(h * HEAD_DIM, (h + 1) * HEAD_DIM)
        s = jnp.where(ok, _dot_t(qs_ref[h], k_ref[pl.ds(start, tk), cols]), NEG_INF)
        sink = jnp.concatenate([jnp.full((tq, 1), sink_ref[h * GROUP + g] * LOG2E, F32) for g in range(GROUP)],
                               axis=0)
        m = jnp.maximum(jnp.max(jnp.maximum(s[:, :tq], s[:, tq:]), axis=1, keepdims=True), sink)
        p = jnp.exp2(s - m)
        pv = _dot(p.astype(BF16), jnp.concatenate([v_ref[pl.ds(start, tk), cols], ones], axis=1))
        o = pv[:, :HEAD_DIM] / (pv[:, HEAD_DIM:] + jnp.exp2(sink - m))
        for g in range(GROUP):
            o_ref[:, h * gw + g * HEAD_DIM:h * gw + (g + 1) * HEAD_DIM] = o[g * tq:(g + 1) * tq, :].astype(o_ref.dtype)


def _swa_attention(proj, sinks, q_cb, k_cb, v_cb):
    s = proj.shape[0]
    tq = BAND_BLOCK
    nq = SWA_KV_HEADS * GROUP
    kvw = SWA_KV_HEADS * HEAD_DIM
    assert SWA_WINDOW <= tq and tq == LANES and q_cb % nq == 0 and k_cb % SWA_KV_HEADS == v_cb % SWA_KV_HEADS == 0
    return pl.pallas_call(
        _swa_kernel,
        grid=(s // tq,),
        in_specs=[
            pl.BlockSpec(memory_space=pltpu.SMEM),
            pl.BlockSpec((tq, nq * HEAD_DIM), lambda i: (i, q_cb // nq)),
            _resident((s, kvw), lambda i: (0, k_cb // SWA_KV_HEADS)),
            _resident((s, kvw), lambda i: (0, v_cb // SWA_KV_HEADS)),
        ],
        out_specs=pl.BlockSpec((tq, nq * HEAD_DIM), lambda i: (i, 0)),
        out_shape=jax.ShapeDtypeStruct((s, nq * HEAD_DIM), BF16),
        scratch_shapes=[pltpu.VMEM((SWA_KV_HEADS, GROUP * tq, HEAD_DIM), BF16)],
        compiler_params=_cparams("arbitrary"),
        name="swa_attn",
    )(sinks, proj, proj, proj)


FOX_LANES = 16


def _split3(x):
    x1 = x.astype(BF16)
    r1 = x - x1.astype(F32)
    x2 = r1.astype(BF16)
    return x1, x2, (r1 - x2.astype(F32)).astype(BF16)


def _fox_tables(n_heads):
    place = np.zeros((6, LANES, LANES), np.float32)
    ones_q = np.zeros((1, LANES), np.float32)
    ones_k = np.zeros((1, LANES), np.float32)
    for h in range(n_heads):
        for i in range(6):
            place[i, h, FOX_LANES * h + i] = 1.0
        ones_q[0, FOX_LANES * h + 3:FOX_LANES * h + 6] = 1.0
        ones_k[0, FOX_LANES * h:FOX_LANES * h + 3] = 1.0
    return jnp.asarray(place, dtype=BF16), jnp.asarray(ones_q), jnp.asarray(ones_k)


def _fox_cum_kernel(aux_ref, b_ref, place_ref, oq_ref, ok_ref, qx_ref, kx_ref, carry_ref):
    i = pl.program_id(0)

    @pl.when(i == 0)
    def _():
        carry_ref[...] = jnp.zeros_like(carry_ref)

    x = aux_ref[...] + b_ref[...]
    log_f = -(jnp.maximum(-x, 0.0) + jnp.log(1.0 + jnp.exp(-jnp.abs(x))))
    tb = x.shape[0]
    r = lax.broadcasted_iota(jnp.int32, (tb, tb), 0)
    c = lax.broadcasted_iota(jnp.int32, (tb, tb), 1)
    tri = jnp.where(c <= r, 1.0, 0.0).astype(BF16)
    x1, x2, x3 = _split3(log_f)
    cum = _dot(tri, x1) + _dot(tri, x2) + _dot(tri, x3) + carry_ref[...]
    carry_ref[...] = cum[tb - 1:tb, :]
    c1, c2, c3 = _split3(cum * LOG2E)
    qx = _dot(c1, place_ref[0]) + _dot(c2, place_ref[1]) + _dot(c3, place_ref[2]) + oq_ref[...]
    kx = ok_ref[...] - (_dot(c1, place_ref[3]) + _dot(c2, place_ref[4]) + _dot(c3, place_ref[5]))
    qx_ref[...] = qx.astype(BF16)
    kx_ref[...] = kx.astype(BF16)


def _fox_cum(aux, bias_row, n_heads):
    s, w = aux.shape
    tb = min(512, s)
    place, ones_q, ones_k = _fox_tables(n_heads)
    full = lambda shape: pl.BlockSpec(shape, lambda i: (0,) * len(shape))
    return pl.pallas_call(
        _fox_cum_kernel,
        grid=(s // tb,),
        in_specs=[pl.BlockSpec((tb, w), lambda i: (i, 0)), full((1, w)), full(place.shape), full((1, w)), full((1, w))],
        out_specs=[pl.BlockSpec((tb, w), lambda i: (i, 0)), pl.BlockSpec((tb, w), lambda i: (i, 0))],
        out_shape=[jax.ShapeDtypeStruct((s, w), BF16), jax.ShapeDtypeStruct((s, w), BF16)],
        scratch_shapes=[pltpu.VMEM((1, w), F32)],
        compiler_params=_cparams("arbitrary"),
        name="fox_cum",
    )(aux, bias_row, place, ones_q, ones_k)


def _fox_kernel(q_ref, qx_ref, k_ref, kx_ref, v_ref, o_ref, *state, tk, hp):
    hb = pl.program_id(0)
    qi = pl.program_id(1)
    tq = q_ref.shape[0]
    lane = lax.broadcasted_iota(jnp.int32, (tq, LANES), 1)
    qxf = qx_ref[...].astype(F32)
    qas = []
    for hh in range(hp):
        mine = jnp.right_shift(lane, 4) == hb * hp + hh
        qx = jnp.where(mine, qxf, 0.0).astype(BF16)
        qas.append(jnp.concatenate([q_ref[:, hh * HEAD_DIM:(hh + 1) * HEAD_DIM], qx], axis=1))

    last = jnp.right_shift(qi, _log2(tk // tq))
    t = qi * tq + lax.broadcasted_iota(jnp.int32, (tq, tk), 0)
    kpos = last * tk + lax.broadcasted_iota(jnp.int32, (tq, tk), 1)
    outs = _flash_attention(qas, k_ref, kx_ref, v_ref, state, last, tk, kpos <= t)
    for hh in range(hp):
        o_ref[:, hh * HEAD_DIM:(hh + 1) * HEAD_DIM] = outs[hh].astype(o_ref.dtype)


def _fox_attention(proj, qx, kx, n_heads, q_cb, k_cb, v_cb):
    s = proj.shape[0]
    tq = tk = min(FLASH_TILE, s)
    hp = HEADS_PER_STEP
    hw = hp * HEAD_DIM
    assert s % tk == 0 and FOX_LANES == 16 and n_heads % hp == 0 and q_cb % hp == k_cb % hp == v_cb % hp == 0
    return pl.pallas_call(
        functools.partial(_fox_kernel, tk=tk, hp=hp),
        grid=(n_heads // hp, s // tq),
        in_specs=[
            pl.BlockSpec((tq, hw), lambda h, i: (i, q_cb // hp + h)),
            pl.BlockSpec((tq, LANES), lambda h, i: (i, 0)),
            _resident((s, hw), lambda h, i: (0, k_cb // hp + h)),
            _resident((s, LANES), lambda h, i: (0, 0)),
            _resident((s, hw), lambda h, i: (0, v_cb // hp + h)),
        ],
        out_specs=pl.BlockSpec((tq, hw), lambda h, i: (i, h)),
        out_shape=jax.ShapeDtypeStruct((s, n_heads * HEAD_DIM), BF16),
        scratch_shapes=_flash_state(hp, tq, tk),
        compiler_params=_cparams("arbitrary", "arbitrary"),
        name="fox_attn",
    )(proj, qx, proj, kx, proj)


def _gelu_tanh(x):
    return 0.5 * x * (1.0 + jnp.tanh(math.sqrt(2.0 / math.pi) * (x + 0.044715 * (x * x * x))))


def _nsa_compress_kernel(a_ref, w1_ref, pos_ref, w2_ref, o_ref):
    a = a_ref[...]
    half = a.shape[1]
    n = a.shape[0]
    lo = _dot(a, w1_ref[0:half, :])
    hi = _dot(a, w1_ref[half:, :])
    pos8 = jnp.broadcast_to(pos_ref[...], (8, pos_ref.shape[1]))
    pterm = _dot(pos8, w1_ref[...])[0:1]
    pre = lo + pltpu.roll(hi, n - 1, 0) + pterm
    o_ref[...] = _dot(_gelu_tanh(pre).astype(BF16), w2_ref[...]).astype(o_ref.dtype)


def _nsa_compress(a, w1, pos, w2):
    n, rows, width = a.shape
    return pl.pallas_call(
        _nsa_compress_kernel,
        grid=(n,),
        in_specs=[
            pl.BlockSpec((None, rows, width), lambda i: (i, 0, 0)),
            pl.BlockSpec((None, 2 * width, HEAD_DIM), lambda i: (i // NSA_KV_HEADS, 0, 0)),
            pl.BlockSpec((None, 1, 2 * width), lambda i: (i // NSA_KV_HEADS, 0, 0)),
            pl.BlockSpec((None, HEAD_DIM, HEAD_DIM), lambda i: (i // NSA_KV_HEADS, 0, 0)),
        ],
        out_specs=pl.BlockSpec((None, rows, HEAD_DIM), lambda i: (i, 0, 0)),
        out_shape=jax.ShapeDtypeStruct((n, rows, HEAD_DIM), BF16),
        compiler_params=_cparams("arbitrary"),
        name="nsa_compress",
    )(a, w1, pos, w2)


def _nsa_cmp_kernel(q_ref, kc_ref, vc_ref, ovt_ref, ocmp_ref, sel_ref):
    qi = pl.program_id(0)
    tq = q_ref.shape[0]
    ncp = kc_ref.shape[1]
    gw = GROUP * HEAD_DIM
    t = qi * tq + lax.broadcasted_iota(jnp.int32, (tq, ncp), 0)
    n = lax.broadcasted_iota(jnp.int32, (tq, ncp), 1)
    vis = (n * NSA_CMP_STRIDE + (NSA_CMP_LEN - 1)) <= t
    ones = jnp.ones((ncp, LANES), BF16)
    blk = lax.broadcasted_iota(jnp.int32, (LANES, tq), 0)
    blkf = blk.astype(F32)
    cur = (qi * tq + lax.broadcasted_iota(jnp.int32, (LANES, tq), 1)) >> SLC_SHIFT
    forced = (blk == 0) | (blk == cur) | (blk == cur - 1)
    future = blk > cur
    ovt = ovt_ref[...]

    for h in range(NSA_KV_HEADS):
        kc = kc_ref[h]
        va = jnp.concatenate([vc_ref[h], ones], axis=1)
        psum = jnp.zeros((tq, ncp), F32)
        for g in range(GROUP):
            cols = slice(h * gw + g * HEAD_DIM, h * gw + (g + 1) * HEAD_DIM)
            s = jnp.where(vis, _dot_t(q_ref[:, cols], kc), NEG_INF)
            chunks = [s[:, c * LANES:(c + 1) * LANES] for c in range(ncp // LANES)]
            m = jnp.max(functools.reduce(jnp.maximum, chunks), axis=1, keepdims=True)
            p = jnp.where(vis, jnp.exp2(s - m), 0.0)
            pv = _dot(p.astype(BF16), va)
            inv = 1.0 / jnp.maximum(pv[:, HEAD_DIM:], 1e-30)
            ocmp_ref[:, cols] = pv[:, :HEAD_DIM] * inv
            psum = psum + p * jnp.concatenate([inv] * (ncp // LANES), axis=1)

        p_hi = psum.astype(BF16)
        p_lo = (psum - p_hi.astype(F32)).astype(BF16)
        imp = _dot_t(ovt, p_hi) + _dot_t(ovt, p_lo)
        imp = jnp.where(forced, jnp.inf, imp)
        imp = jnp.where(future, -jnp.inf, imp)
        sel = jnp.zeros(imp.shape, F32)
        for _ in range(NSA_SLC_TOPK):
            mx = jnp.max(imp, axis=0, keepdims=True)
            idx = jnp.min(jnp.where(imp == mx, blkf, float(LANES)), axis=0, keepdims=True)
            pick = blkf == idx
            sel = jnp.where(pick, 1.0, sel)
            imp = jnp.where(pick, -jnp.inf, imp)
        sel_ref[h] = jnp.where(sel > 0.5, 0.0, NEG_INF).T.astype(sel_ref.dtype)


def _nsa_cmp_select(proj, kv_c, overlap_t, q_cb):
    s = proj.shape[0]
    tq = BAND_BLOCK
    nq = NSA_KV_HEADS * GROUP
    ncp = kv_c.shape[1]
    assert q_cb % nq == 0 and tq == LANES
    return pl.pallas_call(
        _nsa_cmp_kernel,
        grid=(s // tq,),
        in_specs=[
            pl.BlockSpec((tq, nq * HEAD_DIM), lambda i: (i, q_cb // nq)),
            pl.BlockSpec((NSA_KV_HEADS, ncp, HEAD_DIM), lambda i: (0, 0, 0)),
            pl.BlockSpec((NSA_KV_HEADS, ncp, HEAD_DIM), lambda i: (1, 0, 0)),
            pl.BlockSpec((LANES, ncp), lambda i: (0, 0)),
        ],
        out_specs=[
            pl.BlockSpec((tq, nq * HEAD_DIM), lambda i: (i, 0)),
            pl.BlockSpec((NSA_KV_HEADS, tq, LANES), lambda i: (0, i, 0)),
        ],
        out_shape=[
            jax.ShapeDtypeStruct((s, nq * HEAD_DIM), F32),
            jax.ShapeDtypeStruct((NSA_KV_HEADS, s, LANES), BF16),
        ],
        compiler_params=_cparams("arbitrary"),
        name="nsa_cmp_select",
    )(proj, kv_c, kv_c, overlap_t)


def _nsa_slc_kernel(q_ref, k_ref, kx_ref, v_ref, sel_ref, o_ref, qa_ref, *state, tk):
    qi = pl.program_id(0)
    tq = q_ref.shape[0]
    rows = GROUP * tq
    gw = GROUP * HEAD_DIM
    for h in range(NSA_KV_HEADS):
        selb = sel_ref[h]
        for g in range(GROUP):
            qa_ref[h, g * tq:(g + 1) * tq, 0:HEAD_DIM] = q_ref[:, h * gw + g * HEAD_DIM:h * gw + (g + 1) * HEAD_DIM]
            qa_ref[h, g * tq:(g + 1) * tq, HEAD_DIM:] = selb
    qas = [qa_ref[h] for h in range(NSA_KV_HEADS)]

    last = jnp.right_shift(qi, _log2(tk // tq))
    t = qi * tq + (lax.broadcasted_iota(jnp.int32, (rows, tk), 0) & (tq - 1))
    kpos = last * tk + lax.broadcasted_iota(jnp.int32, (rows, tk), 1)
    outs = _flash_attention(qas, k_ref, kx_ref, v_ref, state, last, tk, kpos <= t)
    for h in range(NSA_KV_HEADS):
        for g in range(GROUP):
            o_ref[:, h * gw + g * HEAD_DIM:h * gw + (g + 1) * HEAD_DIM] = outs[h][g * tq:(g + 1) * tq, :]


def _nsa_selected(proj, selb, q_cb, k_cb, v_cb):
    s = proj.shape[0]
    tq = min(FLASH_TILE, s)
    tk = min(FLASH_TILE, s)
    nq = NSA_KV_HEADS * GROUP
    kvw = NSA_KV_HEADS * HEAD_DIM
    assert tq & (tq - 1) == 0 and s % tk == 0 and q_cb % nq == 0 and k_cb % NSA_KV_HEADS == v_cb % NSA_KV_HEADS == 0
    return pl.pallas_call(
        functools.partial(_nsa_slc_kernel, tk=tk),
        grid=(s // tq,),
        in_specs=[
            pl.BlockSpec((tq, nq * HEAD_DIM), lambda i: (i, q_cb // nq)),
            _resident((s, kvw), lambda i: (0, k_cb // NSA_KV_HEADS)),
            _resident((s, LANES), lambda i: (0, 0)),
            _resident((s, kvw), lambda i: (0, v_cb // NSA_KV_HEADS)),
            pl.BlockSpec((NSA_KV_HEADS, tq, LANES), lambda i: (0, i, 0)),
        ],
        out_specs=pl.BlockSpec((tq, nq * HEAD_DIM), lambda i: (i, 0)),
        out_shape=jax.ShapeDtypeStruct((s, nq * HEAD_DIM), F32),
        scratch_shapes=[pltpu.VMEM((NSA_KV_HEADS, GROUP * tq, 2 * HEAD_DIM), BF16)]
        + _flash_state(NSA_KV_HEADS, GROUP * tq, tk),
        compiler_params=_cparams("arbitrary"),
        name="nsa_selected",
    )(proj, proj, _block_onehot(s, NSA_SLC_BLOCK), proj, selb)


def _nsa_win_kernel(q_ref, k_ref, v_ref, gate_ref, ocmp_ref, oslc_ref, o_ref, qs_ref, *, gate_lane):
    qi = pl.program_id(0)
    tq = q_ref.shape[0]
    gw = GROUP * HEAD_DIM
    rows = GROUP * tq
    n_prev = -(-(NSA_WINDOW - 1) // tq)
    row = lax.broadcasted_iota(jnp.int32, (rows, tq), 0) & (tq - 1)
    col = lax.broadcasted_iota(jnp.int32, (rows, tq), 1)
    ones = jnp.ones((tq, LANES), BF16)
    for h in range(NSA_KV_HEADS):
        for g in range(GROUP):
            qs_ref[h, g * tq:(g + 1) * tq, :] = q_ref[:, h * gw + g * HEAD_DIM:h * gw + (g + 1) * HEAD_DIM]
        q = qs_ref[h]
        cols = slice(h * HEAD_DIM, (h + 1) * HEAD_DIM)
        scores, starts = [], []
        for b in range(n_prev + 1):
            kb = qi - b
            start = pl.multiple_of(jnp.maximum(kb, 0) * tq, tq)
            s = _dot_t(q, k_ref[pl.ds(start, tq), cols])
            dist = row - col + b * tq
            if b * tq - (tq - 1) < 0:
                s = jnp.where(dist >= 0, s, NEG_INF)
            if b * tq + (tq - 1) >= NSA_WINDOW:
                s = jnp.where(dist < NSA_WINDOW, s, NEG_INF)
            if b > 0:
                s = s + jnp.where(kb >= 0, 0.0, NEG_INF)
            scores.append(s)
            starts.append(start)
        m = jnp.max(functools.reduce(jnp.maximum, scores), axis=1, keepdims=True)
        pv = None
        for s, start in zip(scores, starts):
            va = jnp.concatenate([v_ref[pl.ds(start, tq), cols], ones], axis=1)
            term = _dot(jnp.exp2(s - m).astype(BF16), va)
            pv = term if pv is None else pv + term
        o_win = pv[:, :HEAD_DIM] / pv[:, HEAD_DIM:]
        sig = jax.nn.sigmoid(gate_ref[...])
        for g in range(GROUP):
            oc = slice(h * gw + g * HEAD_DIM, h * gw + (g + 1) * HEAD_DIM)
            c = gate_lane + 3 * (h * GROUP + g)
            o = (sig[:, c:c + 1] * ocmp_ref[:, oc]
                 + sig[:, c + 1:c + 2] * oslc_ref[:, oc]
                 + sig[:, c + 2:c + 3] * o_win[g * tq:(g + 1) * tq, :])
            o_ref[:, oc] = o.astype(o_ref.dtype)


def _nsa_window_merge(proj, aux, gate_lane, o_cmp, o_slc, q_cb, k_cb, v_cb):
    s = proj.shape[0]
    tq = BAND_BLOCK
    nq = NSA_KV_HEADS * GROUP
    kvw = NSA_KV_HEADS * HEAD_DIM
    assert tq & (tq - 1) == 0 and q_cb % nq == 0 and k_cb % NSA_KV_HEADS == v_cb % NSA_KV_HEADS == 0
    wide = pl.BlockSpec((tq, nq * HEAD_DIM), lambda i: (i, 0))
    return pl.pallas_call(
        functools.partial(_nsa_win_kernel, gate_lane=gate_lane),
        grid=(s // tq,),
        in_specs=[
            pl.BlockSpec((tq, nq * HEAD_DIM), lambda i: (i, q_cb // nq)),
            _resident((s, kvw), lambda i: (0, k_cb // NSA_KV_HEADS)),
            _resident((s, kvw), lambda i: (0, v_cb // NSA_KV_HEADS)),
            pl.BlockSpec((tq, LANES), lambda i: (i, 0)),
            wide,
            wide,
        ],
        out_specs=wide,
        out_shape=jax.ShapeDtypeStruct((s, nq * HEAD_DIM), BF16),
        scratch_shapes=[pltpu.VMEM((NSA_KV_HEADS, GROUP * tq, HEAD_DIM), BF16)],
        compiler_params=_cparams("arbitrary"),
        name="nsa_window_merge",
    )(proj, proj, proj, aux, o_cmp, o_slc)


def _overlap_matrix(seq):
    n_pad = seq // NSA_CMP_STRIDE
    n_cmp = (seq - NSA_CMP_LEN) // NSA_CMP_STRIDE + 1
    c_start = np.arange(n_pad)[:, None] * NSA_CMP_STRIDE
    s_start = np.arange(LANES)[None, :] * NSA_SLC_BLOCK
    ov = (c_start < s_start + NSA_SLC_BLOCK) & (c_start + NSA_CMP_LEN > s_start)
    ov &= (np.arange(n_pad)[:, None] < n_cmp) & (np.arange(LANES)[None, :] < seq // NSA_SLC_BLOCK)
    return jnp.asarray(ov.T.astype(np.float32), dtype=BF16)


def _ctypes(spec):
    return jnp.asarray(np.concatenate([np.full(n, kind, np.int32) for n, kind in spec]))


def _even_layer(x, mods, norm_g, w_in, sinks, w_out, layer, tables):
    sh1, sc1, g1 = mods
    heads = w_in.shape[2] // HEAD_DIM
    n_moba = 8
    ctypes = _ctypes([(n_moba, CT_ROPE_SCALE), (n_moba, CT_ROPE), (n_moba, CT_PLAIN),
                      (8, CT_ROPE_SCALE), (SWA_KV_HEADS, CT_ROPE), (SWA_KV_HEADS, CT_PLAIN)])
    assert ctypes.shape[0] == heads
    proj = _norm_proj(x, norm_g, sc1, sh1, w_in, ctypes, tables, 512, layer=layer)
    nb = x.shape[0] // MOBA_BLOCK
    kmean = _moba_kmean(proj, 1, n_moba * HEAD_DIM)
    kmean = jnp.pad(kmean, ((0, LANES - nb), (0, 0))).astype(BF16)
    oa = _moba_attention(proj, kmean, n_moba, 0, n_moba, 2 * n_moba)
    ob = _swa_attention(proj, sinks, 3 * n_moba, 4 * n_moba, 4 * n_moba + SWA_KV_HEADS)
    return _out_proj(oa, ob, w_out[layer].astype(BF16), x, g1)


def _odd_layer(x, mods, norm_g, w_in, forget_b, k_pos, k_w1, k_w2, v_pos, v_w1, v_w2, w_out, layer, tables, overlap):
    sh1, sc1, g1 = mods
    s = x.shape[0]
    n_fox = 8
    hw = n_fox * HEAD_DIM
    kvw = NSA_KV_HEADS * HEAD_DIM
    o_fc = 3 * hw
    o_qd = o_fc + n_fox
    o_gd = o_qd + hw + 6 * kvw
    w_main = jnp.concatenate([w_in[:, :o_fc], w_in[:, o_qd:o_gd]], axis=1).astype(BF16)
    n_aux = n_fox + 3 * 8
    w_aux = jnp.concatenate([w_in[:, o_fc:o_qd], w_in[:, o_gd:], jnp.zeros((w_in.shape[0], LANES - n_aux), F32)],
                            axis=1).astype(BF16)
    ctypes = _ctypes([(8, CT_SCALE), (8, CT_PLAIN), (8, CT_PLAIN), (8, CT_ROPE_SCALE),
                      (2, CT_ROPE), (2, CT_PLAIN), (2, CT_ROPE), (2, CT_PLAIN), (2, CT_ROPE), (2, CT_PLAIN)])
    proj, aux = _norm_proj(x, norm_g, sc1, sh1, w_main, ctypes, tables, 512, w_aux=w_aux)

    bias_row = jnp.pad(forget_b.astype(F32), (0, LANES - n_fox)).reshape(1, LANES)
    fox_qx, fox_kx = _fox_cum(aux, bias_row, n_fox)
    oc = _fox_attention(proj, fox_qx, fox_kx, n_fox, 0, 8, 16)

    c0 = 32
    cmp_in = proj[:, c0 * HEAD_DIM:(c0 + 4) * HEAD_DIM].reshape(s, 4, HEAD_DIM).transpose(1, 0, 2)
    cmp_in = cmp_in.reshape(4, s // NSA_CMP_STRIDE, NSA_CMP_STRIDE * HEAD_DIM)
    kv_c = _nsa_compress(
        cmp_in,
        jnp.stack([k_w1, v_w1]).astype(BF16),
        jnp.stack([k_pos.reshape(1, -1), v_pos.reshape(1, -1)]).astype(BF16),
        jnp.stack([k_w2, v_w2]).astype(BF16))
    o_cmp, sel = _nsa_cmp_select(proj, kv_c, overlap, 24)
    o_slc = _nsa_selected(proj, sel, 24, c0 + 4, c0 + 6)
    od = _nsa_window_merge(proj, aux, n_fox, o_cmp, o_slc, 24, c0 + 8, c0 + 10)
    return _out_proj(oc, od, w_out[layer].astype(BF16), x, g1)


def kernel(x, c, norm_mix_g, norm_mlp_g, ada_w, ada_b, mlp_up, mlp_down, even_w_in, even_sinks, even_w_out, odd_w_in, fox_forget_b, nsa_k_pos, nsa_k_w1, nsa_k_w2, nsa_v_pos, nsa_v_w1, nsa_v_w2, odd_w_out, final_norm_g):
    batch, seq, d = x.shape
    assert batch == 1
    depth = ada_w.shape[0]
    tables = _epilogue_tables(seq)
    overlap = _overlap_matrix(seq)
    mod = _ada_mod(c, ada_w, ada_b).reshape(depth, 6, 1, d)
    xs = x[0]
    for i in range(depth):
        sh1, sc1, g1, sh2, sc2, g2 = [mod[i, t] for t in range(6)]
        ng = norm_mix_g[i].reshape(1, d)
        j = i // 2
        if i % 2 == 0:
            xs = _even_layer(xs, (sh1, sc1, g1), ng, even_w_in, even_sinks[j], even_w_out, j, tables)
        else:
            xs = _odd_layer(xs, (sh1, sc1, g1), ng, odd_w_in[j], fox_forget_b[j], nsa_k_pos[j], nsa_k_w1[j],
                            nsa_k_w2[j], nsa_v_pos[j], nsa_v_w1[j], nsa_v_w2[j], odd_w_out, j, tables, overlap)
        xs = _mlp(xs, norm_mlp_g[i].reshape(1, d), sc2, sh2, mlp_up, mlp_down, i, g2,
                  final_norm_g.reshape(1, d), final=(i == depth - 1))
    return xs[None]
```

```python
import functools
import math

import numpy as np
import jax
import jax.numpy as jnp
from jax import lax
from jax.experimental import pallas as pl
from jax.experimental.pallas import tpu as pltpu

HEAD_DIM = 128
ROPE_THETA = 10000.0
NORM_EPS = 1e-6
MOBA_BLOCK = 256
MOBA_TOPK = 3
SWA_KV_HEADS = 2
SWA_WINDOW = 128
NSA_KV_HEADS = 2
NSA_CMP_LEN = 32
NSA_CMP_STRIDE = 16
NSA_SLC_BLOCK = 64
SLC_SHIFT = 6
NSA_SLC_TOPK = 16
NSA_WINDOW = 512
BAND_BLOCK = 128
NEG_INF = -1e30
LOG2E = math.log2(math.e)
SCALE = HEAD_DIM ** -0.5 * LOG2E
LANES = 128
GROUP = 4
VMEM_LIMIT = 48 * 1024 * 1024

F32 = jnp.float32
BF16 = jnp.bfloat16

CT_PLAIN, CT_ROPE, CT_ROPE_SCALE, CT_SCALE = 0, 1, 2, 3


VMEM_LIMIT_MLP = 56 * 1024 * 1024


def _cparams(*sem, vmem=VMEM_LIMIT):
    return pltpu.CompilerParams(dimension_semantics=sem, vmem_limit_bytes=vmem)


def _dot(a, b):
    return jnp.dot(a, b, preferred_element_type=F32)


def _dot_t(a, b):
    return lax.dot_general(a, b, (((1,), (1,)), ((), ())), preferred_element_type=F32)


def _ada_kernel(c_ref, w_ref, b_ref, o_ref):
    c = c_ref[...]
    cond = c * jax.nn.sigmoid(c)
    o_ref[...] = jnp.sum(w_ref[...] * cond, axis=0, keepdims=True) + b_ref[...]


def _ada_mod(c, ada_w, ada_b):
    depth, d, n = ada_w.shape
    tn = 2048
    return pl.pallas_call(
        _ada_kernel,
        grid=(depth, n // tn),
        in_specs=[
            pl.BlockSpec((d, 1), lambda l, j: (0, 0)),
            pl.BlockSpec((None, d, tn), lambda l, j: (l, 0, j)),
            pl.BlockSpec((None, 1, tn), lambda l, j: (l, 0, j)),
        ],
        out_specs=pl.BlockSpec((None, 1, tn), lambda l, j: (l, 0, j)),
        out_shape=jax.ShapeDtypeStruct((depth, 1, n), F32),
        compiler_params=_cparams("arbitrary", "arbitrary"),
        name="ada_mod",
    )(c.reshape(d, 1), ada_w, ada_b.reshape(depth, 1, n))


def _norm_mod(x, g, sc, sh):
    y = x * lax.rsqrt(jnp.mean(x * x, axis=-1, keepdims=True) + NORM_EPS)
    return (y * g) * (1.0 + sc) + sh


PROJ_ROW_CHUNKS = 2


def _proj_kernel(ct_ref, x_ref, g_ref, sc_ref, sh_ref, w_ref, ta_ref, tb_ref, *rest, aux):
    if aux:
        wa_ref, o_ref, oa_ref, h_ref = rest
    else:
        o_ref, h_ref = rest
    j = pl.program_id(1)

    @pl.when(j == 0)
    def _():
        h_ref[...] = _norm_mod(x_ref[...], g_ref[...], sc_ref[...], sh_ref[...]).astype(BF16)
        if aux:
            oa_ref[...] = _dot(h_ref[...], wa_ref[...])

    w = w_ref[...].astype(BF16)
    groups = w.shape[1] // LANES
    tm = h_ref.shape[0]
    chunk = tm // PROJ_ROW_CHUNKS
    for ri in range(PROJ_ROW_CHUNKS):
        rows = slice(ri * chunk, (ri + 1) * chunk)
        y = _dot(h_ref[rows, :], w)
        for gi in range(groups):
            lo, hi = gi * LANES, (gi + 1) * LANES
            yg = y[:, lo:hi]
            ct = ct_ref[j * groups + gi]
            r = yg * ta_ref[ct, rows, :] + pltpu.roll(yg, HEAD_DIM // 2, 1) * tb_ref[ct, rows, :]
            o_ref[rows, lo:hi] = r.astype(o_ref.dtype)


def _epilogue_tables(seq):
    f32 = np.float32
    inv = (f32(1.0) / (f32(ROPE_THETA) ** (np.arange(0, HEAD_DIM, 2, dtype=f32) / f32(HEAD_DIM)))).astype(f32)
    ang = np.arange(seq, dtype=f32)[:, None] * inv[None, :]
    cos, sin = np.cos(ang), np.sin(ang)
    cosf = np.concatenate([cos, cos], axis=-1)
    sinf = np.concatenate([-sin, sin], axis=-1)
    one, zero = np.ones_like(cosf), np.zeros_like(cosf)
    by_kind = {CT_PLAIN: (one, zero), CT_ROPE: (cosf, sinf), CT_ROPE_SCALE: (cosf * f32(SCALE), sinf * f32(SCALE)),
               CT_SCALE: (one * f32(SCALE), zero)}
    return (jnp.asarray(np.stack([by_kind[k][0] for k in range(4)])),
            jnp.asarray(np.stack([by_kind[k][1] for k in range(4)])))


def _norm_proj(x, g, sc, sh, w, ctypes, tables, tn, layer=None, w_aux=None):
    s, d = x.shape
    n = w.shape[-1]
    tm = min(1024, s)
    kinds = tables[0].shape[0]
    if layer is None:
        w_spec = pl.BlockSpec((d, tn), lambda i, j, ct: (0, j))
    else:
        w_spec = pl.BlockSpec((None, d, tn), lambda i, j, ct: (layer, 0, j))
    aux = w_aux is not None
    in_specs = [
        pl.BlockSpec((tm, d), lambda i, j, ct: (i, 0)),
        pl.BlockSpec((1, d), lambda i, j, ct: (0, 0)),
        pl.BlockSpec((1, d), lambda i, j, ct: (0, 0)),
        pl.BlockSpec((1, d), lambda i, j, ct: (0, 0)),
        w_spec,
        pl.BlockSpec((kinds, tm, LANES), lambda i, j, ct: (0, i, 0)),
        pl.BlockSpec((kinds, tm, LANES), lambda i, j, ct: (0, i, 0)),
    ]
    out_specs = pl.BlockSpec((tm, tn), lambda i, j, ct: (i, j))
    out_shape = jax.ShapeDtypeStruct((s, n), BF16)
    operands = [ctypes, x, g, sc, sh, w, *tables]
    if aux:
        in_specs.append(pl.BlockSpec((d, LANES), lambda i, j, ct: (0, 0)))
        out_specs = [out_specs, pl.BlockSpec((tm, LANES), lambda i, j, ct: (i, 0))]
        out_shape = [out_shape, jax.ShapeDtypeStruct((s, LANES), F32)]
        operands.append(w_aux)
    grid_spec = pltpu.PrefetchScalarGridSpec(
        num_scalar_prefetch=1,
        grid=(s // tm, n // tn),
        in_specs=in_specs,
        out_specs=out_specs,
        scratch_shapes=[pltpu.VMEM((tm, d), BF16)],
    )
    return pl.pallas_call(
        functools.partial(_proj_kernel, aux=aux),
        grid_spec=grid_spec,
        out_shape=out_shape,
        compiler_params=_cparams("arbitrary", "arbitrary"),
        name="norm_proj",
    )(*operands)


def _outproj_kernel(a_ref, b_ref, w_ref, x_ref, g_ref, o_ref):
    ha = a_ref.shape[1]
    y = _dot(a_ref[...], w_ref[0:ha, :]) + _dot(b_ref[...], w_ref[ha:, :])
    o_ref[...] = x_ref[...] + g_ref[...] * y


def _out_proj(oa, ob, w, x, gate):
    s, d = x.shape
    ha, hb = oa.shape[1], ob.shape[1]
    tm = min(512, s)
    return pl.pallas_call(
        _outproj_kernel,
        grid=(s // tm,),
        in_specs=[
            pl.BlockSpec((tm, ha), lambda i: (i, 0)),
            pl.BlockSpec((tm, hb), lambda i: (i, 0)),
            _resident((ha + hb, d), lambda i: (0, 0)),
            pl.BlockSpec((tm, d), lambda i: (i, 0)),
            pl.BlockSpec((1, d), lambda i: (0, 0)),
        ],
        out_specs=pl.BlockSpec((tm, d), lambda i: (i, 0)),
        out_shape=jax.ShapeDtypeStruct((s, d), F32),
        compiler_params=_cparams("arbitrary"),
        name="out_proj",
    )(oa, ob, w, x, gate)


def _rms_norm(x, g):
    return x * lax.rsqrt(jnp.mean(x * x, axis=-1, keepdims=True) + NORM_EPS) * g


def _mlp_kernel(x_ref, g_ref, sc_ref, sh_ref, up_ref, down_ref, gate_ref, fg_ref, o_ref, h_ref, *, final):
    f = pl.program_id(1)

    @pl.when(f == 0)
    def _():
        h_ref[...] = _norm_mod(x_ref[...], g_ref[...], sc_ref[...], sh_ref[...]).astype(BF16)
        o_ref[...] = jnp.zeros_like(o_ref)

    hid = jnp.maximum(_dot(h_ref[...], up_ref[...].astype(BF16)), 0.0)
    o_ref[...] += _dot((hid * hid).astype(BF16), down_ref[...].astype(BF16))

    @pl.when(f == pl.num_programs(1) - 1)
    def _():
        y = x_ref[...] + gate_ref[...] * o_ref[...]
        o_ref[...] = _rms_norm(y, fg_ref[...]) if final else y


def _mlp(x, g, sc, sh, up, down, layer, gate, final_g, final):
    s, d = x.shape
    ff = up.shape[2]
    tm, tf = min(1024, s), 512
    return pl.pallas_call(
        functools.partial(_mlp_kernel, final=final),
        grid=(s // tm, ff // tf),
        in_specs=[
            pl.BlockSpec((tm, d), lambda i, f: (i, 0)),
            pl.BlockSpec((1, d), lambda i, f: (0, 0)),
            pl.BlockSpec((1, d), lambda i, f: (0, 0)),
            pl.BlockSpec((1, d), lambda i, f: (0, 0)),
            pl.BlockSpec((None, d, tf), lambda i, f: (layer, 0, f)),
            pl.BlockSpec((None, tf, d), lambda i, f: (layer, f, 0)),
            pl.BlockSpec((1, d), lambda i, f: (0, 0)),
            pl.BlockSpec((1, d), lambda i, f: (0, 0)),
        ],
        out_specs=pl.BlockSpec((tm, d), lambda i, f: (i, 0), pipeline_mode=pl.Buffered(1)),
        out_shape=jax.ShapeDtypeStruct((s, d), F32),
        scratch_shapes=[pltpu.VMEM((tm, d), BF16)],
        compiler_params=_cparams("arbitrary", "arbitrary", vmem=VMEM_LIMIT_MLP),
        name="mlp",
    )(x, g, sc, sh, up, down, gate, final_g)


def _rows(ref, start, size):
    return ref[pl.ds(pl.multiple_of(start, size), size), :]


def _log2(n):
    assert n > 0 and n & (n - 1) == 0
    return n.bit_length() - 1


FLASH_TILE = 512
HEADS_PER_STEP = 4


def _resident(shape, index_map):
    return pl.BlockSpec(shape, index_map, pipeline_mode=pl.Buffered(1))


def _diag_tiles(qi, tq, tk, rows):
    t = qi * tq + (lax.broadcasted_iota(jnp.int32, (rows, tk), 0) & (tq - 1))
    col = lax.broadcasted_iota(jnp.int32, (rows, tk), 1)
    if tq >= tk:
        base, count = qi * (tq // tk), tq // tk
    else:
        base, count = jnp.right_shift(qi, _log2(tk // tq)), 1
    return base, [(base + d) * tk + col <= t for d in range(count)]


def _flash_attention(qas, k_ref, kx_ref, v_ref, state, base, tk, diag_masks):
    m_ref, alpha_ref, p_ref, acc_ref = state
    chains = range(len(qas))
    ones = jnp.ones((tk, LANES), BF16)

    def score(start, mask, first):
        rows = pl.ds(pl.multiple_of(start, tk), tk)
        for c in chains:
            ka = jnp.concatenate([k_ref[rows, c * HEAD_DIM:(c + 1) * HEAD_DIM], kx_ref[rows, :]], axis=1)
            s = _dot_t(qas[c], ka)
            if mask is not None:
                s = jnp.where(mask, s, NEG_INF)
            chunks = [s[:, i * LANES:(i + 1) * LANES] for i in range(tk // LANES)]
            m_cur = jnp.max(functools.reduce(jnp.maximum, chunks), axis=1, keepdims=True)
            if first:
                m_new = jnp.broadcast_to(m_cur, (s.shape[0], LANES))
            else:
                m_prev = m_ref[c]
                m_new = jnp.maximum(m_prev, m_cur)
                alpha_ref[c] = jnp.exp2(m_prev - m_new)
            p_ref[c] = jnp.concatenate([jnp.exp2(ch - m_new) for ch in chunks], axis=1).astype(BF16)
            m_ref[c] = m_new

    def accumulate(start):
        rows = pl.ds(pl.multiple_of(start, tk), tk)
        for c in chains:
            va = jnp.concatenate([v_ref[rows, c * HEAD_DIM:(c + 1) * HEAD_DIM], ones], axis=1)
            alpha = alpha_ref[c]
            acc_ref[c] = jnp.concatenate([alpha, alpha], axis=1) * acc_ref[c] + _dot(p_ref[c], va)

    for c in chains:
        acc_ref[c] = jnp.zeros(acc_ref.shape[1:], F32)
        alpha_ref[c] = jnp.ones(alpha_ref.shape[1:], F32)
    score(base * tk, diag_masks[0], True)
    for d in range(1, len(diag_masks)):
        accumulate((base + d - 1) * tk)
        score((base + d) * tk, diag_masks[d], False)
    pending = base + len(diag_masks) - 1

    def body(p, carry):
        accumulate(jnp.where(p == 0, pending, p - 1) * tk)
        score(p * tk, None, False)
        return carry

    lax.fori_loop(0, base, body, 0)
    accumulate(jnp.where(base == 0, pending, base - 1) * tk)
    return [acc_ref[c][:, :HEAD_DIM] / acc_ref[c][:, HEAD_DIM:] for c in chains]


def _flash_state(chains, rows, tk):
    return [pltpu.VMEM((chains, rows, LANES), F32), pltpu.VMEM((chains, rows, LANES), F32),
            pltpu.VMEM((chains, rows, tk), BF16), pltpu.VMEM((chains, rows, 2 * HEAD_DIM), F32)]


def _block_onehot(seq, block):
    return jnp.asarray((np.arange(seq)[:, None] // block == np.arange(LANES)[None, :]).astype(np.float32), dtype=BF16)


def _kmean_kernel(k_ref, o_ref):
    k = k_ref[...].astype(F32)
    o_ref[...] = jnp.sum(k, axis=0, keepdims=True) * (1.0 / k.shape[0])


def _moba_kmean(proj, col_block, width):
    s = proj.shape[0]
    nb = s // MOBA_BLOCK
    out = pl.pallas_call(
        _kmean_kernel,
        grid=(nb,),
        in_specs=[pl.BlockSpec((MOBA_BLOCK, width), lambda b: (b, col_block))],
        out_specs=pl.BlockSpec((None, 1, width), lambda b: (b, 0, 0)),
        out_shape=jax.ShapeDtypeStruct((nb, 1, width), F32),
        compiler_params=_cparams("arbitrary"),
        name="moba_kmean",
    )(proj)
    return out.reshape(nb, width)


def _moba_kernel(q_ref, k_ref, kx_ref, v_ref, km_ref, o_ref, *state, tk, hp):
    qi = pl.program_id(1)
    tq = q_ref.shape[0]
    blk = lax.broadcasted_iota(jnp.int32, (LANES, tq), 0)
    blkf = blk.astype(F32)
    own = jnp.right_shift(qi * tq + lax.broadcasted_iota(jnp.int32, (LANES, tq), 1), _log2(MOBA_BLOCK))
    ownf = own.astype(F32)

    qas = []
    for hh in range(hp):
        cols = slice(hh * HEAD_DIM, (hh + 1) * HEAD_DIM)
        q = q_ref[:, cols]
        g = jnp.where(blk < own, _dot_t(km_ref[:, cols], q), NEG_INF)
        sel = jnp.zeros((LANES, tq), F32)
        for _ in range(MOBA_TOPK):
            mx = jnp.max(g, axis=0, keepdims=True)
            idx = jnp.min(jnp.where(g == mx, blkf, float(LANES)), axis=0, keepdims=True)
            pick = blkf == idx
            sel = jnp.where(pick & (idx < ownf), 1.0, sel)
            g = jnp.where(pick, -jnp.inf, g)
        qx = jnp.where((sel > 0.5) | (blk == own), 0.0, NEG_INF).T.astype(BF16)
        qas.append(jnp.concatenate([q, qx], axis=1))

    base, diag_masks = _diag_tiles(qi, tq, tk, tq)
    outs = _flash_attention(qas, k_ref, kx_ref, v_ref, state, base, tk, diag_masks)
    for hh in range(hp):
        o_ref[:, hh * HEAD_DIM:(hh + 1) * HEAD_DIM] = outs[hh].astype(o_ref.dtype)


def _moba_attention(proj, kmean, n_heads, q_cb, k_cb, v_cb):
    s = proj.shape[0]
    tq = tk = min(FLASH_TILE, s)
    hp = HEADS_PER_STEP
    hw = hp * HEAD_DIM
    assert s % tq == 0 and tk % MOBA_BLOCK == 0 and n_heads % hp == 0 and q_cb % hp == k_cb % hp == v_cb % hp == 0
    return pl.pallas_call(
        functools.partial(_moba_kernel, tk=tk, hp=hp),
        grid=(n_heads // hp, s // tq),
        in_specs=[
            pl.BlockSpec((tq, hw), lambda h, i: (i, q_cb // hp + h)),
            _resident((s, hw), lambda h, i: (0, k_cb // hp + h)),
            _resident((s, LANES), lambda h, i: (0, 0)),
            _resident((s, hw), lambda h, i: (0, v_cb // hp + h)),
            pl.BlockSpec((LANES, hw), lambda h, i: (0, h)),
        ],
        out_specs=pl.BlockSpec((tq, hw), lambda h, i: (i, h)),
        out_shape=jax.ShapeDtypeStruct((s, n_heads * HEAD_DIM), BF16),
        scratch_shapes=_flash_state(hp, tq, tk),
        compiler_params=_cparams("arbitrary", "arbitrary"),
        name="moba_attn",
    )(proj, proj, _block_onehot(s, MOBA_BLOCK), proj, kmean)


def _swa_kernel(sink_ref, q_ref, k_ref, v_ref, o_ref, qs_ref):
    qi = pl.program_id(0)
    tq = q_ref.shape[0]
    tk = 2 * tq
    rows = GROUP * tq
    gw = GROUP * HEAD_DIM
    start = pl.multiple_of(jnp.maximum(qi - 1, 0) * tq, tq)
    t = qi * tq + (lax.broadcasted_iota(jnp.int32, (rows, tk), 0) & (tq - 1))
    kpos = start + lax.broadcasted_iota(jnp.int32, (rows, tk), 1)
    ok = (kpos <= t) & (kpos > t - SWA_WINDOW)
    ones = jnp.ones((tk, LANES), BF16)
    for h in range(SWA_KV_HEADS):
        for g in range(GROUP):
            qs_ref[h, g * tq:(g + 1) * tq, :] = q_ref[:, h * gw + g * HEAD_DIM:h * gw + (g + 1) * HEAD_DIM]
        cols = slice(h * HEAD_DIM, (h + 1) * HEAD_DIM)
        s = jnp.where(ok, _dot_t(qs_ref[h], k_ref[pl.ds(start, tk), cols]), NEG_INF)
        sink = jnp.concatenate([jnp.full((tq, 1), sink_ref[h * GROUP + g] * LOG2E, F32) for g in range(GROUP)],
                               axis=0)
        m = jnp.maximum(jnp.max(jnp.maximum(s[:, :tq], s[:, tq:]), axis=1, keepdims=True), sink)
        p = jnp.exp2(s - m)
        pv = _dot(p.astype(BF16), jnp.concatenate([v_ref[pl.ds(start, tk), cols], ones], axis=1))
        o = pv[:, :HEAD_DIM] / (pv[:, HEAD_DIM:] + jnp.exp2(sink - m))
        for g in range(GROUP):
            o_ref[:, h * gw + g * HEAD_DIM:h * gw + (g + 1) * HEAD_DIM] = o[g * tq:(g + 1) * tq, :].astype(o_ref.dtype)


def _swa_attention(proj, sinks, q_cb, k_cb, v_cb):
    s = proj.shape[0]
    tq = BAND_BLOCK
    nq = SWA_KV_HEADS * GROUP
    kvw = SWA_KV_HEADS * HEAD_DIM
    assert SWA_WINDOW <= tq and tq == LANES and q_cb % nq == 0 and k_cb % SWA_KV_HEADS == v_cb % SWA_KV_HEADS == 0
    return pl.pallas_call(
        _swa_kernel,
        grid=(s // tq,),
        in_specs=[
            pl.BlockSpec(memory_space=pltpu.SMEM),
            pl.BlockSpec((tq, nq * HEAD_DIM), lambda i: (i, q_cb // nq)),
            _resident((s, kvw), lambda i: (0, k_cb // SWA_KV_HEADS)),
            _resident((s, kvw), lambda i: (0, v_cb // SWA_KV_HEADS)),
        ],
        out_specs=pl.BlockSpec((tq, nq * HEAD_DIM), lambda i: (i, 0)),
        out_shape=jax.ShapeDtypeStruct((s, nq * HEAD_DIM), BF16),
        scratch_shapes=[pltpu.VMEM((SWA_KV_HEADS, GROUP * tq, HEAD_DIM), BF16)],
        compiler_params=_cparams("arbitrary"),
        name="swa_attn",
    )(sinks, proj, proj, proj)


FOX_LANES = 16


def _split3(x):
    x1 = x.astype(BF16)
    r1 = x - x1.astype(F32)
    x2 = r1.astype(BF16)
    return x1, x2, (r1 - x2.astype(F32)).astype(BF16)


def _fox_tables(n_heads):
    place = np.zeros((6, LANES, LANES), np.float32)
    ones_q = np.zeros((1, LANES), np.float32)
    ones_k = np.zeros((1, LANES), np.float32)
    for h in range(n_heads):
        for i in range(6):
            place[i, h, FOX_LANES * h + i] = 1.0
        ones_q[0, FOX_LANES * h + 3:FOX_LANES * h + 6] = 1.0
        ones_k[0, FOX_LANES * h:FOX_LANES * h + 3] = 1.0
    return jnp.asarray(place, dtype=BF16), jnp.asarray(ones_q), jnp.asarray(ones_k)


def _fox_cum_kernel(aux_ref, b_ref, place_ref, oq_ref, ok_ref, qx_ref, kx_ref, carry_ref):
    i = pl.program_id(0)

    @pl.when(i == 0)
    def _():
        carry_ref[...] = jnp.zeros_like(carry_ref)

    x = aux_ref[...] + b_ref[...]
    log_f = -(jnp.maximum(-x, 0.0) + jnp.log(1.0 + jnp.exp(-jnp.abs(x))))
    tb = x.shape[0]
    r = lax.broadcasted_iota(jnp.int32, (tb, tb), 0)
    c = lax.broadcasted_iota(jnp.int32, (tb, tb), 1)
    tri = jnp.where(c <= r, 1.0, 0.0).astype(BF16)
    x1, x2, x3 = _split3(log_f)
    cum = _dot(tri, x1) + _dot(tri, x2) + _dot(tri, x3) + carry_ref[...]
    carry_ref[...] = cum[tb - 1:tb, :]
    c1, c2, c3 = _split3(cum * LOG2E)
    qx = _dot(c1, place_ref[0]) + _dot(c2, place_ref[1]) + _dot(c3, place_ref[2]) + oq_ref[...]
    kx = ok_ref[...] - (_dot(c1, place_ref[3]) + _dot(c2, place_ref[4]) + _dot(c3, place_ref[5]))
    qx_ref[...] = qx.astype(BF16)
    kx_ref[...] = kx.astype(BF16)


def _fox_cum(aux, bias_row, n_heads):
    s, w = aux.shape
    tb = min(512, s)
    place, ones_q, ones_k = _fox_tables(n_heads)
    full = lambda shape: pl.BlockSpec(shape, lambda i: (0,) * len(shape))
    return pl.pallas_call(
        _fox_cum_kernel,
        grid=(s // tb,),
        in_specs=[pl.BlockSpec((tb, w), lambda i: (i, 0)), full((1, w)), full(place.shape), full((1, w)), full((1, w))],
        out_specs=[pl.BlockSpec((tb, w), lambda i: (i, 0)), pl.BlockSpec((tb, w), lambda i: (i, 0))],
        out_shape=[jax.ShapeDtypeStruct((s, w), BF16), jax.ShapeDtypeStruct((s, w), BF16)],
        scratch_shapes=[pltpu.VMEM((1, w), F32)],
        compiler_params=_cparams("arbitrary"),
        name="fox_cum",
    )(aux, bias_row, place, ones_q, ones_k)


def _fox_kernel(q_ref, qx_ref, k_ref, kx_ref, v_ref, o_ref, *state, tk, hp):
    hb = pl.program_id(0)
    qi = pl.program_id(1)
    tq = q_ref.shape[0]
    lane = lax.broadcasted_iota(jnp.int32, (tq, LANES), 1)
    qxf = qx_ref[...].astype(F32)
    qas = []
    for hh in range(hp):
        mine = jnp.right_shift(lane, 4) == hb * hp + hh
        qx = jnp.where(mine, qxf, 0.0).astype(BF16)
        qas.append(jnp.concatenate([q_ref[:, hh * HEAD_DIM:(hh + 1) * HEAD_DIM], qx], axis=1))

    base, diag_masks = _diag_tiles(qi, tq, tk, tq)
    outs = _flash_attention(qas, k_ref, kx_ref, v_ref, state, base, tk, diag_masks)
    for hh in range(hp):
        o_ref[:, hh * HEAD_DIM:(hh + 1) * HEAD_DIM] = outs[hh].astype(o_ref.dtype)


def _fox_attention(proj, qx, kx, n_heads, q_cb, k_cb, v_cb):
    s = proj.shape[0]
    tq = tk = min(FLASH_TILE, s)
    hp = HEADS_PER_STEP
    hw = hp * HEAD_DIM
    assert s % tq == 0 and FOX_LANES == 16 and n_heads % hp == 0 and q_cb % hp == k_cb % hp == v_cb % hp == 0
    return pl.pallas_call(
        functools.partial(_fox_kernel, tk=tk, hp=hp),
        grid=(n_heads // hp, s // tq),
        in_specs=[
            pl.BlockSpec((tq, hw), lambda h, i: (i, q_cb // hp + h)),
            pl.BlockSpec((tq, LANES), lambda h, i: (i, 0)),
            _resident((s, hw), lambda h, i: (0, k_cb // hp + h)),
            _resident((s, LANES), lambda h, i: (0, 0)),
            _resident((s, hw), lambda h, i: (0, v_cb // hp + h)),
        ],
        out_specs=pl.BlockSpec((tq, hw), lambda h, i: (i, h)),
        out_shape=jax.ShapeDtypeStruct((s, n_heads * HEAD_DIM), BF16),
        scratch_shapes=_flash_state(hp, tq, tk),
        compiler_params=_cparams("arbitrary", "arbitrary"),
        name="fox_attn",
    )(proj, qx, proj, kx, proj)


def _gelu_tanh(x):
    return 0.5 * x * (1.0 + jnp.tanh(math.sqrt(2.0 / math.pi) * (x + 0.044715 * (x * x * x))))


def _nsa_compress_kernel(a_ref, w1_ref, pos_ref, w2_ref, o_ref):
    a = a_ref[...]
    half = a.shape[1]
    n = a.shape[0]
    lo = _dot(a, w1_ref[0:half, :])
    hi = _dot(a, w1_ref[half:, :])
    pos8 = jnp.broadcast_to(pos_ref[...], (8, pos_ref.shape[1]))
    pterm = _dot(pos8, w1_ref[...])[0:1]
    pre = lo + pltpu.roll(hi, n - 1, 0) + pterm
    o_ref[...] = _dot(_gelu_tanh(pre).astype(BF16), w2_ref[...]).astype(o_ref.dtype)


def _nsa_compress(a, w1, pos, w2):
    n, rows, width = a.shape
    return pl.pallas_call(
        _nsa_compress_kernel,
        grid=(n,),
        in_specs=[
            pl.BlockSpec((None, rows, width), lambda i: (i, 0, 0)),
            pl.BlockSpec((None, 2 * width, HEAD_DIM), lambda i: (i // NSA_KV_HEADS, 0, 0)),
            pl.BlockSpec((None, 1, 2 * width), lambda i: (i // NSA_KV_HEADS, 0, 0)),
            pl.BlockSpec((None, HEAD_DIM, HEAD_DIM), lambda i: (i // NSA_KV_HEADS, 0, 0)),
        ],
        out_specs=pl.BlockSpec((None, rows, HEAD_DIM), lambda i: (i, 0, 0)),
        out_shape=jax.ShapeDtypeStruct((n, rows, HEAD_DIM), BF16),
        compiler_params=_cparams("arbitrary"),
        name="nsa_compress",
    )(a, w1, pos, w2)


def _nsa_cmp_kernel(q_ref, kc_ref, vc_ref, ovt_ref, ocmp_ref, sel_ref):
    qi = pl.program_id(0)
    tq = q_ref.shape[0]
    ncp = kc_ref.shape[1]
    gw = GROUP * HEAD_DIM
    t = qi * tq + lax.broadcasted_iota(jnp.int32, (tq, ncp), 0)
    n = lax.broadcasted_iota(jnp.int32, (tq, ncp), 1)
    vis = (n * NSA_CMP_STRIDE + (NSA_CMP_LEN - 1)) <= t
    ones = jnp.ones((ncp, LANES), BF16)
    blk = lax.broadcasted_iota(jnp.int32, (LANES, tq), 0)
    blkf = blk.astype(F32)
    cur = (qi * tq + lax.broadcasted_iota(jnp.int32, (LANES, tq), 1)) >> SLC_SHIFT
    forced = (blk == 0) | (blk == cur) | (blk == cur - 1)
    future = blk > cur
    ovt = ovt_ref[...]

    for h in range(NSA_KV_HEADS):
        kc = kc_ref[h]
        va = jnp.concatenate([vc_ref[h], ones], axis=1)
        psum = jnp.zeros((tq, ncp), F32)
        for g in range(GROUP):
            cols = slice(h * gw + g * HEAD_DIM, h * gw + (g + 1) * HEAD_DIM)
            s = jnp.where(vis, _dot_t(q_ref[:, cols], kc), NEG_INF)
            chunks = [s[:, c * LANES:(c + 1) * LANES] for c in range(ncp // LANES)]
            m = jnp.max(functools.reduce(jnp.maximum, chunks), axis=1, keepdims=True)
            p = jnp.where(vis, jnp.exp2(s - m), 0.0)
            pv = _dot(p.astype(BF16), va)
            inv = 1.0 / jnp.maximum(pv[:, HEAD_DIM:], 1e-30)
            ocmp_ref[:, cols] = pv[:, :HEAD_DIM] * inv
            psum = psum + p * jnp.concatenate([inv] * (ncp // LANES), axis=1)

        p_hi = psum.astype(BF16)
        p_lo = (psum - p_hi.astype(F32)).astype(BF16)
        imp = _dot_t(ovt, p_hi) + _dot_t(ovt, p_lo)
        imp = jnp.where(forced, jnp.inf, imp)
        imp = jnp.where(future, -jnp.inf, imp)
        sel = jnp.zeros(imp.shape, F32)
        for _ in range(NSA_SLC_TOPK):
            mx = jnp.max(imp, axis=0, keepdims=True)
            idx = jnp.min(jnp.where(imp == mx, blkf, float(LANES)), axis=0, keepdims=True)
            pick = blkf == idx
            sel = jnp.where(pick, 1.0, sel)
            imp = jnp.where(pick, -jnp.inf, imp)
        sel_ref[h] = jnp.where(sel > 0.5, 0.0, NEG_INF).T.astype(sel_ref.dtype)


def _nsa_cmp_select(proj, kv_c, overlap_t, q_cb):
    s = proj.shape[0]
    tq = BAND_BLOCK
    nq = NSA_KV_HEADS * GROUP
    ncp = kv_c.shape[1]
    assert q_cb % nq == 0 and tq == LANES
    return pl.pallas_call(
        _nsa_cmp_kernel,
        grid=(s // tq,),
        in_specs=[
            pl.BlockSpec((tq, nq * HEAD_DIM), lambda i: (i, q_cb // nq)),
            pl.BlockSpec((NSA_KV_HEADS, ncp, HEAD_DIM), lambda i: (0, 0, 0)),
            pl.BlockSpec((NSA_KV_HEADS, ncp, HEAD_DIM), lambda i: (1, 0, 0)),
            pl.BlockSpec((LANES, ncp), lambda i: (0, 0)),
        ],
        out_specs=[
            pl.BlockSpec((tq, nq * HEAD_DIM), lambda i: (i, 0)),
            pl.BlockSpec((NSA_KV_HEADS, tq, LANES), lambda i: (0, i, 0)),
        ],
        out_shape=[
            jax.ShapeDtypeStruct((s, nq * HEAD_DIM), F32),
            jax.ShapeDtypeStruct((NSA_KV_HEADS, s, LANES), BF16),
        ],
        compiler_params=_cparams("arbitrary"),
        name="nsa_cmp_select",
    )(proj, kv_c, kv_c, overlap_t)


def _nsa_slc_kernel(q_ref, k_ref, kx_ref, v_ref, sel_ref, o_ref, qa_ref, *state, tk):
    qi = pl.program_id(0)
    tq = q_ref.shape[0]
    rows = GROUP * tq
    gw = GROUP * HEAD_DIM
    for h in range(NSA_KV_HEADS):
        selb = sel_ref[h]
        for g in range(GROUP):
            qa_ref[h, g * tq:(g + 1) * tq, 0:HEAD_DIM] = q_ref[:, h * gw + g * HEAD_DIM:h * gw + (g + 1) * HEAD_DIM]
            qa_ref[h, g * tq:(g + 1) * tq, HEAD_DIM:] = selb
    qas = [qa_ref[h] for h in range(NSA_KV_HEADS)]

    base, diag_masks = _diag_tiles(qi, tq, tk, rows)
    outs = _flash_attention(qas, k_ref, kx_ref, v_ref, state, base, tk, diag_masks)
    for h in range(NSA_KV_HEADS):
        for g in range(GROUP):
            o_ref[:, h * gw + g * HEAD_DIM:h * gw + (g + 1) * HEAD_DIM] = outs[h][g * tq:(g + 1) * tq, :]


def _nsa_selected(proj, selb, q_cb, k_cb, v_cb):
    s = proj.shape[0]
    tq = min(FLASH_TILE, s)
    tk = min(FLASH_TILE, s)
    nq = NSA_KV_HEADS * GROUP
    kvw = NSA_KV_HEADS * HEAD_DIM
    assert tq & (tq - 1) == 0 and s % tk == 0 and q_cb % nq == 0 and k_cb % NSA_KV_HEADS == v_cb % NSA_KV_HEADS == 0
    return pl.pallas_call(
        functools.partial(_nsa_slc_kernel, tk=tk),
        grid=(s // tq,),
        in_specs=[
            pl.BlockSpec((tq, nq * HEAD_DIM), lambda i: (i, q_cb // nq)),
            _resident((s, kvw), lambda i: (0, k_cb // NSA_KV_HEADS)),
            _resident((s, LANES), lambda i: (0, 0)),
            _resident((s, kvw), lambda i: (0, v_cb // NSA_KV_HEADS)),
            pl.BlockSpec((NSA_KV_HEADS, tq, LANES), lambda i: (0, i, 0)),
        ],
        out_specs=pl.BlockSpec((tq, nq * HEAD_DIM), lambda i: (i, 0)),
        out_shape=jax.ShapeDtypeStruct((s, nq * HEAD_DIM), F32),
        scratch_shapes=[pltpu.VMEM((NSA_KV_HEADS, GROUP * tq, 2 * HEAD_DIM), BF16)]
        + _flash_state(NSA_KV_HEADS, GROUP * tq, tk),
        compiler_params=_cparams("arbitrary"),
        name="nsa_selected",
    )(proj, proj, _block_onehot(s, NSA_SLC_BLOCK), proj, selb)


def _nsa_win_kernel(q_ref, k_ref, v_ref, gate_ref, ocmp_ref, oslc_ref, o_ref, qs_ref, *, gate_lane):
    qi = pl.program_id(0)
    tq = q_ref.shape[0]
    gw = GROUP * HEAD_DIM
    rows = GROUP * tq
    n_prev = -(-(NSA_WINDOW - 1) // tq)
    row = lax.broadcasted_iota(jnp.int32, (rows, tq), 0) & (tq - 1)
    col = lax.broadcasted_iota(jnp.int32, (rows, tq), 1)
    ones = jnp.ones((tq, LANES), BF16)
    for h in range(NSA_KV_HEADS):
        for g in range(GROUP):
            qs_ref[h, g * tq:(g + 1) * tq, :] = q_ref[:, h * gw + g * HEAD_DIM:h * gw + (g + 1) * HEAD_DIM]
        q = qs_ref[h]
        cols = slice(h * HEAD_DIM, (h + 1) * HEAD_DIM)
        scores, starts = [], []
        for b in range(n_prev + 1):
            kb = qi - b
            start = pl.multiple_of(jnp.maximum(kb, 0) * tq, tq)
            s = _dot_t(q, k_ref[pl.ds(start, tq), cols])
            dist = row - col + b * tq
            if b * tq - (tq - 1) < 0:
                s = jnp.where(dist >= 0, s, NEG_INF)
            if b * tq + (tq - 1) >= NSA_WINDOW:
                s = jnp.where(dist < NSA_WINDOW, s, NEG_INF)
            if b > 0:
                s = s + jnp.where(kb >= 0, 0.0, NEG_INF)
            scores.append(s)
            starts.append(start)
        m = jnp.max(functools.reduce(jnp.maximum, scores), axis=1, keepdims=True)
        pv = None
        for s, start in zip(scores, starts):
            va = jnp.concatenate([v_ref[pl.ds(start, tq), cols], ones], axis=1)
            term = _dot(jnp.exp2(s - m).astype(BF16), va)
            pv = term if pv is None else pv + term
        o_win = pv[:, :HEAD_DIM] / pv[:, HEAD_DIM:]
        sig = jax.nn.sigmoid(gate_ref[...])
        for g in range(GROUP):
            oc = slice(h * gw + g * HEAD_DIM, h * gw + (g + 1) * HEAD_DIM)
            c = gate_lane + 3 * (h * GROUP + g)
            o = (sig[:, c:c + 1] * ocmp_ref[:, oc]
                 + sig[:, c + 1:c + 2] * oslc_ref[:, oc]
                 + sig[:, c + 2:c + 3] * o_win[g * tq:(g + 1) * tq, :])
            o_ref[:, oc] = o.astype(o_ref.dtype)


def _nsa_window_merge(proj, aux, gate_lane, o_cmp, o_slc, q_cb, k_cb, v_cb):
    s = proj.shape[0]
    tq = BAND_BLOCK
    nq = NSA_KV_HEADS * GROUP
    kvw = NSA_KV_HEADS * HEAD_DIM
    assert tq & (tq - 1) == 0 and q_cb % nq == 0 and k_cb % NSA_KV_HEADS == v_cb % NSA_KV_HEADS == 0
    wide = pl.BlockSpec((tq, nq * HEAD_DIM), lambda i: (i, 0))
    return pl.pallas_call(
        functools.partial(_nsa_win_kernel, gate_lane=gate_lane),
        grid=(s // tq,),
        in_specs=[
            pl.BlockSpec((tq, nq * HEAD_DIM), lambda i: (i, q_cb // nq)),
            _resident((s, kvw), lambda i: (0, k_cb // NSA_KV_HEADS)),
            _resident((s, kvw), lambda i: (0, v_cb // NSA_KV_HEADS)),
            pl.BlockSpec((tq, LANES), lambda i: (i, 0)),
            wide,
            wide,
        ],
        out_specs=wide,
        out_shape=jax.ShapeDtypeStruct((s, nq * HEAD_DIM), BF16),
        scratch_shapes=[pltpu.VMEM((NSA_KV_HEADS, GROUP * tq, HEAD_DIM), BF16)],
        compiler_params=_cparams("arbitrary"),
        name="nsa_window_merge",
    )(proj, proj, proj, aux, o_cmp, o_slc)


def _overlap_matrix(seq):
    n_pad = seq // NSA_CMP_STRIDE
    n_cmp = (seq - NSA_CMP_LEN) // NSA_CMP_STRIDE + 1
    c_start = np.arange(n_pad)[:, None] * NSA_CMP_STRIDE
    s_start = np.arange(LANES)[None, :] * NSA_SLC_BLOCK
    ov = (c_start < s_start + NSA_SLC_BLOCK) & (c_start + NSA_CMP_LEN > s_start)
    ov &= (np.arange(n_pad)[:, None] < n_cmp) & (np.arange(LANES)[None, :] < seq // NSA_SLC_BLOCK)
    return jnp.asarray(ov.T.astype(np.float32), dtype=BF16)


def _ctypes(spec):
    return jnp.asarray(np.concatenate([np.full(n, kind, np.int32) for n, kind in spec]))


def _even_layer(x, mods, norm_g, w_in, sinks, w_out, layer, tables):
    sh1, sc1, g1 = mods
    heads = w_in.shape[2] // HEAD_DIM
    n_moba = 8
    ctypes = _ctypes([(n_moba, CT_ROPE_SCALE), (n_moba, CT_ROPE), (n_moba, CT_PLAIN),
                      (8, CT_ROPE_SCALE), (SWA_KV_HEADS, CT_ROPE), (SWA_KV_HEADS, CT_PLAIN)])
    assert ctypes.shape[0] == heads
    proj = _norm_proj(x, norm_g, sc1, sh1, w_in, ctypes, tables, 512, layer=layer)
    nb = x.shape[0] // MOBA_BLOCK
    kmean = _moba_kmean(proj, 1, n_moba * HEAD_DIM)
    kmean = jnp.pad(kmean, ((0, LANES - nb), (0, 0))).astype(BF16)
    oa = _moba_attention(proj, kmean, n_moba, 0, n_moba, 2 * n_moba)
    ob = _swa_attention(proj, sinks, 3 * n_moba, 4 * n_moba, 4 * n_moba + SWA_KV_HEADS)
    return _out_proj(oa, ob, w_out[layer].astype(BF16), x, g1)


def _odd_layer(x, mods, norm_g, w_in, forget_b, k_pos, k_w1, k_w2, v_pos, v_w1, v_w2, w_out, layer, tables, overlap):
    sh1, sc1, g1 = mods
    s = x.shape[0]
    n_fox = 8
    hw = n_fox * HEAD_DIM
    kvw = NSA_KV_HEADS * HEAD_DIM
    o_fc = 3 * hw
    o_qd = o_fc + n_fox
    o_gd = o_qd + hw + 6 * kvw
    w_main = jnp.concatenate([w_in[:, :o_fc], w_in[:, o_qd:o_gd]], axis=1).astype(BF16)
    n_aux = n_fox + 3 * 8
    w_aux = jnp.concatenate([w_in[:, o_fc:o_qd], w_in[:, o_gd:], jnp.zeros((w_in.shape[0], LANES - n_aux), F32)],
                            axis=1).astype(BF16)
    ctypes = _ctypes([(8, CT_SCALE), (8, CT_PLAIN), (8, CT_PLAIN), (8, CT_ROPE_SCALE),
                      (2, CT_ROPE), (2, CT_PLAIN), (2, CT_ROPE), (2, CT_PLAIN), (2, CT_ROPE), (2, CT_PLAIN)])
    proj, aux = _norm_proj(x, norm_g, sc1, sh1, w_main, ctypes, tables, 512, w_aux=w_aux)

    bias_row = jnp.pad(forget_b.astype(F32), (0, LANES - n_fox)).reshape(1, LANES)
    fox_qx, fox_kx = _fox_cum(aux, bias_row, n_fox)
    oc = _fox_attention(proj, fox_qx, fox_kx, n_fox, 0, 8, 16)

    c0 = 32
    cmp_in = proj[:, c0 * HEAD_DIM:(c0 + 4) * HEAD_DIM].reshape(s, 4, HEAD_DIM).transpose(1, 0, 2)
    cmp_in = cmp_in.reshape(4, s // NSA_CMP_STRIDE, NSA_CMP_STRIDE * HEAD_DIM)
    kv_c = _nsa_compress(
        cmp_in,
        jnp.stack([k_w1, v_w1]).astype(BF16),
        jnp.stack([k_pos.reshape(1, -1), v_pos.reshape(1, -1)]).astype(BF16),
        jnp.stack([k_w2, v_w2]).astype(BF16))
    o_cmp, sel = _nsa_cmp_select(proj, kv_c, overlap, 24)
    o_slc = _nsa_selected(proj, sel, 24, c0 + 4, c0 + 6)
    od = _nsa_window_merge(proj, aux, n_fox, o_cmp, o_slc, 24, c0 + 8, c0 + 10)
    return _out_proj(oc, od, w_out[layer].astype(BF16), x, g1)


def kernel(x, c, norm_mix_g, norm_mlp_g, ada_w, ada_b, mlp_up, mlp_down, even_w_in, even_sinks, even_w_out, odd_w_in, fox_forget_b, nsa_k_pos, nsa_k_w1, nsa_k_w2, nsa_v_pos, nsa_v_w1, nsa_v_w2, odd_w_out, final_norm_g):
    batch, seq, d = x.shape
    assert batch == 1
    depth = ada_w.shape[0]
    tables = _epilogue_tables(seq)
    overlap = _overlap_matrix(seq)
    mod = _ada_mod(c, ada_w, ada_b).reshape(depth, 6, 1, d)
    xs = x[0]
    for i in range(depth):
        sh1, sc1, g1, sh2, sc2, g2 = [mod[i, t] for t in range(6)]
        ng = norm_mix_g[i].reshape(1, d)
        j = i // 2
        if i % 2 == 0:
            xs = _even_layer(xs, (sh1, sc1, g1), ng, even_w_in, even_sinks[j], even_w_out, j, tables)
        else:
            xs = _odd_layer(xs, (sh1, sc1, g1), ng, odd_w_in[j], fox_forget_b[j], nsa_k_pos[j], nsa_k_w1[j],
                            nsa_k_w2[j], nsa_v_pos[j], nsa_v_w1[j], nsa_v_w2[j], odd_w_out, j, tables, overlap)
        xs = _mlp(xs, norm_mlp_g[i].reshape(1, d), sc2, sh2, mlp_up, mlp_down, i, g2,
                  final_norm_g.reshape(1, d), final=(i == depth - 1))
    return xs[None]
```
